```python
import math
import jax, jax.numpy as jnp
from jax import lax
import numpy as np

D_MODEL = 2048
BATCH = 4
SEQ = 2048
DEPTH = 4
DEC_BATCH = 8
DEC_SEQ = 8
PAST_LEN = 16384
PAGE_SIZE = 128

N_A = DEPTH // 2
N_B = DEPTH - N_A
RWKV_HEAD = 64
RWKV_HEADS = D_MODEL // RWKV_HEAD
D_DECAY_LORA = max(32, int(round(1.8 * D_MODEL ** 0.5 / 32)) * 32)
D_AAA_LORA = max(32, int(round(1.8 * D_MODEL ** 0.5 / 32)) * 32)
D_MV_LORA = max(32, int(round(1.3 * D_MODEL ** 0.5 / 32)) * 32)
D_GATE_LORA = max(32, int(round(0.6 * D_MODEL ** 0.8 / 32)) * 32)
ATT_HEAD = 128
ATT_HEADS = D_MODEL // (2 * ATT_HEAD)
ATT_WIDTH = ATT_HEADS * 2 * ATT_HEAD
D_FF = 4 * D_MODEL
ROPE_THETA = 10000.0
Q_BLOCK = 128
NORM_EPS = 1e-6
GN_EPS = 64e-5
SUBLN_EPS = 1e-5

kernel_name = "yoco_rwkv7_diff_attn_step"


def rms_norm(x, g, eps=NORM_EPS):
    x32 = x.astype(jnp.float32)
    y = x32 * lax.rsqrt(jnp.mean(x32 * x32, axis=-1, keepdims=True) + eps)
    return (y * g.astype(jnp.float32)).astype(x.dtype)


def sq_relu_mlp(x, w1, w2):
    h = jax.nn.relu(x @ w1)
    return (h * h) @ w2


def rope(x, pos):
    half = x.shape[-1] // 2
    inv = jnp.power(ROPE_THETA, -jnp.arange(half, dtype=jnp.float32) / half)
    ang = pos.astype(jnp.float32)[:, None] * inv[None, :]
    cos = jnp.cos(ang)[None, :, None, None, :]
    sin = jnp.sin(ang)[None, :, None, None, :]
    x32 = x.astype(jnp.float32)
    x1, x2 = x32[..., :half], x32[..., half:]
    return jnp.concatenate([x1 * cos - x2 * sin, x2 * cos + x1 * sin], axis=-1).astype(x.dtype)


def wkv7_scan(r, w, k, v, a, b, s0):
    def step(s, inp):
        r_t, w_t, k_t, v_t, a_t, b_t = inp
        sa = jnp.einsum('bhvk,bhk->bhv', s, a_t)
        s = s * w_t[:, :, None, :] + sa[..., None] * b_t[:, :, None, :] + v_t[..., None] * k_t[:, :, None, :]
        y = jnp.einsum('bhvk,bhk->bhv', s, r_t)
        return s, y
    seq = tuple(jnp.swapaxes(t, 0, 1) for t in (r, w, k, v, a, b))
    s_final, y = lax.scan(step, s0, seq)
    return jnp.swapaxes(y, 0, 1), s_final


def rwkv7_time_mix(xn, shift0, s0, v_first, p, vres):
    mu, vec, wr, wk, wv, wo, w1, w2, a1, a2, g1, g2, rk = p
    Bsz, T, D = xn.shape
    H, N = RWKV_HEADS, RWKV_HEAD
    x_prev = jnp.concatenate([shift0[:, None, :].astype(xn.dtype), xn[:, :-1]], axis=1)
    xx = x_prev - xn
    xr, xw, xk, xv, xa, xg = (xn + xx * mu[i] for i in range(6))
    w0, a0, k_k, k_a, lnx_w, lnx_b = (vec[i] for i in range(6))
    r = xr @ wr
    w_raw = -jax.nn.softplus(-(w0 + jnp.tanh(xw @ w1) @ w2)) - 0.5
    k = xk @ wk
    v = xv @ wv
    if vres is None:
        v_first = v
    else:
        v0, v1, v2 = vres
        v = v + (v_first - v) * jax.nn.sigmoid(v0 + (xv @ v1) @ v2)
    a = jax.nn.sigmoid(a0 + (xa @ a1) @ a2)
    g = jax.nn.sigmoid(xg @ g1) @ g2
    hd = lambda t: t.reshape(Bsz, T, H, N).astype(jnp.float32)
    kk = hd(k * k_k)
    kk = kk / jnp.maximum(jnp.sqrt(jnp.sum(kk * kk, axis=-1, keepdims=True)), 1e-12)
    k = k * (1 + (a - 1) * k_a)
    r_h, k_h, v_h, a_h = hd(r), hd(k), hd(v), hd(a)
    decay = jnp.exp(-jnp.exp(hd(w_raw)))
    y, s_new = wkv7_scan(r_h, decay, k_h, v_h, -kk, kk * a_h, s0.astype(jnp.float32))
    mean = jnp.mean(y, axis=-1, keepdims=True)
    var = jnp.mean(jnp.square(y - mean), axis=-1, keepdims=True)
    y = ((y - mean) * lax.rsqrt(var + GN_EPS)).reshape(Bsz, T, D)
    y = y * lnx_w.astype(jnp.float32) + lnx_b.astype(jnp.float32)
    bonus = jnp.sum(r_h * k_h * rk.astype(jnp.float32), axis=-1, keepdims=True) * v_h
    y = (y + bonus.reshape(Bsz, T, D)).astype(xn.dtype)
    out = (y * g) @ wo
    return out, xn[:, -1], s_new.astype(s0.dtype), v_first


def lambda_init(layer_idx):
    return 0.8 - 0.6 * math.exp(-0.3 * layer_idx)


def diff_attn_prompt(q, k, v, lam):
    Bsz, T = q.shape[:2]
    nb = T // Q_BLOCK
    qb = jnp.swapaxes(q.reshape(Bsz, nb, Q_BLOCK, ATT_HEADS, 2, ATT_HEAD), 0, 1)
    k_pos = jnp.arange(T)
    scale = ATT_HEAD ** -0.5

    def block(args):
        i, q_blk = args
        s = jnp.einsum('bqhmd,bkhmd->bhmqk', q_blk, k).astype(jnp.float32) * scale
        q_pos = i * Q_BLOCK + jnp.arange(Q_BLOCK)
        s = jnp.where(k_pos[None, :] <= q_pos[:, None], s, -jnp.inf)
        p = jax.nn.softmax(s, axis=-1).astype(v.dtype)
        o = jnp.einsum('bhmqk,bkhe->bqhme', p, v)
        return o[..., 0, :] - lam * o[..., 1, :]

    out = lax.map(block, (jnp.arange(nb), qb))
    return jnp.swapaxes(out, 0, 1).reshape(Bsz, T, ATT_HEADS, 2 * ATT_HEAD)


def diff_attn_sample(q, k_new, v_new, k_past, v_past, lam):
    Tq = q.shape[1]
    P = k_past.shape[1]
    scale = ATT_HEAD ** -0.5
    s_past = jnp.einsum('bqhmd,bkhmd->bhmqk', q, k_past).astype(jnp.float32) * scale
    s_new = jnp.einsum('bqhmd,bkhmd->bhmqk', q, k_new).astype(jnp.float32) * scale
    causal = jnp.arange(Tq)[None, :] <= jnp.arange(Tq)[:, None]
    s_new = jnp.where(causal, s_new, -jnp.inf)
    p = jax.nn.softmax(jnp.concatenate([s_past, s_new], axis=-1), axis=-1).astype(v_new.dtype)
    o = (jnp.einsum('bhmqk,bkhe->bqhme', p[..., :P], v_past)
         + jnp.einsum('bhmqk,bkhe->bqhme', p[..., P:], v_new))
    return o[..., 0, :] - lam * o[..., 1, :]


def setup_inputs(seed: int = 0) -> dict:
    key = jax.random.key(seed)
    keys = list(jax.random.split(key, 64))

    def nk():
        return keys.pop()

    f32 = jnp.float32

    def normal(shape, scale):
        return jax.random.normal(nk(), shape, f32) * scale

    def gain(shape):
        return 1.0 + normal(shape, 0.05)

    D, H, N = D_MODEL, RWKV_HEADS, RWKV_HEAD
    n_pages = PAST_LEN // PAGE_SIZE
    n_used = DEC_BATCH * n_pages
    n_pool = n_used + n_used // 4
    perm = jax.random.permutation(nk(), n_pool).astype(jnp.int32)
    page_table = perm[:n_used].reshape(DEC_BATCH, n_pages)

    rwkv_vec = jnp.stack([
        jax.random.uniform(nk(), (N_A, D), f32, -5.0, -1.0),
        normal((N_A, D), 0.1),
        0.85 + normal((N_A, D), 0.02),
        1.0 + normal((N_A, D), 0.02),
        1.0 + normal((N_A, D), 0.02),
        normal((N_A, D), 0.02),
    ], axis=1)

    return {
        "x_prompt": normal((BATCH, SEQ, D), 1.0),
        "x_sample": normal((DEC_BATCH, DEC_SEQ, D), 1.0),
        "cache_k": normal((n_pool, PAGE_SIZE, ATT_HEADS, 2, ATT_HEAD), 1.0),
        "cache_v": normal((n_pool, PAGE_SIZE, ATT_HEADS, 2 * ATT_HEAD), 1.0),
        "state_shift": normal((N_A, DEC_BATCH, D), 1.0),
        "state_wkv": normal((N_A, DEC_BATCH, H, N, N), 0.1),
        "page_table": page_table,
        "norm_mix": gain((DEPTH, 2, D)),
        "norm_ffn": gain((DEPTH, 2, D)),
        "rwkv_mu": jax.random.uniform(nk(), (N_A, 6, D), f32),
        "rwkv_vec": rwkv_vec,
        "rwkv_wr": normal((N_A, D, D), D ** -0.5),
        "rwkv_wk": normal((N_A, D, D), D ** -0.5),
        "rwkv_wv": normal((N_A, D, D), D ** -0.5),
        "rwkv_wo": normal((N_A, D, D), D ** -0.5),
        "rwkv_w1": normal((N_A, D, D_DECAY_LORA), D ** -0.5),
        "rwkv_w2": normal((N_A, D_DECAY_LORA, D), 0.5 * D_DECAY_LORA ** -0.5),
        "rwkv_a1": normal((N_A, D, D_AAA_LORA), D ** -0.5),
        "rwkv_a2": normal((N_A, D_AAA_LORA, D), 0.5 * D_AAA_LORA ** -0.5),
        "rwkv_v0": normal((N_A - 1, D), 0.1),
        "rwkv_v1": normal((N_A - 1, D, D_MV_LORA), D ** -0.5),
        "rwkv_v2": normal((N_A - 1, D_MV_LORA, D), 0.5 * D_MV_LORA ** -0.5),
        "rwkv_g1": normal((N_A, D, D_GATE_LORA), D ** -0.5),
        "rwkv_g2": normal((N_A, D_GATE_LORA, D), D_GATE_LORA ** -0.5),
        "rwkv_rk": normal((N_A, H, N), 0.1),
        "kv_norm": gain((D,)),
        "kv_wk": normal((D, ATT_WIDTH), D ** -0.5),
        "kv_wv": normal((D, ATT_WIDTH), D ** -0.5),
        "attn_wq": normal((N_B, D, ATT_WIDTH), D ** -0.5),
        "attn_wo": normal((N_B, ATT_WIDTH, D), ATT_WIDTH ** -0.5),
        "attn_lambda": normal((N_B, 4, ATT_HEAD), 0.1),
        "attn_subln": gain((N_B, 2 * ATT_HEAD)),
        "ffn_w1": normal((DEPTH, D, D_FF), D ** -0.5),
        "ffn_w2": normal((DEPTH, D_FF, D), D_FF ** -0.5),
    }


def reference(x_prompt, x_sample, cache_k, cache_v, state_shift, state_wkv, page_table,
              norm_mix, norm_ffn, rwkv_mu, rwkv_vec, rwkv_wr, rwkv_wk, rwkv_wv, rwkv_wo,
              rwkv_w1, rwkv_w2, rwkv_a1, rwkv_a2, rwkv_v0, rwkv_v1, rwkv_v2, rwkv_g1, rwkv_g2,
              rwkv_rk, kv_norm, kv_wk, kv_wv, attn_wq, attn_wo, attn_lambda, attn_subln,
              ffn_w1, ffn_w2):

    def trunk(x, shift0, wkv0, pos, attend):
        Bsz, T, _ = x.shape
        new_shift, new_wkv = [], []
        v_first = None
        k_sh = None
        v_sh = None
        for l in range(DEPTH):
            xn = rms_norm(x, norm_mix[l, 0])
            if l < N_A:
                vres = None if l == 0 else (rwkv_v0[l - 1], rwkv_v1[l - 1], rwkv_v2[l - 1])
                p = (rwkv_mu[l], rwkv_vec[l], rwkv_wr[l], rwkv_wk[l], rwkv_wv[l], rwkv_wo[l],
                     rwkv_w1[l], rwkv_w2[l], rwkv_a1[l], rwkv_a2[l], rwkv_g1[l], rwkv_g2[l], rwkv_rk[l])
                h, sh, st, v_first = rwkv7_time_mix(xn, shift0[l], wkv0[l], v_first, p, vres)
                new_shift.append(sh)
                new_wkv.append(st)
            else:
                j = l - N_A
                q = rope((xn @ attn_wq[j]).reshape(Bsz, T, ATT_HEADS, 2, ATT_HEAD), pos)
                lp = attn_lambda[j].astype(jnp.float32)
                lam_i = lambda_init(l)
                lam = jnp.exp(jnp.sum(lp[0] * lp[1])) - jnp.exp(jnp.sum(lp[2] * lp[3])) + lam_i
                o = attend(q, k_sh, v_sh, lam.astype(x.dtype))
                o = rms_norm(o, attn_subln[j], SUBLN_EPS) * (1.0 - lam_i)
                h = o.reshape(Bsz, T, ATT_WIDTH) @ attn_wo[j]
            x = x + rms_norm(h, norm_mix[l, 1])
            x = x + rms_norm(sq_relu_mlp(rms_norm(x, norm_ffn[l, 0]), ffn_w1[l], ffn_w2[l]), norm_ffn[l, 1])
            if l == N_A - 1:
                kv_in = rms_norm(x, kv_norm)
                k_sh = rope((kv_in @ kv_wk).reshape(Bsz, T, ATT_HEADS, 2, ATT_HEAD), pos)
                v_sh = (kv_in @ kv_wv).reshape(Bsz, T, ATT_HEADS, 2 * ATT_HEAD)
        return x, k_sh, v_sh, jnp.stack(new_shift), jnp.stack(new_wkv)

    Bp, Tp, D = x_prompt.shape
    shift0_p = jnp.zeros((N_A, Bp, D), x_prompt.dtype)
    wkv0_p = jnp.zeros((N_A, Bp, RWKV_HEADS, RWKV_HEAD, RWKV_HEAD), state_wkv.dtype)
    pos_p = jnp.arange(Tp, dtype=jnp.int32)
    y_prompt, k_prompt, v_prompt, shift_prompt, wkv_prompt = trunk(
        x_prompt, shift0_p, wkv0_p, pos_p, diff_attn_prompt)

    Bd, n_pages = page_table.shape
    page = cache_k.shape[1]
    past_len = n_pages * page
    k_past = cache_k[page_table].reshape(Bd, past_len, ATT_HEADS, 2, ATT_HEAD)
    v_past = cache_v[page_table].reshape(Bd, past_len, ATT_HEADS, 2 * ATT_HEAD)
    pos_s = past_len + jnp.arange(x_sample.shape[1], dtype=jnp.int32)
    attend_s = lambda q, k, v, lam: diff_attn_sample(q, k, v, k_past, v_past, lam)
    y_sample, k_sample, v_sample, shift_sample, wkv_sample = trunk(
        x_sample, state_shift, state_wkv, pos_s, attend_s)

    return (y_prompt, y_sample, k_prompt, v_prompt, shift_prompt, wkv_prompt,
            k_sample, v_sample, shift_sample, wkv_sample)
```

```python
import functools
import math

import jax
import jax.numpy as jnp
from jax import lax
from jax.experimental import pallas as pl
from jax.experimental.pallas import tpu as pltpu

F32 = jnp.float32
BF16 = jnp.bfloat16

RWKV_HEAD = 64
ATT_HEAD = 128
NORM_EPS = 1e-6
GN_EPS = 64e-5
SUBLN_EPS = 1e-5
ROPE_THETA = 10000.0

LANES = 128
GROUP = 256
HEADS_PER_GROUP = GROUP // RWKV_HEAD
CHUNK = 64
VMEM_LIMIT = 48 * 1024 * 1024
NEG_BIG = -1e30


def _cparams(*sem):
    return pltpu.CompilerParams(dimension_semantics=sem, vmem_limit_bytes=VMEM_LIMIT)


def _dot(a, b):
    return jnp.dot(a, b, preferred_element_type=F32)


def _dot_nt(a, b):
    return lax.dot_general(a, b, (((1,), (1,)), ((), ())), preferred_element_type=F32)


def _rms(x, eps):
    return x * lax.rsqrt(jnp.mean(x * x, axis=-1, keepdims=True) + eps)


def _mm_kernel(x_ref, w_ref, *rest, act, rope):
    o_ref = rest[-1]
    acc = _dot(x_ref[...].astype(BF16), w_ref[...])
    if act == "tanh":
        acc = jnp.tanh(acc)
    elif act == "sigmoid":
        acc = jax.nn.sigmoid(acc)
    if rope:
        cos = rest[0][...]
        sin = rest[1][...]
        for c in range(acc.shape[1] // LANES):
            blk = acc[:, c * LANES:(c + 1) * LANES]
            rot = pltpu.roll(blk, LANES // 2, 1)
            o_ref[:, c * LANES:(c + 1) * LANES] = (blk * cos + rot * sin).astype(o_ref.dtype)
    else:
        o_ref[...] = acc.astype(o_ref.dtype)


def _planes(x):
    return x if isinstance(x, tuple) else (x[None], 0)


def _matmul(x, w, *, out_dtype=F32, act=None, rope=None, tm=512, tn=512):
    x, p0 = _planes(x)
    squeeze = w.ndim == 2
    if squeeze:
        w = w[None]
    _, M, K = x.shape
    G, _, N = w.shape
    tm = min(tm, M)
    if rope is not None:
        tm = min(tm, rope[0].shape[0])
        assert rope[0].shape[0] % tm == 0
    tn = min(tn, N)
    assert M % tm == 0 and N % tn == 0
    in_specs = [pl.BlockSpec((None, tm, K), lambda g, i, j: (p0 + g, i, 0)),
                pl.BlockSpec((None, K, tn), lambda g, i, j: (g, 0, j))]
    args = [x, w]
    if rope is not None:
        cos, sin = rope
        nblk = cos.shape[0] // tm
        spec = pl.BlockSpec((tm, LANES), lambda g, i, j: (i % nblk, 0))
        in_specs += [spec, spec]
        args += [cos, sin]
    out = pl.pallas_call(
        functools.partial(_mm_kernel, act=act, rope=rope is not None),
        grid=(G, M // tm, N // tn),
        in_specs=in_specs,
        out_specs=pl.BlockSpec((None, tm, tn), lambda g, i, j: (g, i, j)),
        out_shape=jax.ShapeDtypeStruct((G, M, N), out_dtype),
        compiler_params=_cparams("parallel", "parallel", "arbitrary"),
        name="matmul",
    )(*args)
    return out[0] if squeeze else out


def _lora_kernel(x_ref, w1_ref, w2_ref, o_ref, *, act):
    h = _dot(x_ref[...].astype(BF16), w1_ref[...])
    if act == "tanh":
        h = jnp.tanh(h)
    elif act == "sigmoid":
        h = jax.nn.sigmoid(h)
    o_ref[...] = _dot(h.astype(BF16), w2_ref[...])


def _lora(x, w1, w2, *, act=None, tm=512):
    x, p0 = _planes(x)
    _, M, K = x.shape
    R = w1.shape[1]
    N = w2.shape[1]
    tm = min(tm, M)
    return pl.pallas_call(
        functools.partial(_lora_kernel, act=act),
        grid=(M // tm,),
        in_specs=[pl.BlockSpec((None, tm, K), lambda i: (p0, i, 0)),
                  pl.BlockSpec((K, R), lambda i: (0, 0)),
                  pl.BlockSpec((R, N), lambda i: (0, 0))],
        out_specs=pl.BlockSpec((tm, N), lambda i: (i, 0)),
        out_shape=jax.ShapeDtypeStruct((M, N), F32),
        compiler_params=_cparams("parallel"),
        name="lora",
    )(x, w1, w2)


def _pad_lora(w1, w2):
    r = w1.shape[1]
    rp = -(-r // LANES) * LANES
    return (jnp.pad(w1, ((0, 0), (0, rp - r))).astype(BF16),
            jnp.pad(w2, ((0, rp - r), (0, 0))).astype(BF16))


def _mlp_kernel(x_ref, w1_ref, w2_ref, o_ref):
    f = pl.program_id(1)
    h = jnp.maximum(_dot(x_ref[...], w1_ref[...]), 0.0)
    part = _dot((h * h).astype(BF16), w2_ref[...])

    @pl.when(f == 0)
    def _():
        o_ref[...] = part

    @pl.when(f > 0)
    def _():
        o_ref[...] += part


def _mlp(x, w1, w2, *, tm=512, tf=512):
    x, p0 = _planes(x)
    _, M, D = x.shape
    F = w1.shape[1]
    tm = min(tm, M)
    return pl.pallas_call(
        _mlp_kernel,
        grid=(M // tm, F // tf),
        in_specs=[pl.BlockSpec((None, tm, D), lambda i, f: (p0, i, 0)),
                  pl.BlockSpec((D, tf), lambda i, f: (0, f)),
                  pl.BlockSpec((tf, D), lambda i, f: (f, 0))],
        out_specs=pl.BlockSpec((tm, D), lambda i, f: (i, 0)),
        out_shape=jax.ShapeDtypeStruct((M, D), F32),
        compiler_params=_cparams("parallel", "arbitrary"),
        name="mlp",
    )(x, w1, w2)


MIX_ORDER = (0, 2, 3, 1, 4, 5)
PL_R, PL_K, PL_V, PL_W, PL_A, PL_G = range(6)


def _prep_kernel(x_ref, sh_ref, g_ref, mu_ref, mix_ref, last_ref, carry_ref):
    i = pl.program_id(1)
    xn = _rms(x_ref[...], NORM_EPS) * g_ref[...]
    tr = xn.shape[0]

    @pl.when(i == 0)
    def _():
        carry_ref[...] = sh_ref[...]

    row = lax.broadcasted_iota(jnp.int32, xn.shape, 0)
    xprev = jnp.where(row == 0, carry_ref[...], pltpu.roll(xn, 1, 0))
    xx = xprev - xn
    for plane, m in enumerate(MIX_ORDER):
        mix_ref[plane] = (xn + xx * mu_ref[m:m + 1, :]).astype(mix_ref.dtype)
    last = xn[tr - 1:tr, :]
    carry_ref[...] = last
    last_ref[...] = last


def _prep(x, shift0, g, mu, *, B, T, out_dtype):
    M, D = x.shape
    tr = min(256, T)
    nt = T // tr
    mix, last = pl.pallas_call(
        _prep_kernel,
        grid=(B, nt),
        in_specs=[pl.BlockSpec((tr, D), lambda b, i: (b * nt + i, 0)),
                  pl.BlockSpec((None, 1, D), lambda b, i: (b, 0, 0)),
                  pl.BlockSpec((1, D), lambda b, i: (0, 0)),
                  pl.BlockSpec((6, D), lambda b, i: (0, 0))],
        out_specs=[pl.BlockSpec((6, tr, D), lambda b, i: (0, b * nt + i, 0)),
                   pl.BlockSpec((None, 1, D), lambda b, i: (b, 0, 0))],
        out_shape=[jax.ShapeDtypeStruct((6, M, D), out_dtype),
                   jax.ShapeDtypeStruct((B, 1, D), F32)],
        scratch_shapes=[pltpu.VMEM((1, D), F32)],
        compiler_params=_cparams("arbitrary", "arbitrary"),
        name="rwkv_prep",
    )(x, shift0.reshape(B, 1, D), g.reshape(1, D), mu)
    return mix, last.reshape(B, D)


def _resnorm_kernel(x_ref, h_ref, gp_ref, *rest, n_out):
    x = x_ref[...] + _rms(h_ref[...], NORM_EPS) * gp_ref[...]
    if n_out:
        gn_ref, xo_ref, no_ref = rest
        y = _rms(x, NORM_EPS)
        for j in range(n_out):
            no_ref[j] = (y * gn_ref[j:j + 1, :]).astype(no_ref.dtype)
    else:
        xo_ref, = rest
    xo_ref[...] = x


def _resnorm(x, h, g_post, g_next):
    M, D = x.shape
    n_out = len(g_next)
    tr = min(256, M)
    row = pl.BlockSpec((tr, D), lambda i: (i, 0))
    vec = pl.BlockSpec((1, D), lambda i: (0, 0))
    in_specs = [row, row, vec]
    args = [x, h, g_post.reshape(1, D)]
    out_specs = [row]
    out_shape = [jax.ShapeDtypeStruct((M, D), F32)]
    if n_out:
        in_specs.append(pl.BlockSpec((n_out, D), lambda i: (0, 0)))
        args.append(jnp.stack(g_next))
        out_specs.append(pl.BlockSpec((n_out, tr, D), lambda i: (0, i, 0)))
        out_shape.append(jax.ShapeDtypeStruct((n_out, M, D), BF16))
    res = pl.pallas_call(
        functools.partial(_resnorm_kernel, n_out=n_out),
        grid=(M // tr,),
        in_specs=in_specs,
        out_specs=out_specs,
        out_shape=out_shape,
        compiler_params=_cparams("parallel"),
        name="resnorm",
    )(*args)
    return (res[0], res[1]) if n_out else (res[0], None)


def _norm_kernel(x_ref, g_ref, o_ref):
    o_ref[...] = (_rms(x_ref[...], NORM_EPS) * g_ref[...]).astype(o_ref.dtype)


def _norm(x, g):
    M, D = x.shape
    tr = min(256, M)
    return pl.pallas_call(
        _norm_kernel,
        grid=(M // tr,),
        in_specs=[pl.BlockSpec((tr, D), lambda i: (i, 0)), pl.BlockSpec((1, D), lambda i: (0, 0))],
        out_specs=pl.BlockSpec((tr, D), lambda i: (i, 0)),
        out_shape=jax.ShapeDtypeStruct((M, D), BF16),
        compiler_params=_cparams("parallel"),
        name="norm",
    )(x, g.reshape(1, D))


def _split2(x):
    hi = x.astype(BF16)
    lo = (x - hi.astype(F32)).astype(BF16)
    return hi, lo


def _seg_sum(x, ones_bd):
    hi, lo = _split2(x)
    n = x.shape[0]
    s = _dot(jnp.concatenate([hi, lo], axis=0), ones_bd)
    return s[:n] + s[n:]


def _wkv_kernel(r_ref, k_ref, v_ref, wl_ref, al_ref, g_ref, *rest, rows, vres):
    if vres:
        vf_ref, vl_ref, p_ref, s0_ref, y_ref, so_ref, s_ref = rest
    else:
        p_ref, s0_ref, y_ref, so_ref, s_ref = rest
    c = pl.program_id(2)
    nc = pl.num_programs(2)
    L = CHUNK
    W = GROUP
    HG = HEADS_PER_GROUP

    @pl.when(c == 0)
    def _():
        s_ref[...] = s0_ref[...]

    def load(ref):
        x = ref[...].astype(F32)
        if rows < L:
            x = jnp.concatenate([x, jnp.zeros((L - rows, W), F32)], axis=0)
        return x

    p = p_ref[...]
    w0, a0, k_k, k_a = p[0:1], p[1:2], p[2:3], p[3:4]
    lnx_w, lnx_b, rk, v0 = p[4:5], p[5:6], p[6:7], p[7:8]

    lane_head = lax.broadcasted_iota(jnp.int32, (1, W), 1) // RWKV_HEAD
    rr = lax.broadcasted_iota(jnp.int32, (W, W), 0) // RWKV_HEAD
    cc = lax.broadcasted_iota(jnp.int32, (W, W), 1) // RWKV_HEAD
    bd = rr == cc
    ones_bd = jnp.where(bd, 1.0, 0.0).astype(BF16)
    trow = lax.broadcasted_iota(jnp.int32, (L, HG * L), 0)
    tcol = lax.broadcasted_iota(jnp.int32, (L, HG * L), 1) & (L - 1)
    strict = tcol < trow
    incl = tcol <= trow
    rb = lax.broadcasted_iota(jnp.int32, (HG * L, HG * L), 0) // L
    cb = lax.broadcasted_iota(jnp.int32, (HG * L, HG * L), 1) // L
    bd_l = rb == cb

    def ystack(x):
        return jnp.concatenate(
            [jnp.where(lane_head == h, x, 0.0) for h in range(HG)], axis=0).astype(BF16)

    r = load(r_ref)
    k = load(k_ref)
    v = load(v_ref)
    a_sig = jax.nn.sigmoid(load(al_ref) + a0)
    if vres:
        v = v + (load(vf_ref) - v) * jax.nn.sigmoid(load(vl_ref) + v0)
    z = -(load(wl_ref) + w0)
    softplus = jnp.maximum(z, 0.0) + jnp.log(1.0 + jnp.exp(-jnp.abs(z)))
    lw = -jnp.exp(-softplus - 0.5)
    if rows < L:
        trow1 = lax.broadcasted_iota(jnp.int32, (L, W), 0)
        lw = jnp.where(trow1 < rows, lw, 0.0)
    kkr = k * k_k
    nrm = jnp.sqrt(_seg_sum(kkr * kkr, ones_bd))
    kk = kkr / jnp.maximum(nrm, 1e-12)
    k2 = k * (1.0 + (a_sig - 1.0) * k_a)
    a = -kk
    b = kk * a_sig

    l1 = lw.astype(BF16)
    l2r = lw - l1.astype(F32)
    l2 = l2r.astype(BF16)
    l3 = (l2r - l2.astype(F32)).astype(BF16)
    tri = (lax.broadcasted_iota(jnp.int32, (L, L), 1)
           <= lax.broadcasted_iota(jnp.int32, (L, L), 0))
    tri = jnp.where(tri, 1.0, 0.0).astype(BF16)
    cs = _dot(tri, jnp.concatenate([l1, l2, l3], axis=1))
    cum = cs[:, :W] + cs[:, W:2 * W] + cs[:, 2 * W:]
    cl = cum[L - 1:L, :]
    e_c = jnp.exp(cum)
    e_cm = jnp.exp(cum - lw)
    e_nc = jnp.exp(-cum)
    e_lc = jnp.exp(cl - cum)
    g_l = jnp.exp(cl)
    at = a * e_cm
    rt = r * e_c
    bt = b * e_nc
    kt = k2 * e_nc
    bh = b * e_lc
    kh = k2 * e_lc

    x4 = _dot_nt(jnp.concatenate([at, rt], axis=0).astype(BF16),
                 jnp.concatenate([ystack(bt), ystack(kt)], axis=0))
    n_ab = jnp.where(strict, x4[:L, :HG * L], 0.0)
    m_ak = jnp.where(strict, x4[:L, HG * L:], 0.0)
    m_rb = jnp.where(incl, x4[L:, :HG * L], 0.0)
    m_rk = jnp.where(incl, x4[L:, HG * L:], 0.0)

    def bdiag(x):
        return jnp.where(bd_l, jnp.concatenate([x] * HG, axis=0), 0.0).astype(BF16)

    t_inv = jnp.where(tcol == trow, 1.0, 0.0) + n_ab
    pw = _dot(n_ab.astype(BF16), bdiag(n_ab))
    steps = int(math.log2(L)) - 1
    for it in range(steps):
        rhs = bdiag(pw)
        if it < steps - 1:
            res = _dot(jnp.concatenate([t_inv, pw], axis=0).astype(BF16), rhs)
            t_inv = t_inv + res[:L]
            pw = res[L:]
        else:
            t_inv = t_inv + _dot(t_inv.astype(BF16), rhs)

    yv = ystack(v)
    makv = _dot(m_ak.astype(BF16), yv)
    au = _dot(t_inv.astype(BF16), jnp.concatenate([ystack(at), ystack(makv)], axis=1))
    abar = au[:, :W]
    u0 = au[:, W:]
    y_abar = ystack(abar)
    d1 = _dot(m_rb.astype(BF16), jnp.concatenate([y_abar, ystack(u0)], axis=1))
    rbar = rt + d1[:, :W]
    o0 = d1[:, W:] + _dot(m_rk.astype(BF16), yv)

    s = s_ref[...]
    s_bf = s.astype(BF16)
    o = _dot_nt(rbar.astype(BF16), s_bf) + o0
    sa = _dot_nt(s_bf, y_abar)
    uv_t = jnp.concatenate([u0, v], axis=0).T
    q_t = _dot(uv_t.astype(BF16), jnp.concatenate([bh, kh], axis=0).astype(BF16))
    s_new = s * g_l + _dot(sa.astype(BF16), ystack(bh)) + jnp.where(bd, q_t, 0.0)
    s_ref[...] = s_new

    @pl.when(c == nc - 1)
    def _():
        so_ref[...] = s_new

    inv_n = 1.0 / RWKV_HEAD
    mean = _seg_sum(o, ones_bd) * inv_n
    dlt = o - mean
    var = _seg_sum(dlt * dlt, ones_bd) * inv_n
    yn = dlt * lax.rsqrt(var + GN_EPS) * lnx_w + lnx_b
    bonus = _seg_sum(r * k2 * rk, ones_bd) * v
    out = (yn + bonus) * load(g_ref)
    y_ref[...] = out[:rows].astype(y_ref.dtype)


def _wkv(rkv, wl, al, gate, params, s0_bd, *, B, T, vres=None, out_dtype=BF16):
    _, M, D = rkv.shape
    G = D // GROUP
    rows = min(CHUNK, T)
    nc = T // rows

    def plane(pi):
        return pl.BlockSpec((None, rows, GROUP), lambda b, g, c: (pi, b * nc + c, g))

    tile = pl.BlockSpec((rows, GROUP), lambda b, g, c: (b * nc + c, g))
    state = pl.BlockSpec((None, GROUP, GROUP), lambda b, g, c: (b * G + g, 0, 0))
    in_specs = [plane(0), plane(1), plane(2), tile, tile, tile]
    args = [rkv, rkv, rkv, wl, al, gate]
    if vres is not None:
        rkv_first, vl = vres
        in_specs += [plane(2), tile]
        args += [rkv_first, vl]
    in_specs += [pl.BlockSpec((8, GROUP), lambda b, g, c: (0, g)), state]
    args += [params, s0_bd]
    y, s_out = pl.pallas_call(
        functools.partial(_wkv_kernel, rows=rows, vres=vres is not None),
        grid=(B, G, nc),
        in_specs=in_specs,
        out_specs=[tile, state],
        out_shape=[jax.ShapeDtypeStruct((M, D), out_dtype),
                   jax.ShapeDtypeStruct(s0_bd.shape, F32)],
        scratch_shapes=[pltpu.VMEM((GROUP, GROUP), F32)],
        compiler_params=_cparams("parallel", "parallel", "arbitrary"),
        name="wkv7",
    )(*args)
    return y, s_out


def _state_to_bd(s):
    B, H, N, _ = s.shape
    G = H // HEADS_PER_GROUP
    s = s.reshape(B, G, HEADS_PER_GROUP, N, N)
    eye = jnp.eye(HEADS_PER_GROUP, dtype=s.dtype)
    return jnp.einsum("bghvk,hj->bghvjk", s, eye).reshape(B * G, GROUP, GROUP)


def _bd_to_state(s_bd, B):
    G = s_bd.shape[0] // B
    N = RWKV_HEAD
    s = s_bd.reshape(B, G, HEADS_PER_GROUP, N, HEADS_PER_GROUP, N)
    s = jnp.stack([s[:, :, h, :, h, :] for h in range(HEADS_PER_GROUP)], axis=2)
    return s.reshape(B, G * HEADS_PER_GROUP, N, N)


def _lambda(lp, lam_init):
    return (jnp.exp(jnp.sum(lp[0:1] * lp[1:2], keepdims=True))
            - jnp.exp(jnp.sum(lp[2:3] * lp[3:4], keepdims=True)) + lam_init)


def _softmax_step(s, vt, m, l, acc):
    m_new = jnp.maximum(m, jnp.max(s, axis=-1, keepdims=True))
    alpha = jnp.exp(m - m_new)
    p = jnp.exp(s - m_new)
    l_new = alpha * l + jnp.sum(p, axis=-1, keepdims=True)
    acc_new = alpha * acc + _dot(p.astype(BF16), vt)
    return m_new, l_new, acc_new


def _attn_prompt_kernel(q_ref, k_ref, v_ref, lp_ref, sg_ref, o_ref, *, tq, lam_init):
    qi = pl.program_id(2)
    scale = ATT_HEAD ** -0.5
    q = q_ref[...]
    q0 = q[:, :ATT_HEAD]
    q1 = q[:, ATT_HEAD:]
    tk = tq
    qpos = qi * tq + lax.broadcasted_iota(jnp.int32, (tq, tk), 0)
    kiota = lax.broadcasted_iota(jnp.int32, (tq, tk), 1)

    def body(j, carry):
        m0, l0, a0, m1, l1, a1 = carry
        start = pl.multiple_of(j * tk, tk)
        kt = k_ref[pl.ds(start, tk), :].astype(BF16)
        vt = v_ref[pl.ds(start, tk), :].astype(BF16)
        mask = (j * tk + kiota) <= qpos
        s0 = jnp.where(mask, _dot_nt(q0, kt[:, :ATT_HEAD]) * scale, NEG_BIG)
        s1 = jnp.where(mask, _dot_nt(q1, kt[:, ATT_HEAD:]) * scale, NEG_BIG)
        m0, l0, a0 = _softmax_step(s0, vt, m0, l0, a0)
        m1, l1, a1 = _softmax_step(s1, vt, m1, l1, a1)
        return m0, l0, a0, m1, l1, a1

    neg = jnp.full((tq, 1), NEG_BIG, F32)
    zero = jnp.zeros((tq, 1), F32)
    zacc = jnp.zeros((tq, 2 * ATT_HEAD), F32)
    m0, l0, a0, m1, l1, a1 = lax.fori_loop(0, qi + 1, body, (neg, zero, zacc, neg, zero, zacc))
    lam = _lambda(lp_ref[...], lam_init)
    o = a0 / l0 - lam * (a1 / l1)
    o = _rms(o, SUBLN_EPS) * sg_ref[...] * (1.0 - lam_init)
    o_ref[...] = o.astype(o_ref.dtype)


def _attn_prompt(q, k, v, lp, sg, *, B, T, lam_init, tq=256):
    M, Wd = q.shape
    HW = 2 * ATT_HEAD
    H = Wd // HW
    nq = T // tq
    return pl.pallas_call(
        functools.partial(_attn_prompt_kernel, tq=tq, lam_init=lam_init),
        grid=(B, H, nq),
        in_specs=[pl.BlockSpec((tq, HW), lambda b, h, i: (b * nq + i, h)),
                  pl.BlockSpec((T, HW), lambda b, h, i: (b, h)),
                  pl.BlockSpec((T, HW), lambda b, h, i: (b, h)),
                  pl.BlockSpec((4, ATT_HEAD), lambda b, h, i: (0, 0)),
                  pl.BlockSpec((1, HW), lambda b, h, i: (0, 0))],
        out_specs=pl.BlockSpec((tq, HW), lambda b, h, i: (b * nq + i, h)),
        out_shape=jax.ShapeDtypeStruct((M, Wd), BF16),
        compiler_params=_cparams("parallel", "parallel", "arbitrary"),
        name="diff_attn_prompt",
    )(q, k, v, lp, sg.reshape(1, HW))


def _attn_sample_kernel(pt_ref, q_ref, kc_ref, vc_ref, kn_ref, vn_ref, lp_ref, sg_ref, o_ref,
                        m_ref, l_ref, acc_ref, *, heads, tq, page, lam_init):
    p = pl.program_id(1)
    npg = pl.num_programs(1)
    scale = ATT_HEAD ** -0.5
    HW = 2 * ATT_HEAD

    @pl.when(p == 0)
    def _():
        m_ref[...] = jnp.full(m_ref.shape, NEG_BIG, F32)
        l_ref[...] = jnp.zeros(l_ref.shape, F32)
        acc_ref[...] = jnp.zeros(acc_ref.shape, F32)

    q = q_ref[...].astype(BF16)

    def update(kb, vb, mask):
        for h in range(heads):
            vt = vb[:, h * HW:(h + 1) * HW]
            for mp in range(2):
                idx = 2 * h + mp
                lo = h * HW + mp * ATT_HEAD
                s = _dot_nt(q[:, lo:lo + ATT_HEAD], kb[:, lo:lo + ATT_HEAD]) * scale
                if mask is not None:
                    s = jnp.where(mask, s, NEG_BIG)
                m_new, l_new, a_new = _softmax_step(s, vt, m_ref[idx], l_ref[idx], acc_ref[idx])
                m_ref[idx] = m_new
                l_ref[idx] = l_new
                acc_ref[idx] = a_new

    update(kc_ref[...].astype(BF16), vc_ref[...].astype(BF16), None)

    @pl.when(p == npg - 1)
    def _():
        pad = jnp.zeros((page - tq, kn_ref.shape[1]), F32)
        kb = jnp.concatenate([kn_ref[...], pad], axis=0).astype(BF16)
        vb = jnp.concatenate([vn_ref[...], pad], axis=0).astype(BF16)
        row = lax.broadcasted_iota(jnp.int32, (tq, page), 0)
        col = lax.broadcasted_iota(jnp.int32, (tq, page), 1)
        update(kb, vb, col <= row)
        lam = _lambda(lp_ref[...], lam_init)
        sg = sg_ref[...]
        for h in range(heads):
            o = acc_ref[2 * h] / l_ref[2 * h] - lam * (acc_ref[2 * h + 1] / l_ref[2 * h + 1])
            o = _rms(o, SUBLN_EPS) * sg * (1.0 - lam_init)
            o_ref[:, h * HW:(h + 1) * HW] = o


def _attn_sample(q, cache_k, cache_v, k_new, v_new, page_table, lp, sg, *, lam_init):
    Bd, npg = page_table.shape
    M, Wd = q.shape
    tq = M // Bd
    page = cache_k.shape[1]
    HW = 2 * ATT_HEAD
    heads = Wd // HW
    rows = pl.BlockSpec((tq, Wd), lambda b, p, pt: (b, 0))
    pages = pl.BlockSpec((None, page, Wd), lambda b, p, pt: (pt[b * npg + p], 0, 0))
    grid_spec = pltpu.PrefetchScalarGridSpec(
        num_scalar_prefetch=1,
        grid=(Bd, npg),
        in_specs=[rows, pages, pages, rows, rows,
                  pl.BlockSpec((4, ATT_HEAD), lambda b, p, pt: (0, 0)),
                  pl.BlockSpec((1, HW), lambda b, p, pt: (0, 0))],
        out_specs=rows,
        scratch_shapes=[pltpu.VMEM((2 * heads, tq, 1), F32),
                        pltpu.VMEM((2 * heads, tq, 1), F32),
                        pltpu.VMEM((2 * heads, tq, HW), F32)],
    )
    return pl.pallas_call(
        functools.partial(_attn_sample_kernel, heads=heads, tq=tq, page=page, lam_init=lam_init),
        grid_spec=grid_spec,
        out_shape=jax.ShapeDtypeStruct((M, Wd), F32),
        compiler_params=_cparams("parallel", "arbitrary"),
        name="diff_attn_sample",
    )(page_table.reshape(-1), q, cache_k, cache_v, k_new, v_new, lp, sg.reshape(1, HW))


def _rope_tables(pos, reps):
    half = ATT_HEAD // 2
    inv = jnp.power(ROPE_THETA, -jnp.arange(half, dtype=F32) / half)
    ang = pos.astype(F32)[:, None] * inv[None, :]
    cos = jnp.concatenate([jnp.cos(ang), jnp.cos(ang)], axis=-1)
    sin = jnp.concatenate([-jnp.sin(ang), jnp.sin(ang)], axis=-1)
    return jnp.tile(cos, (reps, 1)), jnp.tile(sin, (reps, 1))


def _lambda_init(layer_idx):
    return 0.8 - 0.6 * math.exp(-0.3 * layer_idx)


def kernel(x_prompt, x_sample, cache_k, cache_v, state_shift, state_wkv, page_table, norm_mix, norm_ffn, rwkv_mu, rwkv_vec, rwkv_wr, rwkv_wk, rwkv_wv, rwkv_wo, rwkv_w1, rwkv_w2, rwkv_a1, rwkv_a2, rwkv_v0, rwkv_v1, rwkv_v2, rwkv_g1, rwkv_g2, rwkv_rk, kv_norm, kv_wk, kv_wv, attn_wq, attn_wo, attn_lambda, attn_subln, ffn_w1, ffn_w2):
    depth = norm_mix.shape[0]
    n_a = rwkv_mu.shape[0]
    D = x_prompt.shape[-1]
    Wd = kv_wk.shape[1]
    n_att_heads = Wd // (2 * ATT_HEAD)
    n_rwkv_heads = D // RWKV_HEAD

    w_rkv = [jnp.stack([rwkv_wr[l], rwkv_wk[l], rwkv_wv[l]]).astype(BF16) for l in range(n_a)]
    w_o = [rwkv_wo[l].astype(BF16) for l in range(n_a)]
    lora_w = [_pad_lora(rwkv_w1[l], rwkv_w2[l]) for l in range(n_a)]
    lora_a = [_pad_lora(rwkv_a1[l], rwkv_a2[l]) for l in range(n_a)]
    lora_g = [_pad_lora(rwkv_g1[l], rwkv_g2[l]) for l in range(n_a)]
    lora_v = [_pad_lora(rwkv_v1[l], rwkv_v2[l]) for l in range(n_a - 1)]
    wkv_params = []
    for l in range(n_a):
        v0 = rwkv_v0[l - 1] if l > 0 else jnp.zeros((D,), F32)
        wkv_params.append(jnp.concatenate(
            [rwkv_vec[l], rwkv_rk[l].reshape(1, D), v0.reshape(1, D)], axis=0))
    w_kv = jnp.stack([kv_wk, kv_wv]).astype(BF16)
    w_q = [attn_wq[j].astype(BF16) for j in range(depth - n_a)]
    w_ao = [attn_wo[j].astype(BF16) for j in range(depth - n_a)]
    w_f1 = [ffn_w1[l].astype(BF16) for l in range(depth)]
    w_f2 = [ffn_w2[l].astype(BF16) for l in range(depth)]

    def trunk(x, B, T, shift0, wkv0, rope, attend, small):
        act_dtype = F32 if small else BF16
        new_shift, new_wkv = [], []
        rkv_first = None
        k_sh = v_sh = None
        xn_next = None
        for l in range(depth):
            if l < n_a:
                mix, last = _prep(x, shift0[l], norm_mix[l, 0], rwkv_mu[l], B=B, T=T, out_dtype=act_dtype)
                new_shift.append(last)
                rkv = _matmul((mix, PL_R), w_rkv[l])
                wl = _lora((mix, PL_W), *lora_w[l], act="tanh")
                al = _lora((mix, PL_A), *lora_a[l])
                gate = _lora((mix, PL_G), *lora_g[l], act="sigmoid")
                vres = None
                if l == 0:
                    rkv_first = rkv
                else:
                    vres = (rkv_first, _lora((mix, PL_V), *lora_v[l - 1]))
                y, s_bd = _wkv(rkv, wl, al, gate, wkv_params[l], _state_to_bd(wkv0[l].astype(F32)),
                               B=B, T=T, vres=vres, out_dtype=act_dtype)
                new_wkv.append(_bd_to_state(s_bd, B).astype(wkv0.dtype))
                h = _matmul(y, w_o[l])
            else:
                j = l - n_a
                lam_init = _lambda_init(l)
                q = _matmul(xn_next, w_q[j], out_dtype=act_dtype, rope=rope)
                o = attend(q, k_sh, v_sh, attn_lambda[j], attn_subln[j], lam_init)
                h = _matmul(o, w_ao[j])
            x, nrm = _resnorm(x, h, norm_mix[l, 1], [norm_ffn[l, 0]])
            h = _mlp((nrm, 0), w_f1[l], w_f2[l])
            g_next = []
            if l == n_a - 1:
                g_next.append(kv_norm)
            if n_a - 1 <= l < depth - 1:
                g_next.append(norm_mix[l + 1, 0])
            x, nrm = _resnorm(x, h, norm_ffn[l, 1], g_next)
            if l == n_a - 1:
                k_sh = _matmul((nrm, 0), w_kv[0], rope=rope)
                v_sh = _matmul((nrm, 0), w_kv[1])
            if g_next:
                xn_next = (nrm, len(g_next) - 1)
        return x, k_sh, v_sh, jnp.stack(new_shift), jnp.stack(new_wkv)

    Bp, Tp, _ = x_prompt.shape
    shift0_p = jnp.zeros((n_a, Bp, D), F32)
    wkv0_p = jnp.zeros((n_a, Bp, n_rwkv_heads, RWKV_HEAD, RWKV_HEAD), state_wkv.dtype)
    rope_p = _rope_tables(jnp.arange(Tp, dtype=jnp.int32), 1)

    def attend_p(q, k, v, lp, sg, lam_init):
        return _attn_prompt(q, k, v, lp, sg, B=Bp, T=Tp, lam_init=lam_init)

    y_p, k_p, v_p, shift_p, wkv_p = trunk(
        x_prompt.reshape(Bp * Tp, D), Bp, Tp, shift0_p, wkv0_p, rope_p, attend_p, False)

    Bd, Td, _ = x_sample.shape
    n_pages = page_table.shape[1]
    page = cache_k.shape[1]
    past_len = n_pages * page
    rope_s = _rope_tables(past_len + jnp.arange(Td, dtype=jnp.int32), Bd)
    ck = cache_k.reshape(cache_k.shape[0], page, Wd)
    cv = cache_v.reshape(cache_v.shape[0], page, Wd)

    def attend_s(q, k, v, lp, sg, lam_init):
        return _attn_sample(q, ck, cv, k, v, page_table, lp, sg, lam_init=lam_init)

    y_s, k_s, v_s, shift_s, wkv_s = trunk(
        x_sample.reshape(Bd * Td, D), Bd, Td, state_shift, state_wkv, rope_s, attend_s, True)

    H = n_att_heads
    return (y_p.reshape(Bp, Tp, D), y_s.reshape(Bd, Td, D),
            k_p.reshape(Bp, Tp, H, 2, ATT_HEAD), v_p.reshape(Bp, Tp, H, 2 * ATT_HEAD),
            shift_p, wkv_p,
            k_s.reshape(Bd, Td, H, 2, ATT_HEAD), v_s.reshape(Bd, Td, H, 2 * ATT_HEAD),
            shift_s, wkv_s)
```

```python
import functools
import math

import jax
import jax.numpy as jnp
from jax import lax
from jax.experimental import pallas as pl
from jax.experimental.pallas import tpu as pltpu

F32 = jnp.float32
BF16 = jnp.bfloat16

RWKV_HEAD = 64
ATT_HEAD = 128
NORM_EPS = 1e-6
GN_EPS = 64e-5
SUBLN_EPS = 1e-5
ROPE_THETA = 10000.0

LANES = 128
GROUP = 256
HEADS_PER_GROUP = GROUP // RWKV_HEAD
CHUNK = 64
WKV_GROUPS_PER_STEP = 4
VMEM_LIMIT = 48 * 1024 * 1024
NEG_BIG = -1e30


def _cparams(*sem):
    return pltpu.CompilerParams(dimension_semantics=sem, vmem_limit_bytes=VMEM_LIMIT)


def _dot(a, b):
    return jnp.dot(a, b, preferred_element_type=F32)


def _dot_nt(a, b):
    return lax.dot_general(a, b, (((1,), (1,)), ((), ())), preferred_element_type=F32)


def _rms(x, eps):
    return x * lax.rsqrt(jnp.mean(x * x, axis=-1, keepdims=True) + eps)


def _mm_kernel(x_ref, w_ref, *rest, act, rope, out_scale):
    acc = _dot(x_ref[...].astype(BF16), w_ref[...])
    if act == "tanh":
        acc = jnp.tanh(acc)
    elif act == "sigmoid":
        acc = jax.nn.sigmoid(acc)
    o_refs = rest[2:] if rope else rest
    if rope:
        cos = rest[0][...] * out_scale
        sin = rest[1][...] * out_scale
        for c in range(acc.shape[1] // LANES):
            blk = acc[:, c * LANES:(c + 1) * LANES]
            res = blk * cos + pltpu.roll(blk, LANES // 2, 1) * sin
            for o_ref in o_refs:
                o_ref[:, c * LANES:(c + 1) * LANES] = res.astype(o_ref.dtype)
    else:
        if out_scale != 1.0:
            acc = acc * out_scale
        for o_ref in o_refs:
            o_ref[...] = acc.astype(o_ref.dtype)


def _planes(x):
    return x if isinstance(x, tuple) else (x[None], 0)


MAX_WEIGHT_TILE_BYTES = 8 * 1024 * 1024


def _matmul(x, w, *, out_dtypes=(F32,), act=None, rope=None, out_scale=1.0, tm=512):
    x, p0 = _planes(x)
    squeeze = w.ndim == 2
    if squeeze:
        w = w[None]
    _, M, K = x.shape
    G, _, N = w.shape
    tm = min(tm, M)
    if rope is not None:
        tm = min(tm, rope[0].shape[0])
        assert rope[0].shape[0] % tm == 0
    tn = N
    while K * tn * w.dtype.itemsize > MAX_WEIGHT_TILE_BYTES and tn % (2 * LANES) == 0:
        tn //= 2
    assert M % tm == 0 and N % tn == 0
    in_specs = [pl.BlockSpec((None, tm, K), lambda g, i, j: (p0 + g, i, 0)),
                pl.BlockSpec((None, K, tn), lambda g, i, j: (g, 0, j))]
    args = [x, w]
    if rope is not None:
        cos, sin = rope
        nblk = cos.shape[0] // tm
        spec = pl.BlockSpec((tm, LANES), lambda g, i, j: (i % nblk, 0))
        in_specs += [spec, spec]
        args += [cos, sin]
    outs = pl.pallas_call(
        functools.partial(_mm_kernel, act=act, rope=rope is not None, out_scale=out_scale),
        grid=(G, M // tm, N // tn),
        in_specs=in_specs,
        out_specs=[pl.BlockSpec((None, tm, tn), lambda g, i, j: (g, i, j)) for _ in out_dtypes],
        out_shape=[jax.ShapeDtypeStruct((G, M, N), dt) for dt in out_dtypes],
        compiler_params=_cparams("parallel", "parallel", "arbitrary"),
        name="matmul",
    )(*args)
    outs = [o[0] if squeeze else o for o in outs]
    return outs[0] if len(outs) == 1 else outs


def _lora_kernel(x_ref, w1_ref, w2_ref, o_ref, *, act):
    h = _dot(x_ref[...].astype(BF16), w1_ref[...])
    if act == "tanh":
        h = jnp.tanh(h)
    elif act == "sigmoid":
        h = jax.nn.sigmoid(h)
    o_ref[...] = _dot(h.astype(BF16), w2_ref[...])


def _lora(x, w1, w2, *, act=None, tm=512):
    x, p0 = _planes(x)
    _, M, K = x.shape
    R = w1.shape[1]
    N = w2.shape[1]
    tm = min(tm, M)
    return pl.pallas_call(
        functools.partial(_lora_kernel, act=act),
        grid=(M // tm,),
        in_specs=[pl.BlockSpec((None, tm, K), lambda i: (p0, i, 0)),
                  pl.BlockSpec((K, R), lambda i: (0, 0)),
                  pl.BlockSpec((R, N), lambda i: (0, 0))],
        out_specs=pl.BlockSpec((tm, N), lambda i: (i, 0)),
        out_shape=jax.ShapeDtypeStruct((M, N), F32),
        compiler_params=_cparams("parallel"),
        name="lora",
    )(x, w1, w2)


def _pad_lora(w1, w2):
    r = w1.shape[1]
    rp = -(-r // LANES) * LANES
    return (jnp.pad(w1, ((0, 0), (0, rp - r))).astype(BF16),
            jnp.pad(w2, ((0, rp - r), (0, 0))).astype(BF16))


def _mlp_kernel(x_ref, w1_ref, w2_ref, o_ref):
    f = pl.program_id(1)
    h = jnp.maximum(_dot(x_ref[...], w1_ref[...]), 0.0)
    part = _dot((h * h).astype(BF16), w2_ref[...])

    @pl.when(f == 0)
    def _():
        o_ref[...] = part

    @pl.when(f > 0)
    def _():
        o_ref[...] += part


def _mlp(x, w1, w2, *, tm=512, tf=1024):
    x, p0 = _planes(x)
    _, M, D = x.shape
    F = w1.shape[1]
    tm = min(tm, M)
    return pl.pallas_call(
        _mlp_kernel,
        grid=(M // tm, F // tf),
        in_specs=[pl.BlockSpec((None, tm, D), lambda i, f: (p0, i, 0)),
                  pl.BlockSpec((D, tf), lambda i, f: (0, f)),
                  pl.BlockSpec((tf, D), lambda i, f: (f, 0))],
        out_specs=pl.BlockSpec((tm, D), lambda i, f: (i, 0)),
        out_shape=jax.ShapeDtypeStruct((M, D), F32),
        compiler_params=_cparams("parallel", "arbitrary"),
        name="mlp",
    )(x, w1, w2)


MIX_ORDER = (0, 2, 3, 1, 4, 5)
PL_R, PL_K, PL_V, PL_W, PL_A, PL_G = range(6)


def _prep_kernel(x_ref, sh_ref, g_ref, mu_ref, mix_ref, last_ref, carry_ref):
    i = pl.program_id(1)
    xn = _rms(x_ref[...], NORM_EPS) * g_ref[...]
    tr = xn.shape[0]

    @pl.when(i == 0)
    def _():
        carry_ref[...] = sh_ref[...]

    row = lax.broadcasted_iota(jnp.int32, xn.shape, 0)
    xprev = jnp.where(row == 0, carry_ref[...], pltpu.roll(xn, 1, 0))
    xx = xprev - xn
    for plane, m in enumerate(MIX_ORDER):
        mix_ref[plane] = (xn + xx * mu_ref[m:m + 1, :]).astype(mix_ref.dtype)
    last = xn[tr - 1:tr, :]
    carry_ref[...] = last
    last_ref[...] = last


def _prep(x, shift0, g, mu, *, B, T, out_dtype):
    M, D = x.shape
    tr = min(256, T)
    nt = T // tr
    mix, last = pl.pallas_call(
        _prep_kernel,
        grid=(B, nt),
        in_specs=[pl.BlockSpec((tr, D), lambda b, i: (b * nt + i, 0)),
                  pl.BlockSpec((None, 1, D), lambda b, i: (b, 0, 0)),
                  pl.BlockSpec((1, D), lambda b, i: (0, 0)),
                  pl.BlockSpec((6, D), lambda b, i: (0, 0))],
        out_specs=[pl.BlockSpec((6, tr, D), lambda b, i: (0, b * nt + i, 0)),
                   pl.BlockSpec((None, 1, D), lambda b, i: (b, 0, 0))],
        out_shape=[jax.ShapeDtypeStruct((6, M, D), out_dtype),
                   jax.ShapeDtypeStruct((B, 1, D), F32)],
        scratch_shapes=[pltpu.VMEM((1, D), F32)],
        compiler_params=_cparams("arbitrary", "arbitrary"),
        name="rwkv_prep",
    )(x, shift0.reshape(B, 1, D), g.reshape(1, D), mu)
    return mix, last.reshape(B, D)


def _resnorm_kernel(x_ref, h_ref, gp_ref, *rest, n_out):
    x = x_ref[...] + _rms(h_ref[...], NORM_EPS) * gp_ref[...]
    if n_out:
        gn_ref, xo_ref, no_ref = rest
        y = _rms(x, NORM_EPS)
        for j in range(n_out):
            no_ref[j] = (y * gn_ref[j:j + 1, :]).astype(no_ref.dtype)
    else:
        xo_ref, = rest
    xo_ref[...] = x


def _resnorm(x, h, g_post, g_next):
    M, D = x.shape
    n_out = len(g_next)
    tr = min(256, M)
    row = pl.BlockSpec((tr, D), lambda i: (i, 0))
    vec = pl.BlockSpec((1, D), lambda i: (0, 0))
    in_specs = [row, row, vec]
    args = [x, h, g_post.reshape(1, D)]
    out_specs = [row]
    out_shape = [jax.ShapeDtypeStruct((M, D), F32)]
    if n_out:
        in_specs.append(pl.BlockSpec((n_out, D), lambda i: (0, 0)))
        args.append(jnp.stack(g_next))
        out_specs.append(pl.BlockSpec((n_out, tr, D), lambda i: (0, i, 0)))
        out_shape.append(jax.ShapeDtypeStruct((n_out, M, D), BF16))
    res = pl.pallas_call(
        functools.partial(_resnorm_kernel, n_out=n_out),
        grid=(M // tr,),
        in_specs=in_specs,
        out_specs=out_specs,
        out_shape=out_shape,
        compiler_params=_cparams("parallel"),
        name="resnorm",
    )(*args)
    return (res[0], res[1]) if n_out else (res[0], None)


def _norm_kernel(x_ref, g_ref, o_ref):
    o_ref[...] = (_rms(x_ref[...], NORM_EPS) * g_ref[...]).astype(o_ref.dtype)


def _norm(x, g):
    M, D = x.shape
    tr = min(256, M)
    return pl.pallas_call(
        _norm_kernel,
        grid=(M // tr,),
        in_specs=[pl.BlockSpec((tr, D), lambda i: (i, 0)), pl.BlockSpec((1, D), lambda i: (0, 0))],
        out_specs=pl.BlockSpec((tr, D), lambda i: (i, 0)),
        out_shape=jax.ShapeDtypeStruct((M, D), BF16),
        compiler_params=_cparams("parallel"),
        name="norm",
    )(x, g.reshape(1, D))


def _split2(x):
    hi = x.astype(BF16)
    lo = (x - hi.astype(F32)).astype(BF16)
    return hi, lo


def _seg_sum(x, ones_bd):
    hi, lo = _split2(x)
    n = x.shape[0]
    s = _dot(jnp.concatenate([hi, lo], axis=0), ones_bd)
    return s[:n] + s[n:]


def _wkv_kernel(*refs, rows, vres, ng):
    n_tiles = 8 if vres else 6
    p_ref, s0_ref, y_ref, s_ref = refs[n_tiles:]

    @pl.when(pl.program_id(2) == 0)
    def _():
        s_ref[...] = s0_ref[...]

    gens = []
    for gi in range(ng):
        lanes = pl.ds(gi * GROUP, GROUP)
        vals = [ref[:, lanes] for ref in refs[:n_tiles]] + [p_ref[:, lanes], s_ref[gi]]
        gens.append(_wkv_group(*vals, rows=rows, vres=vres))
    outs = [None] * ng
    live = list(range(ng))
    while live:
        for gi in list(live):
            try:
                next(gens[gi])
            except StopIteration as done:
                outs[gi] = done.value
                live.remove(gi)
    for gi, (y, s_new) in enumerate(outs):
        y_ref[:, pl.ds(gi * GROUP, GROUP)] = y.astype(y_ref.dtype)
        s_ref[gi] = s_new


def _wkv_group(r_ref, k_ref, v_ref, wl_ref, al_ref, g_ref, *rest, rows, vres):
    if vres:
        vf_ref, vl_ref, p, s = rest
    else:
        p, s = rest
    L = CHUNK
    W = GROUP
    HG = HEADS_PER_GROUP

    def load(val):
        x = val.astype(F32)
        if rows < L:
            x = jnp.concatenate([x, jnp.zeros((L - rows, W), F32)], axis=0)
        return x

    w0, a0, k_k, k_a = p[0:1], p[1:2], p[2:3], p[3:4]
    lnx_w, lnx_b, rk, v0 = p[4:5], p[5:6], p[6:7], p[7:8]

    lane_head = lax.broadcasted_iota(jnp.int32, (1, W), 1) // RWKV_HEAD
    rr = lax.broadcasted_iota(jnp.int32, (W, W), 0) // RWKV_HEAD
    cc = lax.broadcasted_iota(jnp.int32, (W, W), 1) // RWKV_HEAD
    bd = rr == cc
    ones_bd = jnp.where(bd, 1.0, 0.0).astype(BF16)
    trow = lax.broadcasted_iota(jnp.int32, (L, HG * L), 0)
    tcol = lax.broadcasted_iota(jnp.int32, (L, HG * L), 1) & (L - 1)
    strict = tcol < trow
    incl = tcol <= trow
    rb = lax.broadcasted_iota(jnp.int32, (HG * L, HG * L), 0) // L
    cb = lax.broadcasted_iota(jnp.int32, (HG * L, HG * L), 1) // L
    bd_l = rb == cb

    def ystack(x):
        return jnp.concatenate(
            [jnp.where(lane_head == h, x, 0.0) for h in range(HG)], axis=0).astype(BF16)

    r = load(r_ref)
    k = load(k_ref)
    v = load(v_ref)
    a_sig = jax.nn.sigmoid(load(al_ref) + a0)
    if vres:
        v = v + (load(vf_ref) - v) * jax.nn.sigmoid(load(vl_ref) + v0)
    z = -(load(wl_ref) + w0)
    softplus = jnp.maximum(z, 0.0) + jnp.log(1.0 + jnp.exp(-jnp.abs(z)))
    lw = -jnp.exp(-softplus - 0.5)
    if rows < L:
        trow1 = lax.broadcasted_iota(jnp.int32, (L, W), 0)
        lw = jnp.where(trow1 < rows, lw, 0.0)
    kkr = k * k_k
    nrm2 = _seg_sum(kkr * kkr, ones_bd)
    yield
    nrm = jnp.sqrt(nrm2)
    kk = kkr / jnp.maximum(nrm, 1e-12)
    k2 = k * (1.0 + (a_sig - 1.0) * k_a)
    a = -kk
    b = kk * a_sig

    l1 = lw.astype(BF16)
    l2r = lw - l1.astype(F32)
    l2 = l2r.astype(BF16)
    l3 = (l2r - l2.astype(F32)).astype(BF16)
    tri = (lax.broadcasted_iota(jnp.int32, (L, L), 1)
           <= lax.broadcasted_iota(jnp.int32, (L, L), 0))
    tri = jnp.where(tri, 1.0, 0.0).astype(BF16)
    cs = _dot(tri, jnp.concatenate([l1, l2, l3], axis=1))
    yield
    cum = cs[:, :W] + cs[:, W:2 * W] + cs[:, 2 * W:]
    cl = cum[L - 1:L, :]
    e_c = jnp.exp(cum)
    e_cm = jnp.exp(cum - lw)
    e_nc = jnp.exp(-cum)
    e_lc = jnp.exp(cl - cum)
    g_l = jnp.exp(cl)
    at = a * e_cm
    rt = r * e_c
    bt = b * e_nc
    kt = k2 * e_nc
    bh = b * e_lc
    kh = k2 * e_lc

    x4 = _dot_nt(jnp.concatenate([at, rt], axis=0).astype(BF16),
                 jnp.concatenate([ystack(bt), ystack(kt)], axis=0))
    yield
    n_ab = jnp.where(strict, x4[:L, :HG * L], 0.0)
    m_ak = jnp.where(strict, x4[:L, HG * L:], 0.0)
    m_rb = jnp.where(incl, x4[L:, :HG * L], 0.0)
    m_rk = jnp.where(incl, x4[L:, HG * L:], 0.0)

    def bdiag(x):
        return jnp.where(bd_l, jnp.concatenate([x] * HG, axis=0), 0.0).astype(BF16)

    t_inv = jnp.where(tcol == trow, 1.0, 0.0) + n_ab
    yv = ystack(v)
    makv = _dot(m_ak.astype(BF16), yv)
    pw = _dot(n_ab.astype(BF16), bdiag(n_ab))
    yield
    steps = int(math.log2(L)) - 1
    for it in range(steps):
        rhs = bdiag(pw)
        if it < steps - 1:
            res = _dot(jnp.concatenate([t_inv, pw], axis=0).astype(BF16), rhs)
            yield
            t_inv = t_inv + res[:L]
            pw = res[L:]
        else:
            res = _dot(t_inv.astype(BF16), rhs)
            yield
            t_inv = t_inv + res

    au = _dot(t_inv.astype(BF16), jnp.concatenate([ystack(at), ystack(makv)], axis=1))
    yield
    abar = au[:, :W]
    u0 = au[:, W:]
    y_abar = ystack(abar)
    d1 = _dot(m_rb.astype(BF16), jnp.concatenate([y_abar, ystack(u0)], axis=1))
    d2 = _dot(m_rk.astype(BF16), yv)
    s_bf = s.astype(BF16)
    sa = _dot_nt(s_bf, y_abar)
    yield
    rbar = rt + d1[:, :W]
    o0 = d1[:, W:] + d2
    o = _dot_nt(rbar.astype(BF16), s_bf) + o0
    uv_t = jnp.concatenate([u0, v], axis=0).T
    q_t = _dot(uv_t.astype(BF16), jnp.concatenate([bh, kh], axis=0).astype(BF16))
    s_new = s * g_l + _dot(sa.astype(BF16), ystack(bh)) + jnp.where(bd, q_t, 0.0)
    bonus = _seg_sum(r * k2 * rk, ones_bd) * v
    yield
    inv_n = 1.0 / RWKV_HEAD
    mean = _seg_sum(o, ones_bd) * inv_n
    yield
    dlt = o - mean
    var = _seg_sum(dlt * dlt, ones_bd) * inv_n
    yield
    yn = dlt * lax.rsqrt(var + GN_EPS) * lnx_w + lnx_b
    out = (yn + bonus) * load(g_ref)
    return out[:rows], s_new


def _wkv(rkv, wl, al, gate, params, s0_bd, *, B, T, vres=None, out_dtype=BF16):
    _, M, D = rkv.shape
    ng = max(d for d in range(1, WKV_GROUPS_PER_STEP + 1) if (D // GROUP) % d == 0)
    G = D // (ng * GROUP)
    rows = min(CHUNK, T)
    nc = T // rows

    def plane(pi):
        return pl.BlockSpec((None, rows, ng * GROUP), lambda b, g, c: (pi, b * nc + c, g))

    tile = pl.BlockSpec((rows, ng * GROUP), lambda b, g, c: (b * nc + c, g))
    state = pl.BlockSpec((ng, GROUP, GROUP), lambda b, g, c: (b * G + g, 0, 0))
    in_specs = [plane(0), plane(1), plane(2), tile, tile, tile]
    args = [rkv, rkv, rkv, wl, al, gate]
    if vres is not None:
        rkv_first, vl = vres
        in_specs += [plane(2), tile]
        args += [rkv_first, vl]
    in_specs += [pl.BlockSpec((8, ng * GROUP), lambda b, g, c: (0, g)), state]
    args += [params, s0_bd]
    y, s_out = pl.pallas_call(
        functools.partial(_wkv_kernel, rows=rows, vres=vres is not None, ng=ng),
        grid=(B, G, nc),
        in_specs=in_specs,
        out_specs=[tile, state],
        out_shape=[jax.ShapeDtypeStruct((M, D), out_dtype),
                   jax.ShapeDtypeStruct(s0_bd.shape, F32)],
        compiler_params=_cparams("parallel", "parallel", "arbitrary"),
        name="wkv7",
    )(*args)
    return y, s_out


def _state_to_bd(s):
    B, H, N, _ = s.shape
    G = H // HEADS_PER_GROUP
    s = s.reshape(B, G, HEADS_PER_GROUP, N, N)
    eye = jnp.eye(HEADS_PER_GROUP, dtype=s.dtype)
    return jnp.einsum("bghvk,hj->bghvjk", s, eye).reshape(B * G, GROUP, GROUP)


def _bd_to_state(s_bd, B):
    G = s_bd.shape[0] // B
    N = RWKV_HEAD
    s = s_bd.reshape(B, G, HEADS_PER_GROUP, N, HEADS_PER_GROUP, N)
    s = jnp.stack([s[:, :, h, :, h, :] for h in range(HEADS_PER_GROUP)], axis=2)
    return s.reshape(B, G * HEADS_PER_GROUP, N, N)


def _lambda(lp, lam_init):
    return (jnp.exp(jnp.sum(lp[0:1] * lp[1:2], keepdims=True))
            - jnp.exp(jnp.sum(lp[2:3] * lp[3:4], keepdims=True)) + lam_init)


Q_SCALE = ATT_HEAD ** -0.5 * math.log2(math.e)


def _rep(x, n):
    return x if n == LANES else jnp.concatenate([x] * (n // LANES), axis=1)


def _softmax_step(s, vt, m, l, acc, ones=None):
    m_new = jnp.maximum(m, jnp.max(s, axis=-1, keepdims=True))
    alpha = jnp.exp2(m - m_new)
    p = jnp.exp2(s - _rep(m_new, s.shape[1]))
    pb = p.astype(BF16)
    if ones is None:
        l_new = alpha * l + jnp.sum(p, axis=-1, keepdims=True)
    else:
        l_new = alpha * l + _dot(pb, ones)
    acc_new = _rep(alpha, acc.shape[1]) * acc + _dot(pb, vt)
    return m_new, l_new, acc_new


def _attn_prompt_kernel(q_ref, k_ref, v_ref, lp_ref, sg_ref, o_ref, m_ref, l_ref, acc_ref,
                        *, tq, lam_init):
    qi = pl.program_id(2)
    q = q_ref[...]
    qs = (q[:, :ATT_HEAD], q[:, ATT_HEAD:])
    m_ref[...] = jnp.full(m_ref.shape, NEG_BIG, F32)
    l_ref[...] = jnp.zeros(l_ref.shape, F32)
    acc_ref[...] = jnp.zeros(acc_ref.shape, F32)
    ones = jnp.ones((tq, LANES), BF16)

    def scores(j):
        kt = k_ref[pl.ds(pl.multiple_of(j * tq, tq), tq), :]
        return tuple(_dot_nt(qs[mp], kt[:, mp * ATT_HEAD:(mp + 1) * ATT_HEAD]) for mp in range(2))

    def update(j, s):
        vt = v_ref[pl.ds(pl.multiple_of(j * tq, tq), tq), :]
        for mp in range(2):
            m_ref[mp], l_ref[mp], acc_ref[mp] = _softmax_step(
                s[mp], vt, m_ref[mp], l_ref[mp], acc_ref[mp], ones)

    def body(j, s):
        s_next = scores(j + 1)
        update(j, s)
        return s_next

    s = lax.fori_loop(0, qi, body, scores(0))
    keep = (lax.broadcasted_iota(jnp.int32, (tq, tq), 1)
            <= lax.broadcasted_iota(jnp.int32, (tq, tq), 0))
    update(qi, tuple(jnp.where(keep, x, NEG_BIG) for x in s))
    lam = _lambda(lp_ref[...], lam_init)
    hw = acc_ref.shape[2]
    o = acc_ref[0] / _rep(l_ref[0], hw) - lam * (acc_ref[1] / _rep(l_ref[1], hw))
    o = _rms(o, SUBLN_EPS) * sg_ref[...] * (1.0 - lam_init)
    o_ref[...] = o.astype(o_ref.dtype)


def _attn_prompt(q, k, v, lp, sg, *, B, T, lam_init, tq=256):
    M, Wd = q.shape
    HW = 2 * ATT_HEAD
    H = Wd // HW
    tq = min(tq, T)
    nq = T // tq
    return pl.pallas_call(
        functools.partial(_attn_prompt_kernel, tq=tq, lam_init=lam_init),
        grid=(B, H, nq),
        in_specs=[pl.BlockSpec((tq, HW), lambda b, h, i: (b * nq + i, h)),
                  pl.BlockSpec((T, HW), lambda b, h, i: (b, h)),
                  pl.BlockSpec((T, HW), lambda b, h, i: (b, h)),
                  pl.BlockSpec((4, ATT_HEAD), lambda b, h, i: (0, 0)),
                  pl.BlockSpec((1, HW), lambda b, h, i: (0, 0))],
        out_specs=pl.BlockSpec((tq, HW), lambda b, h, i: (b * nq + i, h)),
        out_shape=jax.ShapeDtypeStruct((M, Wd), BF16),
        scratch_shapes=[pltpu.VMEM((2, tq, LANES), F32), pltpu.VMEM((2, tq, LANES), F32),
                        pltpu.VMEM((2, tq, HW), F32)],
        compiler_params=_cparams("parallel", "parallel", "arbitrary"),
        name="diff_attn_prompt",
    )(q, k, v, lp, sg.reshape(1, HW))


PAGES_PER_STEP = 4


def _attn_sample_kernel(pt_ref, q_ref, *rest, heads, tq, npp, new_rows, lam_init):
    k_refs = rest[:npp]
    v_refs = rest[npp:2 * npp]
    kn_ref, vn_ref, lp_ref, sg_ref, o_ref, m_ref, l_ref, acc_ref = rest[2 * npp:]
    p = pl.program_id(1)
    npg = pl.num_programs(1)
    HW = 2 * ATT_HEAD
    R = heads * tq

    @pl.when(p == 0)
    def _():
        m_ref[...] = jnp.full(m_ref.shape, NEG_BIG, F32)
        l_ref[...] = jnp.zeros(l_ref.shape, F32)
        acc_ref[...] = jnp.zeros(acc_ref.shape, F32)

    q = q_ref[...]
    qm = [jnp.concatenate([q[:, h * HW + mp * ATT_HEAD:h * HW + (mp + 1) * ATT_HEAD]
                           for h in range(heads)], axis=0).astype(BF16) for mp in range(2)]

    def scores(k0, k1, valid):
        g = jnp.concatenate([_dot_nt(qm[0], k0.astype(BF16)), _dot_nt(qm[1], k1.astype(BF16))],
                            axis=0)
        return jnp.where(valid, g, NEG_BIG)

    def head_mask(ncol):
        row = lax.broadcasted_iota(jnp.int32, (2 * R, ncol), 0)
        col = lax.broadcasted_iota(jnp.int32, (2 * R, ncol), 1)
        return (col % heads) == ((row // tq) % heads), row, col

    state = (m_ref[...], l_ref[...], acc_ref[...])
    nk = k_refs[0].shape[0] // 2
    same_head, _, _ = head_mask(nk)
    for j in range(npp):
        s = scores(k_refs[j][pl.ds(0, nk, stride=2), :], k_refs[j][pl.ds(1, nk, stride=2), :],
                   same_head)
        state = _softmax_step(s, v_refs[j][...].astype(BF16), *state)
    m_ref[...], l_ref[...], acc_ref[...] = state

    @pl.when(p == npg - 1)
    def _():
        pad_to = max(new_rows, LANES)
        k0 = kn_ref[pl.ds(0, new_rows, stride=2), :]
        k1 = kn_ref[pl.ds(1, new_rows, stride=2), :]
        vn = vn_ref[...]
        if pad_to > new_rows:
            zk = jnp.zeros((pad_to - new_rows, ATT_HEAD), F32)
            k0 = jnp.concatenate([k0, zk], axis=0)
            k1 = jnp.concatenate([k1, zk], axis=0)
            vn = jnp.concatenate([vn, jnp.zeros((pad_to - new_rows, HW), F32)], axis=0)
        same, row, col = head_mask(pad_to)
        valid = same & ((col // heads) <= (row % tq)) & (col < new_rows)
        m, l, acc = _softmax_step(scores(k0, k1, valid), vn.astype(BF16), *state)
        lam = _lambda(lp_ref[...], lam_init)
        sg = sg_ref[...]
        for h in range(heads):
            r0 = slice(h * tq, (h + 1) * tq)
            r1 = slice(R + h * tq, R + (h + 1) * tq)
            o = acc[r0] / _rep(l[r0], HW) - lam * (acc[r1] / _rep(l[r1], HW))
            o = _rms(o, SUBLN_EPS) * sg * (1.0 - lam_init)
            o_ref[:, h * HW:(h + 1) * HW] = o


def _attn_sample(q, cache_k, cache_v, k_new, v_new, page_table, lp, sg, *, lam_init):
    Bd, npg = page_table.shape
    M, Wd = q.shape
    tq = M // Bd
    HW = 2 * ATT_HEAD
    heads = Wd // HW
    npp = max(d for d in range(1, PAGES_PER_STEP + 1) if npg % d == 0)
    nsteps = npg // npp
    krows, vrows = cache_k.shape[1], cache_v.shape[1]
    new_rows = tq * heads
    rows = pl.BlockSpec((tq, Wd), lambda b, p, pt: (b, 0))

    def page_spec(nrows, width, j):
        return pl.BlockSpec((None, nrows, width), lambda b, p, pt: (pt[b * npg + p * npp + j], 0, 0))

    grid_spec = pltpu.PrefetchScalarGridSpec(
        num_scalar_prefetch=1,
        grid=(Bd, nsteps),
        in_specs=([rows] + [page_spec(krows, ATT_HEAD, j) for j in range(npp)]
                  + [page_spec(vrows, HW, j) for j in range(npp)]
                  + [pl.BlockSpec((None, 2 * new_rows, ATT_HEAD), lambda b, p, pt: (b, 0, 0)),
                     pl.BlockSpec((None, new_rows, HW), lambda b, p, pt: (b, 0, 0)),
                     pl.BlockSpec((4, ATT_HEAD), lambda b, p, pt: (0, 0)),
                     pl.BlockSpec((1, HW), lambda b, p, pt: (0, 0))]),
        out_specs=rows,
        scratch_shapes=[pltpu.VMEM((2 * new_rows, LANES), F32),
                        pltpu.VMEM((2 * new_rows, LANES), F32),
                        pltpu.VMEM((2 * new_rows, HW), F32)],
    )
    return pl.pallas_call(
        functools.partial(_attn_sample_kernel, heads=heads, tq=tq, npp=npp, new_rows=new_rows,
                          lam_init=lam_init),
        grid_spec=grid_spec,
        out_shape=jax.ShapeDtypeStruct((M, Wd), F32),
        compiler_params=_cparams("parallel", "arbitrary"),
        name="diff_attn_sample",
    )(page_table.reshape(-1), q, *([cache_k] * npp), *([cache_v] * npp),
      k_new.reshape(Bd, 2 * new_rows, ATT_HEAD), v_new.reshape(Bd, new_rows, HW),
      lp, sg.reshape(1, HW))


def _rope_tables(pos, reps):
    half = ATT_HEAD // 2
    inv = jnp.power(ROPE_THETA, -jnp.arange(half, dtype=F32) / half)
    ang = pos.astype(F32)[:, None] * inv[None, :]
    cos = jnp.concatenate([jnp.cos(ang), jnp.cos(ang)], axis=-1)
    sin = jnp.concatenate([-jnp.sin(ang), jnp.sin(ang)], axis=-1)
    return jnp.tile(cos, (reps, 1)), jnp.tile(sin, (reps, 1))


def _lambda_init(layer_idx):
    return 0.8 - 0.6 * math.exp(-0.3 * layer_idx)


def kernel(x_prompt, x_sample, cache_k, cache_v, state_shift, state_wkv, page_table, norm_mix, norm_ffn, rwkv_mu, rwkv_vec, rwkv_wr, rwkv_wk, rwkv_wv, rwkv_wo, rwkv_w1, rwkv_w2, rwkv_a1, rwkv_a2, rwkv_v0, rwkv_v1, rwkv_v2, rwkv_g1, rwkv_g2, rwkv_rk, kv_norm, kv_wk, kv_wv, attn_wq, attn_wo, attn_lambda, attn_subln, ffn_w1, ffn_w2):
    depth = norm_mix.shape[0]
    n_a = rwkv_mu.shape[0]
    D = x_prompt.shape[-1]
    Wd = kv_wk.shape[1]
    n_att_heads = Wd // (2 * ATT_HEAD)
    n_rwkv_heads = D // RWKV_HEAD

    w_rkv = [jnp.stack([rwkv_wr[l], rwkv_wk[l], rwkv_wv[l]]).astype(BF16) for l in range(n_a)]
    w_o = [rwkv_wo[l].astype(BF16) for l in range(n_a)]
    lora_w = [_pad_lora(rwkv_w1[l], rwkv_w2[l]) for l in range(n_a)]
    lora_a = [_pad_lora(rwkv_a1[l], rwkv_a2[l]) for l in range(n_a)]
    lora_g = [_pad_lora(rwkv_g1[l], rwkv_g2[l]) for l in range(n_a)]
    lora_v = [_pad_lora(rwkv_v1[l], rwkv_v2[l]) for l in range(n_a - 1)]
    wkv_params = []
    for l in range(n_a):
        v0 = rwkv_v0[l - 1] if l > 0 else jnp.zeros((D,), F32)
        wkv_params.append(jnp.concatenate(
            [rwkv_vec[l], rwkv_rk[l].reshape(1, D), v0.reshape(1, D)], axis=0))
    w_kv = jnp.stack([kv_wk, kv_wv]).astype(BF16)
    w_q = [attn_wq[j].astype(BF16) for j in range(depth - n_a)]
    w_ao = [attn_wo[j].astype(BF16) for j in range(depth - n_a)]
    w_f1 = [ffn_w1[l].astype(BF16) for l in range(depth)]
    w_f2 = [ffn_w2[l].astype(BF16) for l in range(depth)]

    def trunk(x, B, T, shift0, wkv0, rope, attend, small):
        act_dtype = F32 if small else BF16
        new_shift, new_wkv = [], []
        rkv_first = None
        k_sh = v_sh = k_att = v_att = None
        xn_next = None
        for l in range(depth):
            if l < n_a:
                mix, last = _prep(x, shift0[l], norm_mix[l, 0], rwkv_mu[l], B=B, T=T, out_dtype=act_dtype)
                new_shift.append(last)
                rkv = _matmul((mix, PL_R), w_rkv[l])
                wl = _lora((mix, PL_W), *lora_w[l], act="tanh")
                al = _lora((mix, PL_A), *lora_a[l])
                gate = _lora((mix, PL_G), *lora_g[l], act="sigmoid")
                vres = None
                if l == 0:
                    rkv_first = rkv
                else:
                    vres = (rkv_first, _lora((mix, PL_V), *lora_v[l - 1]))
                y, s_bd = _wkv(rkv, wl, al, gate, wkv_params[l], _state_to_bd(wkv0[l].astype(F32)),
                               B=B, T=T, vres=vres, out_dtype=act_dtype)
                new_wkv.append(_bd_to_state(s_bd, B).astype(wkv0.dtype))
                h = _matmul(y, w_o[l])
            else:
                j = l - n_a
                lam_init = _lambda_init(l)
                q = _matmul(xn_next, w_q[j], out_dtypes=(act_dtype,), rope=rope, out_scale=Q_SCALE)
                o = attend(q, k_att, v_att, attn_lambda[j], attn_subln[j], lam_init)
                h = _matmul(o, w_ao[j])
            x, nrm = _resnorm(x, h, norm_mix[l, 1], [norm_ffn[l, 0]])
            h = _mlp((nrm, 0), w_f1[l], w_f2[l])
            g_next = []
            if l == n_a - 1:
                g_next.append(kv_norm)
            if n_a - 1 <= l < depth - 1:
                g_next.append(norm_mix[l + 1, 0])
            x, nrm = _resnorm(x, h, norm_ffn[l, 1], g_next)
            if l == n_a - 1:
                if small:
                    k_sh = k_att = _matmul((nrm, 0), w_kv[0], rope=rope)
                    v_sh = v_att = _matmul((nrm, 0), w_kv[1])
                else:
                    k_sh, k_att = _matmul((nrm, 0), w_kv[0], rope=rope, out_dtypes=(F32, BF16))
                    v_sh, v_att = _matmul((nrm, 0), w_kv[1], out_dtypes=(F32, BF16))
            if g_next:
                xn_next = (nrm, len(g_next) - 1)
        return x, k_sh, v_sh, jnp.stack(new_shift), jnp.stack(new_wkv)

    Bp, Tp, _ = x_prompt.shape
    shift0_p = jnp.zeros((n_a, Bp, D), F32)
    wkv0_p = jnp.zeros((n_a, Bp, n_rwkv_heads, RWKV_HEAD, RWKV_HEAD), state_wkv.dtype)
    rope_p = _rope_tables(jnp.arange(Tp, dtype=jnp.int32), 1)

    def attend_p(q, k, v, lp, sg, lam_init):
        return _attn_prompt(q, k, v, lp, sg, B=Bp, T=Tp, lam_init=lam_init)

    y_p, k_p, v_p, shift_p, wkv_p = trunk(
        x_prompt.reshape(Bp * Tp, D), Bp, Tp, shift0_p, wkv0_p, rope_p, attend_p, False)

    Bd, Td, _ = x_sample.shape
    n_pages = page_table.shape[1]
    page = cache_k.shape[1]
    past_len = n_pages * page
    rope_s = _rope_tables(past_len + jnp.arange(Td, dtype=jnp.int32), Bd)
    ck = cache_k.reshape(cache_k.shape[0], page * n_att_heads * 2, ATT_HEAD)
    cv = cache_v.reshape(cache_v.shape[0], page * n_att_heads, 2 * ATT_HEAD)

    def attend_s(q, k, v, lp, sg, lam_init):
        return _attn_sample(q, ck, cv, k, v, page_table, lp, sg, lam_init=lam_init)

    y_s, k_s, v_s, shift_s, wkv_s = trunk(
        x_sample.reshape(Bd * Td, D), Bd, Td, state_shift, state_wkv, rope_s, attend_s, True)

    H = n_att_heads
    return (y_p.reshape(Bp, Tp, D), y_s.reshape(Bd, Td, D),
            k_p.reshape(Bp, Tp, H, 2, ATT_HEAD), v_p.reshape(Bp, Tp, H, 2 * ATT_HEAD),
            shift_p, wkv_p,
            k_s.reshape(Bd, Td, H, 2, ATT_HEAD), v_s.reshape(Bd, Td, H, 2 * ATT_HEAD),
            shift_s, wkv_s)
```

```python
import functools
import math

import jax
import jax.numpy as jnp
from jax import lax
from jax.experimental import pallas as pl
from jax.experimental.pallas import tpu as pltpu

F32 = jnp.float32
BF16 = jnp.bfloat16

RWKV_HEAD = 64
ATT_HEAD = 128
NORM_EPS = 1e-6
GN_EPS = 64e-5
SUBLN_EPS = 1e-5
ROPE_THETA = 10000.0

LANES = 128
GROUP = 256
HEADS_PER_GROUP = GROUP // RWKV_HEAD
CHUNK = 64
WKV_GROUPS_PER_STEP = 8
VMEM_LIMIT = 48 * 1024 * 1024
MLP_VMEM_LIMIT = 56 * 1024 * 1024
NEG_BIG = -1e30


def _cparams(*sem):
    return pltpu.CompilerParams(dimension_semantics=sem, vmem_limit_bytes=VMEM_LIMIT)


def _dot(a, b):
    return jnp.dot(a, b, preferred_element_type=F32)


def _dot_nt(a, b):
    return lax.dot_general(a, b, (((1,), (1,)), ((), ())), preferred_element_type=F32)


def _rms(x, eps):
    return x * lax.rsqrt(jnp.mean(x * x, axis=-1, keepdims=True) + eps)


def _mm_kernel(x_ref, w_ref, *rest, act, rope, out_scale):
    acc = _dot(x_ref[...].astype(BF16), w_ref[...])
    if act == "tanh":
        acc = jnp.tanh(acc)
    elif act == "sigmoid":
        acc = jax.nn.sigmoid(acc)
    o_refs = rest[2:] if rope else rest
    if rope:
        cos = rest[0][...] * out_scale
        sin = rest[1][...] * out_scale
        for c in range(acc.shape[1] // LANES):
            blk = acc[:, c * LANES:(c + 1) * LANES]
            res = blk * cos + pltpu.roll(blk, LANES // 2, 1) * sin
            for o_ref in o_refs:
                o_ref[:, c * LANES:(c + 1) * LANES] = res.astype(o_ref.dtype)
    else:
        if out_scale != 1.0:
            acc = acc * out_scale
        for o_ref in o_refs:
            o_ref[...] = acc.astype(o_ref.dtype)


def _planes(x):
    return x if isinstance(x, tuple) else (x[None], 0)


MAX_WEIGHT_TILE_BYTES = 8 * 1024 * 1024


def _matmul(x, w, *, out_dtypes=(F32,), act=None, rope=None, out_scale=1.0, tm=512):
    x, p0 = _planes(x)
    squeeze = w.ndim == 2
    if squeeze:
        w = w[None]
    _, M, K = x.shape
    G, _, N = w.shape
    tm = min(tm, M)
    if rope is not None:
        tm = min(tm, rope[0].shape[0])
        assert rope[0].shape[0] % tm == 0
    tn = N
    while K * tn * w.dtype.itemsize > MAX_WEIGHT_TILE_BYTES and tn % (2 * LANES) == 0:
        tn //= 2
    assert M % tm == 0 and N % tn == 0
    in_specs = [pl.BlockSpec((None, tm, K), lambda g, i, j: (p0 + g, i, 0)),
                pl.BlockSpec((None, K, tn), lambda g, i, j: (g, 0, j))]
    args = [x, w]
    if rope is not None:
        cos, sin = rope
        nblk = cos.shape[0] // tm
        spec = pl.BlockSpec((tm, LANES), lambda g, i, j: (i % nblk, 0))
        in_specs += [spec, spec]
        args += [cos, sin]
    outs = pl.pallas_call(
        functools.partial(_mm_kernel, act=act, rope=rope is not None, out_scale=out_scale),
        grid=(G, M // tm, N // tn),
        in_specs=in_specs,
        out_specs=[pl.BlockSpec((None, tm, tn), lambda g, i, j: (g, i, j)) for _ in out_dtypes],
        out_shape=[jax.ShapeDtypeStruct((G, M, N), dt) for dt in out_dtypes],
        compiler_params=_cparams("parallel", "parallel", "arbitrary"),
        name="matmul",
    )(*args)
    outs = [o[0] if squeeze else o for o in outs]
    return outs[0] if len(outs) == 1 else outs


def _lora_kernel(x_ref, w1_ref, w2_ref, o_ref, *, act):
    h = _dot(x_ref[...].astype(BF16), w1_ref[...])
    if act == "tanh":
        h = jnp.tanh(h)
    elif act == "sigmoid":
        h = jax.nn.sigmoid(h)
    o_ref[...] = _dot(h.astype(BF16), w2_ref[...])


def _lora(x, w1, w2, *, act=None, tm=512):
    x, p0 = _planes(x)
    _, M, K = x.shape
    R = w1.shape[1]
    N = w2.shape[1]
    tm = min(tm, M)
    return pl.pallas_call(
        functools.partial(_lora_kernel, act=act),
        grid=(M // tm,),
        in_specs=[pl.BlockSpec((None, tm, K), lambda i: (p0, i, 0)),
                  pl.BlockSpec((K, R), lambda i: (0, 0)),
                  pl.BlockSpec((R, N), lambda i: (0, 0))],
        out_specs=pl.BlockSpec((tm, N), lambda i: (i, 0)),
        out_shape=jax.ShapeDtypeStruct((M, N), F32),
        compiler_params=_cparams("parallel"),
        name="lora",
    )(x, w1, w2)


def _pad_lora(w1, w2):
    r = w1.shape[1]
    rp = -(-r // LANES) * LANES
    return (jnp.pad(w1, ((0, 0), (0, rp - r))).astype(BF16),
            jnp.pad(w2, ((0, rp - r), (0, 0))).astype(BF16))


def _mlp_kernel(x_ref, w1_ref, w2_ref, xr_ref, gp_ref, *rest, n_out, nf):
    if n_out:
        gn_ref, xo_ref, no_ref = rest
    else:
        xo_ref, = rest
    f = pl.program_id(1)
    h = jnp.maximum(_dot(x_ref[...], w1_ref[...]), 0.0)
    part = _dot((h * h).astype(BF16), w2_ref[...])

    def finish(total):
        x = xr_ref[...] + _rms(total, NORM_EPS) * gp_ref[...]
        xo_ref[...] = x
        if n_out:
            y = _rms(x, NORM_EPS)
            for j in range(n_out):
                no_ref[j] = (y * gn_ref[j:j + 1, :]).astype(no_ref.dtype)

    if nf == 1:
        finish(part)
        return

    @pl.when(f == 0)
    def _():
        xo_ref[...] = part

    @pl.when((f > 0) & (f < nf - 1))
    def _():
        xo_ref[...] += part

    @pl.when(f == nf - 1)
    def _():
        finish(xo_ref[...] + part)


def _mlp(x, w1, w2, x_res, g_post, g_next, *, tm=512, tf=1024):
    x, p0 = _planes(x)
    w1, l1 = _planes(w1)
    w2, l2 = _planes(w2)
    _, M, D = x.shape
    F = w1.shape[2]
    tm = min(tm, M)
    tf = min(tf, F)
    n_out = len(g_next)
    row = pl.BlockSpec((tm, D), lambda i, f: (i, 0))
    vec = pl.BlockSpec((1, D), lambda i, f: (0, 0))
    in_specs = [pl.BlockSpec((None, tm, D), lambda i, f: (p0, i, 0)),
                pl.BlockSpec((None, D, tf), lambda i, f: (l1, 0, f)),
                pl.BlockSpec((None, tf, D), lambda i, f: (l2, f, 0)),
                pl.BlockSpec((tm, D), lambda i, f: (i, 0), pipeline_mode=pl.Buffered(1)),
                vec]
    args = [x, w1, w2, x_res, g_post.reshape(1, D)]
    out_specs = [row]
    out_shape = [jax.ShapeDtypeStruct((M, D), F32)]
    if n_out:
        in_specs.append(pl.BlockSpec((n_out, D), lambda i, f: (0, 0)))
        args.append(jnp.stack(g_next))
        out_specs.append(pl.BlockSpec((n_out, tm, D), lambda i, f: (0, i, 0)))
        out_shape.append(jax.ShapeDtypeStruct((n_out, M, D), BF16))
    res = pl.pallas_call(
        functools.partial(_mlp_kernel, n_out=n_out, nf=F // tf),
        grid=(M // tm, F // tf),
        in_specs=in_specs,
        out_specs=out_specs,
        out_shape=out_shape,
        compiler_params=pltpu.CompilerParams(dimension_semantics=("parallel", "arbitrary"),
                                             vmem_limit_bytes=MLP_VMEM_LIMIT),
        name="mlp",
    )(*args)
    return (res[0], res[1]) if n_out else (res[0], None)


MIX_ORDER = (0, 2, 3, 1, 4, 5)
PL_R, PL_K, PL_V, PL_W, PL_A, PL_G = range(6)


def _prep_kernel(x_ref, sh_ref, g_ref, mu_ref, mix_ref, last_ref, carry_ref):
    i = pl.program_id(1)
    xn = _rms(x_ref[...], NORM_EPS) * g_ref[...]
    tr = xn.shape[0]

    @pl.when(i == 0)
    def _():
        carry_ref[...] = sh_ref[...]

    row = lax.broadcasted_iota(jnp.int32, xn.shape, 0)
    xprev = jnp.where(row == 0, carry_ref[...], pltpu.roll(xn, 1, 0))
    xx = xprev - xn
    for plane, m in enumerate(MIX_ORDER):
        mix_ref[plane] = (xn + xx * mu_ref[m:m + 1, :]).astype(mix_ref.dtype)
    last = xn[tr - 1:tr, :]
    carry_ref[...] = last
    last_ref[...] = last


def _prep(x, shift0, g, mu, *, B, T, out_dtype):
    M, D = x.shape
    tr = min(256, T)
    nt = T // tr
    mix, last = pl.pallas_call(
        _prep_kernel,
        grid=(B, nt),
        in_specs=[pl.BlockSpec((tr, D), lambda b, i: (b * nt + i, 0)),
                  pl.BlockSpec((None, 1, D), lambda b, i: (b, 0, 0)),
                  pl.BlockSpec((1, D), lambda b, i: (0, 0)),
                  pl.BlockSpec((6, D), lambda b, i: (0, 0))],
        out_specs=[pl.BlockSpec((6, tr, D), lambda b, i: (0, b * nt + i, 0)),
                   pl.BlockSpec((None, 1, D), lambda b, i: (b, 0, 0))],
        out_shape=[jax.ShapeDtypeStruct((6, M, D), out_dtype),
                   jax.ShapeDtypeStruct((B, 1, D), F32)],
        scratch_shapes=[pltpu.VMEM((1, D), F32)],
        compiler_params=_cparams("arbitrary", "arbitrary"),
        name="rwkv_prep",
    )(x, shift0.reshape(B, 1, D), g.reshape(1, D), mu)
    return mix, last.reshape(B, D)


def _resnorm_kernel(x_ref, h_ref, *rest, n_out, project):
    h = h_ref[...]
    if project:
        w_ref, *rest = rest
        h = _dot(h.astype(BF16), w_ref[...])
    gp_ref, *rest = rest
    x = x_ref[...] + _rms(h, NORM_EPS) * gp_ref[...]
    if n_out:
        gn_ref, xo_ref, no_ref = rest
        y = _rms(x, NORM_EPS)
        for j in range(n_out):
            no_ref[j] = (y * gn_ref[j:j + 1, :]).astype(no_ref.dtype)
    else:
        xo_ref, = rest
    xo_ref[...] = x


def _resnorm(x, h, g_post, g_next, w=None):
    M, D = x.shape
    n_out = len(g_next)
    tr = min(512 if w is not None else 256, M)
    row = pl.BlockSpec((tr, D), lambda i: (i, 0))
    vec = pl.BlockSpec((1, D), lambda i: (0, 0))
    in_specs = [row, pl.BlockSpec((tr, h.shape[1]), lambda i: (i, 0))]
    args = [x, h]
    if w is not None:
        in_specs.append(pl.BlockSpec(w.shape, lambda i: (0, 0)))
        args.append(w)
    in_specs.append(vec)
    args.append(g_post.reshape(1, D))
    out_specs = [row]
    out_shape = [jax.ShapeDtypeStruct((M, D), F32)]
    if n_out:
        in_specs.append(pl.BlockSpec((n_out, D), lambda i: (0, 0)))
        args.append(jnp.stack(g_next))
        out_specs.append(pl.BlockSpec((n_out, tr, D), lambda i: (0, i, 0)))
        out_shape.append(jax.ShapeDtypeStruct((n_out, M, D), BF16))
    res = pl.pallas_call(
        functools.partial(_resnorm_kernel, n_out=n_out, project=w is not None),
        grid=(M // tr,),
        in_specs=in_specs,
        out_specs=out_specs,
        out_shape=out_shape,
        compiler_params=_cparams("parallel"),
        name="resnorm",
    )(*args)
    return (res[0], res[1]) if n_out else (res[0], None)


def _norm_kernel(x_ref, g_ref, o_ref):
    o_ref[...] = (_rms(x_ref[...], NORM_EPS) * g_ref[...]).astype(o_ref.dtype)


def _norm(x, g):
    M, D = x.shape
    tr = min(256, M)
    return pl.pallas_call(
        _norm_kernel,
        grid=(M // tr,),
        in_specs=[pl.BlockSpec((tr, D), lambda i: (i, 0)), pl.BlockSpec((1, D), lambda i: (0, 0))],
        out_specs=pl.BlockSpec((tr, D), lambda i: (i, 0)),
        out_shape=jax.ShapeDtypeStruct((M, D), BF16),
        compiler_params=_cparams("parallel"),
        name="norm",
    )(x, g.reshape(1, D))


def _split2(x):
    hi = x.astype(BF16)
    lo = (x - hi.astype(F32)).astype(BF16)
    return hi, lo


def _seg_sum(x, ones_bd):
    hi, lo = _split2(x)
    n = x.shape[0]
    s = _dot(jnp.concatenate([hi, lo], axis=0), ones_bd)
    return s[:n] + s[n:]


def _wkv_kernel(*refs, rows, vres, ng):
    n_tiles = 8 if vres else 6
    p_ref, s0_ref, y_ref, s_ref = refs[n_tiles:]

    @pl.when(pl.program_id(2) == 0)
    def _():
        s_ref[...] = s0_ref[...]

    gens = []
    for gi in range(ng):
        lanes = pl.ds(gi * GROUP, GROUP)
        vals = [ref[:, lanes] for ref in refs[:n_tiles]] + [p_ref[:, lanes], s_ref[gi]]
        gens.append(_wkv_group(*vals, rows=rows, vres=vres))
    outs = [None] * ng
    live = list(range(ng))
    while live:
        for gi in list(live):
            try:
                next(gens[gi])
            except StopIteration as done:
                outs[gi] = done.value
                live.remove(gi)
    for gi, (y, s_new) in enumerate(outs):
        y_ref[:, pl.ds(gi * GROUP, GROUP)] = y.astype(y_ref.dtype)
        s_ref[gi] = s_new


def _wkv_group(r_ref, k_ref, v_ref, wl_ref, al_ref, g_ref, *rest, rows, vres):
    if vres:
        vf_ref, vl_ref, p, s = rest
    else:
        p, s = rest
    L = CHUNK
    W = GROUP
    HG = HEADS_PER_GROUP

    def load(val):
        x = val.astype(F32)
        if rows < L:
            x = jnp.concatenate([x, jnp.zeros((L - rows, W), F32)], axis=0)
        return x

    w0, a0, k_k, k_a = p[0:1], p[1:2], p[2:3], p[3:4]
    lnx_w, lnx_b, rk, v0 = p[4:5], p[5:6], p[6:7], p[7:8]

    lane_head = lax.broadcasted_iota(jnp.int32, (1, W), 1) // RWKV_HEAD
    rr = lax.broadcasted_iota(jnp.int32, (W, W), 0) // RWKV_HEAD
    cc = lax.broadcasted_iota(jnp.int32, (W, W), 1) // RWKV_HEAD
    bd = rr == cc
    ones_bd = jnp.where(bd, 1.0, 0.0).astype(BF16)
    trow = lax.broadcasted_iota(jnp.int32, (L, HG * L), 0)
    tcol = lax.broadcasted_iota(jnp.int32, (L, HG * L), 1) & (L - 1)
    strict = tcol < trow
    incl = tcol <= trow
    rb = lax.broadcasted_iota(jnp.int32, (HG * L, HG * L), 0) // L
    cb = lax.broadcasted_iota(jnp.int32, (HG * L, HG * L), 1) // L
    bd_l = rb == cb

    def ystack(x):
        return jnp.concatenate(
            [jnp.where(lane_head == h, x, 0.0) for h in range(HG)], axis=0).astype(BF16)

    r = load(r_ref)
    k = load(k_ref)
    v = load(v_ref)
    a_sig = jax.nn.sigmoid(load(al_ref) + a0)
    if vres:
        v = v + (load(vf_ref) - v) * jax.nn.sigmoid(load(vl_ref) + v0)
    z = -(load(wl_ref) + w0)
    softplus = jnp.maximum(z, 0.0) + jnp.log(1.0 + jnp.exp(-jnp.abs(z)))
    lw = -jnp.exp(-softplus - 0.5)
    if rows < L:
        trow1 = lax.broadcasted_iota(jnp.int32, (L, W), 0)
        lw = jnp.where(trow1 < rows, lw, 0.0)
    kkr = k * k_k
    nrm2 = _seg_sum(kkr * kkr, ones_bd)
    yield
    nrm = jnp.sqrt(nrm2)
    kk = kkr / jnp.maximum(nrm, 1e-12)
    k2 = k * (1.0 + (a_sig - 1.0) * k_a)
    a = -kk
    b = kk * a_sig

    l1 = lw.astype(BF16)
    l2r = lw - l1.astype(F32)
    l2 = l2r.astype(BF16)
    l3 = (l2r - l2.astype(F32)).astype(BF16)
    tri = (lax.broadcasted_iota(jnp.int32, (L, L), 1)
           <= lax.broadcasted_iota(jnp.int32, (L, L), 0))
    tri = jnp.where(tri, 1.0, 0.0).astype(BF16)
    cs = _dot(tri, jnp.concatenate([l1, l2, l3], axis=1))
    yield
    cum = cs[:, :W] + cs[:, W:2 * W] + cs[:, 2 * W:]
    cl = cum[L - 1:L, :]
    e_c = jnp.exp(cum)
    e_cm = jnp.exp(cum - lw)
    e_nc = jnp.exp(-cum)
    e_lc = jnp.exp(cl - cum)
    g_l = jnp.exp(cl)
    at = a * e_cm
    rt = r * e_c
    bt = b * e_nc
    kt = k2 * e_nc
    bh = b * e_lc
    kh = k2 * e_lc

    x4 = _dot_nt(jnp.concatenate([at, rt], axis=0).astype(BF16),
                 jnp.concatenate([ystack(bt), ystack(kt)], axis=0))
    yield
    n_ab = jnp.where(strict, x4[:L, :HG * L], 0.0)
    m_ak = jnp.where(strict, x4[:L, HG * L:], 0.0)
    m_rb = jnp.where(incl, x4[L:, :HG * L], 0.0)
    m_rk = jnp.where(incl, x4[L:, HG * L:], 0.0)

    def bdiag(x):
        return jnp.where(bd_l, jnp.concatenate([x] * HG, axis=0), 0.0).astype(BF16)

    t_inv = jnp.where(tcol == trow, 1.0, 0.0) + n_ab
    yv = ystack(v)
    makv = _dot(m_ak.astype(BF16), yv)
    pw = _dot(n_ab.astype(BF16), bdiag(n_ab))
    yield
    steps = int(math.log2(L)) - 1
    for it in range(steps):
        rhs = bdiag(pw)
        if it < steps - 1:
            res = _dot(jnp.concatenate([t_inv, pw], axis=0).astype(BF16), rhs)
            yield
            t_inv = t_inv + res[:L]
            pw = res[L:]
        else:
            res = _dot(t_inv.astype(BF16), rhs)
            yield
            t_inv = t_inv + res

    au = _dot(t_inv.astype(BF16), jnp.concatenate([ystack(at), ystack(makv)], axis=1))
    yield
    abar = au[:, :W]
    u0 = au[:, W:]
    y_abar = ystack(abar)
    d1 = _dot(m_rb.astype(BF16), jnp.concatenate([y_abar, ystack(u0)], axis=1))
    d2 = _dot(m_rk.astype(BF16), yv)
    s_bf = s.astype(BF16)
    sa = _dot_nt(s_bf, y_abar)
    yield
    rbar = rt + d1[:, :W]
    o0 = d1[:, W:] + d2
    o = _dot_nt(rbar.astype(BF16), s_bf) + o0
    uv_t = jnp.concatenate([u0, v], axis=0).T
    q_t = _dot(uv_t.astype(BF16), jnp.concatenate([bh, kh], axis=0).astype(BF16))
    s_new = s * g_l + _dot(sa.astype(BF16), ystack(bh)) + jnp.where(bd, q_t, 0.0)
    bonus = _seg_sum(r * k2 * rk, ones_bd) * v
    yield
    inv_n = 1.0 / RWKV_HEAD
    mean = _seg_sum(o, ones_bd) * inv_n
    yield
    dlt = o - mean
    var = _seg_sum(dlt * dlt, ones_bd) * inv_n
    yield
    yn = dlt * lax.rsqrt(var + GN_EPS) * lnx_w + lnx_b
    out = (yn + bonus) * load(g_ref)
    return out[:rows], s_new


def _wkv(rkv, wl, al, gate, params, s0_bd, *, B, T, vres=None, out_dtype=BF16):
    _, M, D = rkv.shape
    ng = max(d for d in range(1, WKV_GROUPS_PER_STEP + 1) if (D // GROUP) % d == 0)
    G = D // (ng * GROUP)
    rows = min(CHUNK, T)
    nc = T // rows

    def plane(pi):
        return pl.BlockSpec((None, rows, ng * GROUP), lambda b, g, c: (pi, b * nc + c, g))

    tile = pl.BlockSpec((rows, ng * GROUP), lambda b, g, c: (b * nc + c, g))
    state = pl.BlockSpec((ng, GROUP, GROUP), lambda b, g, c: (b * G + g, 0, 0))
    in_specs = [plane(0), plane(1), plane(2), tile, tile, tile]
    args = [rkv, rkv, rkv, wl, al, gate]
    if vres is not None:
        rkv_first, vl = vres
        in_specs += [plane(2), tile]
        args += [rkv_first, vl]
    in_specs += [pl.BlockSpec((8, ng * GROUP), lambda b, g, c: (0, g)), state]
    args += [params, s0_bd]
    y, s_out = pl.pallas_call(
        functools.partial(_wkv_kernel, rows=rows, vres=vres is not None, ng=ng),
        grid=(B, G, nc),
        in_specs=in_specs,
        out_specs=[tile, state],
        out_shape=[jax.ShapeDtypeStruct((M, D), out_dtype),
                   jax.ShapeDtypeStruct(s0_bd.shape, F32)],
        compiler_params=_cparams("parallel", "parallel", "arbitrary"),
        name="wkv7",
    )(*args)
    return y, s_out


def _state_to_bd(s):
    B, H, N, _ = s.shape
    G = H // HEADS_PER_GROUP
    s = s.reshape(B, G, HEADS_PER_GROUP, N, N)
    eye = jnp.eye(HEADS_PER_GROUP, dtype=s.dtype)
    return jnp.einsum("bghvk,hj->bghvjk", s, eye).reshape(B * G, GROUP, GROUP)


def _bd_to_state(s_bd, B):
    G = s_bd.shape[0] // B
    N = RWKV_HEAD
    s = s_bd.reshape(B, G, HEADS_PER_GROUP, N, HEADS_PER_GROUP, N)
    s = jnp.stack([s[:, :, h, :, h, :] for h in range(HEADS_PER_GROUP)], axis=2)
    return s.reshape(B, G * HEADS_PER_GROUP, N, N)


def _lambda(lp, lam_init):
    return (jnp.exp(jnp.sum(lp[0:1] * lp[1:2], keepdims=True))
            - jnp.exp(jnp.sum(lp[2:3] * lp[3:4], keepdims=True)) + lam_init)


Q_SCALE = ATT_HEAD ** -0.5 * math.log2(math.e)


def _rep(x, n):
    return x if n == LANES else jnp.concatenate([x] * (n // LANES), axis=1)


def _softmax_step(s, vt, m, l, acc, ones=None):
    m_new = jnp.maximum(m, jnp.max(s, axis=-1, keepdims=True))
    alpha = jnp.exp2(m - m_new)
    p = jnp.exp2(s - _rep(m_new, s.shape[1]))
    pb = p.astype(BF16)
    if ones is None:
        l_new = alpha * l + jnp.sum(p, axis=-1, keepdims=True)
    else:
        l_new = alpha * l + _dot(pb, ones)
    acc_new = _rep(alpha, acc.shape[1]) * acc + _dot(pb, vt)
    return m_new, l_new, acc_new


def _attn_prompt_kernel(q_ref, k_ref, v_ref, lp_ref, sg_ref, o_ref, m_ref, l_ref, acc_ref,
                        *, tq, lam_init):
    qi = pl.program_id(2)
    q = q_ref[...]
    qs = (q[:, :ATT_HEAD], q[:, ATT_HEAD:])
    m_ref[...] = jnp.full(m_ref.shape, NEG_BIG, F32)
    l_ref[...] = jnp.zeros(l_ref.shape, F32)
    acc_ref[...] = jnp.zeros(acc_ref.shape, F32)

    def scores(j):
        kt = k_ref[pl.ds(pl.multiple_of(j * tq, tq), tq), :]
        return tuple(_dot_nt(qs[mp], kt[:, mp * ATT_HEAD:(mp + 1) * ATT_HEAD]) for mp in range(2))

    def update(j, s):
        vt = v_ref[pl.ds(pl.multiple_of(j * tq, tq), tq), :]
        for mp in range(2):
            m_ref[mp], l_ref[mp], acc_ref[mp] = _softmax_step(
                s[mp], vt, m_ref[mp], l_ref[mp], acc_ref[mp])

    def body(j, s):
        s_next = scores(j + 1)
        update(j, s)
        return s_next

    s = lax.fori_loop(0, qi, body, scores(0))
    keep = (lax.broadcasted_iota(jnp.int32, (tq, tq), 1)
            <= lax.broadcasted_iota(jnp.int32, (tq, tq), 0))
    update(qi, tuple(jnp.where(keep, x, NEG_BIG) for x in s))
    lam = _lambda(lp_ref[...], lam_init)
    hw = acc_ref.shape[2]
    o = acc_ref[0] / _rep(l_ref[0], hw) - lam * (acc_ref[1] / _rep(l_ref[1], hw))
    o = _rms(o, SUBLN_EPS) * sg_ref[...] * (1.0 - lam_init)
    o_ref[...] = o.astype(o_ref.dtype)


def _attn_prompt(q, k, v, lp, sg, *, B, T, lam_init, tq=512):
    M, Wd = q.shape
    HW = 2 * ATT_HEAD
    H = Wd // HW
    tq = min(tq, T)
    nq = T // tq
    return pl.pallas_call(
        functools.partial(_attn_prompt_kernel, tq=tq, lam_init=lam_init),
        grid=(B, H, nq),
        in_specs=[pl.BlockSpec((tq, HW), lambda b, h, i: (b * nq + i, h)),
                  pl.BlockSpec((T, HW), lambda b, h, i: (b, h)),
                  pl.BlockSpec((T, HW), lambda b, h, i: (b, h)),
                  pl.BlockSpec((4, ATT_HEAD), lambda b, h, i: (0, 0)),
                  pl.BlockSpec((1, HW), lambda b, h, i: (0, 0))],
        out_specs=pl.BlockSpec((tq, HW), lambda b, h, i: (b * nq + i, h)),
        out_shape=jax.ShapeDtypeStruct((M, Wd), BF16),
        scratch_shapes=[pltpu.VMEM((2, tq, LANES), F32), pltpu.VMEM((2, tq, LANES), F32),
                        pltpu.VMEM((2, tq, HW), F32)],
        compiler_params=_cparams("parallel", "parallel", "arbitrary"),
        name="diff_attn_prompt",
    )(q, k, v, lp, sg.reshape(1, HW))


PAGES_PER_STEP = 4


def _attn_sample_kernel(pt_ref, q_ref, *rest, heads, tq, npp, new_rows, lam_init):
    k_refs = rest[:npp]
    v_refs = rest[npp:2 * npp]
    kn_ref, vn_ref, lp_ref, sg_ref, o_ref, m_ref, l_ref, acc_ref = rest[2 * npp:]
    p = pl.program_id(1)
    npg = pl.num_programs(1)
    HW = 2 * ATT_HEAD
    R = heads * tq

    @pl.when(p == 0)
    def _():
        m_ref[...] = jnp.full(m_ref.shape, NEG_BIG, F32)
        l_ref[...] = jnp.zeros(l_ref.shape, F32)
        acc_ref[...] = jnp.zeros(acc_ref.shape, F32)

    q = q_ref[...]
    qm = [jnp.concatenate([q[:, h * HW + mp * ATT_HEAD:h * HW + (mp + 1) * ATT_HEAD]
                           for h in range(heads)], axis=0).astype(BF16) for mp in range(2)]

    def scores(k0, k1, valid):
        g = jnp.concatenate([_dot_nt(qm[0], k0.astype(BF16)), _dot_nt(qm[1], k1.astype(BF16))],
                            axis=0)
        return jnp.where(valid, g, NEG_BIG)

    def head_mask(ncol):
        row = lax.broadcasted_iota(jnp.int32, (2 * R, ncol), 0)
        col = lax.broadcasted_iota(jnp.int32, (2 * R, ncol), 1)
        return (col % heads) == ((row // tq) % heads), row, col

    state = (m_ref[...], l_ref[...], acc_ref[...])
    nk = k_refs[0].shape[0] // 2
    same_head, _, _ = head_mask(nk)
    for j in range(npp):
        s = scores(k_refs[j][pl.ds(0, nk, stride=2), :], k_refs[j][pl.ds(1, nk, stride=2), :],
                   same_head)
        state = _softmax_step(s, v_refs[j][...].astype(BF16), *state)
    m_ref[...], l_ref[...], acc_ref[...] = state

    @pl.when(p == npg - 1)
    def _():
        pad_to = max(new_rows, LANES)
        k0 = kn_ref[pl.ds(0, new_rows, stride=2), :]
        k1 = kn_ref[pl.ds(1, new_rows, stride=2), :]
        vn = vn_ref[...]
        if pad_to > new_rows:
            zk = jnp.zeros((pad_to - new_rows, ATT_HEAD), F32)
            k0 = jnp.concatenate([k0, zk], axis=0)
            k1 = jnp.concatenate([k1, zk], axis=0)
            vn = jnp.concatenate([vn, jnp.zeros((pad_to - new_rows, HW), F32)], axis=0)
        same, row, col = head_mask(pad_to)
        valid = same & ((col // heads) <= (row % tq)) & (col < new_rows)
        m, l, acc = _softmax_step(scores(k0, k1, valid), vn.astype(BF16), *state)
        lam = _lambda(lp_ref[...], lam_init)
        sg = sg_ref[...]
        for h in range(heads):
            r0 = slice(h * tq, (h + 1) * tq)
            r1 = slice(R + h * tq, R + (h + 1) * tq)
            o = acc[r0] / _rep(l[r0], HW) - lam * (acc[r1] / _rep(l[r1], HW))
            o = _rms(o, SUBLN_EPS) * sg * (1.0 - lam_init)
            o_ref[:, h * HW:(h + 1) * HW] = o


def _attn_sample(q, cache_k, cache_v, k_new, v_new, page_table, lp, sg, *, lam_init):
    Bd, npg = page_table.shape
    M, Wd = q.shape
    tq = M // Bd
    HW = 2 * ATT_HEAD
    heads = Wd // HW
    npp = max(d for d in range(1, PAGES_PER_STEP + 1) if npg % d == 0)
    nsteps = npg // npp
    krows, vrows = cache_k.shape[1], cache_v.shape[1]
    new_rows = tq * heads
    rows = pl.BlockSpec((tq, Wd), lambda b, p, pt: (b, 0))

    def page_spec(nrows, width, j):
        return pl.BlockSpec((None, nrows, width), lambda b, p, pt: (pt[b * npg + p * npp + j], 0, 0))

    grid_spec = pltpu.PrefetchScalarGridSpec(
        num_scalar_prefetch=1,
        grid=(Bd, nsteps),
        in_specs=([rows] + [page_spec(krows, ATT_HEAD, j) for j in range(npp)]
                  + [page_spec(vrows, HW, j) for j in range(npp)]
                  + [pl.BlockSpec((None, 2 * new_rows, ATT_HEAD), lambda b, p, pt: (b, 0, 0)),
                     pl.BlockSpec((None, new_rows, HW), lambda b, p, pt: (b, 0, 0)),
                     pl.BlockSpec((4, ATT_HEAD), lambda b, p, pt: (0, 0)),
                     pl.BlockSpec((1, HW), lambda b, p, pt: (0, 0))]),
        out_specs=rows,
        scratch_shapes=[pltpu.VMEM((2 * new_rows, LANES), F32),
                        pltpu.VMEM((2 * new_rows, LANES), F32),
                        pltpu.VMEM((2 * new_rows, HW), F32)],
    )
    return pl.pallas_call(
        functools.partial(_attn_sample_kernel, heads=heads, tq=tq, npp=npp, new_rows=new_rows,
                          lam_init=lam_init),
        grid_spec=grid_spec,
        out_shape=jax.ShapeDtypeStruct((M, Wd), F32),
        compiler_params=_cparams("parallel", "arbitrary"),
        name="diff_attn_sample",
    )(page_table.reshape(-1), q, *([cache_k] * npp), *([cache_v] * npp),
      k_new.reshape(Bd, 2 * new_rows, ATT_HEAD), v_new.reshape(Bd, new_rows, HW),
      lp, sg.reshape(1, HW))


def _rope_tables(pos, reps):
    half = ATT_HEAD // 2
    inv = jnp.power(ROPE_THETA, -jnp.arange(half, dtype=F32) / half)
    ang = pos.astype(F32)[:, None] * inv[None, :]
    cos = jnp.concatenate([jnp.cos(ang), jnp.cos(ang)], axis=-1)
    sin = jnp.concatenate([-jnp.sin(ang), jnp.sin(ang)], axis=-1)
    return jnp.tile(cos, (reps, 1)), jnp.tile(sin, (reps, 1))


def _lambda_init(layer_idx):
    return 0.8 - 0.6 * math.exp(-0.3 * layer_idx)


def kernel(x_prompt, x_sample, cache_k, cache_v, state_shift, state_wkv, page_table, norm_mix, norm_ffn, rwkv_mu, rwkv_vec, rwkv_wr, rwkv_wk, rwkv_wv, rwkv_wo, rwkv_w1, rwkv_w2, rwkv_a1, rwkv_a2, rwkv_v0, rwkv_v1, rwkv_v2, rwkv_g1, rwkv_g2, rwkv_rk, kv_norm, kv_wk, kv_wv, attn_wq, attn_wo, attn_lambda, attn_subln, ffn_w1, ffn_w2):
    depth = norm_mix.shape[0]
    n_a = rwkv_mu.shape[0]
    D = x_prompt.shape[-1]
    Wd = kv_wk.shape[1]
    n_att_heads = Wd // (2 * ATT_HEAD)
    n_rwkv_heads = D // RWKV_HEAD

    w_rkv = [jnp.stack([rwkv_wr[l], rwkv_wk[l], rwkv_wv[l]]).astype(BF16) for l in range(n_a)]
    w_o = [rwkv_wo[l].astype(BF16) for l in range(n_a)]
    lora_w = [_pad_lora(rwkv_w1[l], rwkv_w2[l]) for l in range(n_a)]
    lora_a = [_pad_lora(rwkv_a1[l], rwkv_a2[l]) for l in range(n_a)]
    lora_g = [_pad_lora(rwkv_g1[l], rwkv_g2[l]) for l in range(n_a)]
    lora_v = [_pad_lora(rwkv_v1[l], rwkv_v2[l]) for l in range(n_a - 1)]
    wkv_params = []
    for l in range(n_a):
        v0 = rwkv_v0[l - 1] if l > 0 else jnp.zeros((D,), F32)
        wkv_params.append(jnp.concatenate(
            [rwkv_vec[l], rwkv_rk[l].reshape(1, D), v0.reshape(1, D)], axis=0))
    w_kv = jnp.stack([kv_wk, kv_wv]).astype(BF16)
    w_q = [attn_wq[j].astype(BF16) for j in range(depth - n_a)]
    w_ao = [attn_wo[j].astype(BF16) for j in range(depth - n_a)]
    w_f1 = ffn_w1.astype(BF16)
    w_f2 = ffn_w2.astype(BF16)

    def trunk(x, B, T, shift0, wkv0, rope, attend, small):
        act_dtype = F32 if small else BF16
        new_shift, new_wkv = [], []
        rkv_first = None
        k_sh = v_sh = k_att = v_att = None
        xn_next = None
        for l in range(depth):
            if l < n_a:
                mix, last = _prep(x, shift0[l], norm_mix[l, 0], rwkv_mu[l], B=B, T=T, out_dtype=act_dtype)
                new_shift.append(last)
                rkv = _matmul((mix, PL_R), w_rkv[l])
                wl = _lora((mix, PL_W), *lora_w[l], act="tanh")
                al = _lora((mix, PL_A), *lora_a[l])
                gate = _lora((mix, PL_G), *lora_g[l], act="sigmoid")
                vres = None
                if l == 0:
                    rkv_first = rkv
                else:
                    vres = (rkv_first, _lora((mix, PL_V), *lora_v[l - 1]))
                if wkv0 is None:
                    s0_bd = jnp.zeros((B * D // GROUP, GROUP, GROUP), F32)
                else:
                    s0_bd = _state_to_bd(wkv0[l].astype(F32))
                branch, s_bd = _wkv(rkv, wl, al, gate, wkv_params[l], s0_bd,
                                    B=B, T=T, vres=vres, out_dtype=act_dtype)
                new_wkv.append(_bd_to_state(s_bd, B).astype(state_wkv.dtype))
                w_out = w_o[l]
            else:
                j = l - n_a
                lam_init = _lambda_init(l)
                q = _matmul(xn_next, w_q[j], out_dtypes=(act_dtype,), rope=rope, out_scale=Q_SCALE)
                branch = attend(q, k_att, v_att, attn_lambda[j], attn_subln[j], lam_init)
                w_out = w_ao[j]
            x, nrm = _resnorm(x, branch, norm_mix[l, 1], [norm_ffn[l, 0]], w=w_out)
            g_next = []
            if l == n_a - 1:
                g_next.append(kv_norm)
            if n_a - 1 <= l < depth - 1:
                g_next.append(norm_mix[l + 1, 0])
            x, nrm = _mlp((nrm, 0), (w_f1, l), (w_f2, l), x, norm_ffn[l, 1], g_next)
            if l == n_a - 1:
                if small:
                    k_sh = k_att = _matmul((nrm, 0), w_kv[0], rope=rope)
                    v_sh = v_att = _matmul((nrm, 0), w_kv[1])
                else:
                    k_sh, k_att = _matmul((nrm, 0), w_kv[0], rope=rope, out_dtypes=(F32, BF16))
                    v_sh, v_att = _matmul((nrm, 0), w_kv[1], out_dtypes=(F32, BF16))
            if g_next:
                xn_next = (nrm, len(g_next) - 1)
        return x, k_sh, v_sh, jnp.stack(new_shift), jnp.stack(new_wkv)

    Bp, Tp, _ = x_prompt.shape
    shift0_p = jnp.zeros((n_a, Bp, D), F32)
    wkv0_p = None
    rope_p = _rope_tables(jnp.arange(Tp, dtype=jnp.int32), 1)

    def attend_p(q, k, v, lp, sg, lam_init):
        return _attn_prompt(q, k, v, lp, sg, B=Bp, T=Tp, lam_init=lam_init)

    y_p, k_p, v_p, shift_p, wkv_p = trunk(
        x_prompt.reshape(Bp * Tp, D), Bp, Tp, shift0_p, wkv0_p, rope_p, attend_p, False)

    Bd, Td, _ = x_sample.shape
    n_pages = page_table.shape[1]
    page = cache_k.shape[1]
    past_len = n_pages * page
    rope_s = _rope_tables(past_len + jnp.arange(Td, dtype=jnp.int32), Bd)
    ck = cache_k.reshape(cache_k.shape[0], page * n_att_heads * 2, ATT_HEAD)
    cv = cache_v.reshape(cache_v.shape[0], page * n_att_heads, 2 * ATT_HEAD)

    def attend_s(q, k, v, lp, sg, lam_init):
        return _attn_sample(q, ck, cv, k, v, page_table, lp, sg, lam_init=lam_init)

    y_s, k_s, v_s, shift_s, wkv_s = trunk(
        x_sample.reshape(Bd * Td, D), Bd, Td, state_shift, state_wkv, rope_s, attend_s, True)

    H = n_att_heads
    return (y_p.reshape(Bp, Tp, D), y_s.reshape(Bd, Td, D),
            k_p.reshape(Bp, Tp, H, 2, ATT_HEAD), v_p.reshape(Bp, Tp, H, 2 * ATT_HEAD),
            shift_p, wkv_p,
            k_s.reshape(Bd, Td, H, 2, ATT_HEAD), v_s.reshape(Bd, Td, H, 2 * ATT_HEAD),
            shift_s, wkv_s)
```

```python
import functools
import math

import jax
import jax.numpy as jnp
from jax import lax
from jax.experimental import pallas as pl
from jax.experimental.pallas import tpu as pltpu

F32 = jnp.float32
BF16 = jnp.bfloat16

RWKV_HEAD = 64
ATT_HEAD = 128
NORM_EPS = 1e-6
GN_EPS = 64e-5
SUBLN_EPS = 1e-5
ROPE_THETA = 10000.0

LANES = 128
GROUP = 256
HEADS_PER_GROUP = GROUP // RWKV_HEAD
CHUNK = 64
WKV_GROUPS_PER_STEP = 8
VMEM_LIMIT = 48 * 1024 * 1024
MLP_VMEM_LIMIT = 56 * 1024 * 1024
NEG_BIG = -1e30


def _cparams(*sem):
    return pltpu.CompilerParams(dimension_semantics=sem, vmem_limit_bytes=VMEM_LIMIT)


def _dot(a, b):
    return jnp.dot(a, b, preferred_element_type=F32)


def _dot_nt(a, b):
    return lax.dot_general(a, b, (((1,), (1,)), ((), ())), preferred_element_type=F32)


def _rms(x, eps):
    return x * lax.rsqrt(jnp.mean(x * x, axis=-1, keepdims=True) + eps)


def _mm_kernel(x_ref, w_ref, *rest, act, rope, out_scale):
    acc = _dot(x_ref[...].astype(BF16), w_ref[...])
    if act == "tanh":
        acc = jnp.tanh(acc)
    elif act == "sigmoid":
        acc = jax.nn.sigmoid(acc)
    o_refs = rest[2:] if rope else rest
    if rope:
        cos = rest[0][...] * out_scale
        sin = rest[1][...] * out_scale
        for c in range(acc.shape[1] // LANES):
            blk = acc[:, c * LANES:(c + 1) * LANES]
            res = blk * cos + pltpu.roll(blk, LANES // 2, 1) * sin
            for o_ref in o_refs:
                o_ref[:, c * LANES:(c + 1) * LANES] = res.astype(o_ref.dtype)
    else:
        if out_scale != 1.0:
            acc = acc * out_scale
        for o_ref in o_refs:
            o_ref[...] = acc.astype(o_ref.dtype)


def _planes(x):
    return x if isinstance(x, tuple) else (x[None], 0)


MAX_WEIGHT_TILE_BYTES = 8 * 1024 * 1024


def _matmul(x, w, *, out_dtypes=(F32,), act=None, rope=None, out_scale=1.0, tm=512):
    x, p0 = _planes(x)
    squeeze = w.ndim == 2
    if squeeze:
        w = w[None]
    _, M, K = x.shape
    G, _, N = w.shape
    tm = min(tm, M)
    if rope is not None:
        tm = min(tm, rope[0].shape[0])
        assert rope[0].shape[0] % tm == 0
    tn = N
    while K * tn * w.dtype.itemsize > MAX_WEIGHT_TILE_BYTES and tn % (2 * LANES) == 0:
        tn //= 2
    assert M % tm == 0 and N % tn == 0
    in_specs = [pl.BlockSpec((None, tm, K), lambda g, i, j: (p0 + g, i, 0)),
                pl.BlockSpec((None, K, tn), lambda g, i, j: (g, 0, j))]
    args = [x, w]
    if rope is not None:
        cos, sin = rope
        nblk = cos.shape[0] // tm
        spec = pl.BlockSpec((tm, LANES), lambda g, i, j: (i % nblk, 0))
        in_specs += [spec, spec]
        args += [cos, sin]
    outs = pl.pallas_call(
        functools.partial(_mm_kernel, act=act, rope=rope is not None, out_scale=out_scale),
        grid=(G, M // tm, N // tn),
        in_specs=in_specs,
        out_specs=[pl.BlockSpec((None, tm, tn), lambda g, i, j: (g, i, j)) for _ in out_dtypes],
        out_shape=[jax.ShapeDtypeStruct((G, M, N), dt) for dt in out_dtypes],
        compiler_params=_cparams("parallel", "parallel", "arbitrary"),
        name="matmul",
    )(*args)
    outs = [o[0] if squeeze else o for o in outs]
    return outs[0] if len(outs) == 1 else outs


def _lora_kernel(x_ref, w1_ref, w2_ref, o_ref, *, act):
    h = _dot(x_ref[...].astype(BF16), w1_ref[...])
    if act == "tanh":
        h = jnp.tanh(h)
    elif act == "sigmoid":
        h = jax.nn.sigmoid(h)
    o_ref[...] = _dot(h.astype(BF16), w2_ref[...])


def _lora(x, w1, w2, *, act=None, tm=512):
    x, p0 = _planes(x)
    _, M, K = x.shape
    R = w1.shape[1]
    N = w2.shape[1]
    tm = min(tm, M)
    return pl.pallas_call(
        functools.partial(_lora_kernel, act=act),
        grid=(M // tm,),
        in_specs=[pl.BlockSpec((None, tm, K), lambda i: (p0, i, 0)),
                  pl.BlockSpec((K, R), lambda i: (0, 0)),
                  pl.BlockSpec((R, N), lambda i: (0, 0))],
        out_specs=pl.BlockSpec((tm, N), lambda i: (i, 0)),
        out_shape=jax.ShapeDtypeStruct((M, N), F32),
        compiler_params=_cparams("parallel"),
        name="lora",
    )(x, w1, w2)


def _pad_lora(w1, w2):
    r = w1.shape[1]
    rp = -(-r // LANES) * LANES
    return (jnp.pad(w1, ((0, 0), (0, rp - r))).astype(BF16),
            jnp.pad(w2, ((0, rp - r), (0, 0))).astype(BF16))


def _mlp_kernel(x_ref, w1_ref, w2_ref, xr_ref, gp_ref, *rest, n_out, nf):
    if n_out:
        gn_ref, xo_ref, no_ref = rest
    else:
        xo_ref, = rest
    f = pl.program_id(1)

    @pl.when(f == 0)
    def _():
        xo_ref[...] = jnp.zeros(xo_ref.shape, F32)

    h = jnp.maximum(_dot(x_ref[...], w1_ref[...]), 0.0)
    xo_ref[...] += _dot((h * h).astype(BF16), w2_ref[...])

    @pl.when(f == nf - 1)
    def _():
        x = xr_ref[...] + _rms(xo_ref[...], NORM_EPS) * gp_ref[...]
        xo_ref[...] = x
        if n_out:
            y = _rms(x, NORM_EPS)
            for j in range(n_out):
                no_ref[j] = (y * gn_ref[j:j + 1, :]).astype(no_ref.dtype)


def _mlp(x, w1, w2, x_res, g_post, g_next, *, tm=512, tf=1024):
    x, p0 = _planes(x)
    w1, l1 = _planes(w1)
    w2, l2 = _planes(w2)
    _, M, D = x.shape
    F = w1.shape[2]
    tm = min(tm, M)
    tf = min(tf, F)
    n_out = len(g_next)
    row = pl.BlockSpec((tm, D), lambda i, f: (i, 0))
    vec = pl.BlockSpec((1, D), lambda i, f: (0, 0))
    in_specs = [pl.BlockSpec((None, tm, D), lambda i, f: (p0, i, 0)),
                pl.BlockSpec((None, D, tf), lambda i, f: (l1, 0, f)),
                pl.BlockSpec((None, tf, D), lambda i, f: (l2, f, 0)),
                pl.BlockSpec((tm, D), lambda i, f: (i, 0), pipeline_mode=pl.Buffered(1)),
                vec]
    args = [x, w1, w2, x_res, g_post.reshape(1, D)]
    out_specs = [row]
    out_shape = [jax.ShapeDtypeStruct((M, D), F32)]
    if n_out:
        in_specs.append(pl.BlockSpec((n_out, D), lambda i, f: (0, 0)))
        args.append(jnp.stack(g_next))
        out_specs.append(pl.BlockSpec((n_out, tm, D), lambda i, f: (0, i, 0)))
        out_shape.append(jax.ShapeDtypeStruct((n_out, M, D), BF16))
    res = pl.pallas_call(
        functools.partial(_mlp_kernel, n_out=n_out, nf=F // tf),
        grid=(M // tm, F // tf),
        in_specs=in_specs,
        out_specs=out_specs,
        out_shape=out_shape,
        compiler_params=pltpu.CompilerParams(dimension_semantics=("parallel", "arbitrary"),
                                             vmem_limit_bytes=MLP_VMEM_LIMIT),
        name="mlp",
    )(*args)
    return (res[0], res[1]) if n_out else (res[0], None)


MIX_ORDER = (0, 2, 3, 1, 4, 5)
PL_R, PL_K, PL_V, PL_W, PL_A, PL_G = range(6)


def _prep_kernel(x_ref, sh_ref, g_ref, mu_ref, mix_ref, last_ref, carry_ref):
    i = pl.program_id(1)
    xn = _rms(x_ref[...], NORM_EPS) * g_ref[...]
    tr = xn.shape[0]

    @pl.when(i == 0)
    def _():
        carry_ref[...] = sh_ref[...]

    row = lax.broadcasted_iota(jnp.int32, xn.shape, 0)
    xprev = jnp.where(row == 0, carry_ref[...], pltpu.roll(xn, 1, 0))
    xx = xprev - xn
    for plane, m in enumerate(MIX_ORDER):
        mix_ref[plane] = (xn + xx * mu_ref[m:m + 1, :]).astype(mix_ref.dtype)
    last = xn[tr - 1:tr, :]
    carry_ref[...] = last
    last_ref[...] = last


def _prep(x, shift0, g, mu, *, B, T, out_dtype):
    M, D = x.shape
    tr = min(256, T)
    nt = T // tr
    mix, last = pl.pallas_call(
        _prep_kernel,
        grid=(B, nt),
        in_specs=[pl.BlockSpec((tr, D), lambda b, i: (b * nt + i, 0)),
                  pl.BlockSpec((None, 1, D), lambda b, i: (b, 0, 0)),
                  pl.BlockSpec((1, D), lambda b, i: (0, 0)),
                  pl.BlockSpec((6, D), lambda b, i: (0, 0))],
        out_specs=[pl.BlockSpec((6, tr, D), lambda b, i: (0, b * nt + i, 0)),
                   pl.BlockSpec((None, 1, D), lambda b, i: (b, 0, 0))],
        out_shape=[jax.ShapeDtypeStruct((6, M, D), out_dtype),
                   jax.ShapeDtypeStruct((B, 1, D), F32)],
        scratch_shapes=[pltpu.VMEM((1, D), F32)],
        compiler_params=_cparams("arbitrary", "arbitrary"),
        name="rwkv_prep",
    )(x, shift0.reshape(B, 1, D), g.reshape(1, D), mu)
    return mix, last.reshape(B, D)


def _resnorm_kernel(x_ref, h_ref, *rest, n_out, project):
    h = h_ref[...]
    if project:
        w_ref, *rest = rest
        h = _dot(h.astype(BF16), w_ref[...])
    gp_ref, *rest = rest
    x = x_ref[...] + _rms(h, NORM_EPS) * gp_ref[...]
    if n_out:
        gn_ref, xo_ref, no_ref = rest
        y = _rms(x, NORM_EPS)
        for j in range(n_out):
            no_ref[j] = (y * gn_ref[j:j + 1, :]).astype(no_ref.dtype)
    else:
        xo_ref, = rest
    xo_ref[...] = x


def _resnorm(x, h, g_post, g_next, w=None):
    M, D = x.shape
    n_out = len(g_next)
    tr = min(512 if w is not None else 256, M)
    row = pl.BlockSpec((tr, D), lambda i: (i, 0))
    vec = pl.BlockSpec((1, D), lambda i: (0, 0))
    in_specs = [row, pl.BlockSpec((tr, h.shape[1]), lambda i: (i, 0))]
    args = [x, h]
    if w is not None:
        in_specs.append(pl.BlockSpec(w.shape, lambda i: (0, 0)))
        args.append(w)
    in_specs.append(vec)
    args.append(g_post.reshape(1, D))
    out_specs = [row]
    out_shape = [jax.ShapeDtypeStruct((M, D), F32)]
    if n_out:
        in_specs.append(pl.BlockSpec((n_out, D), lambda i: (0, 0)))
        args.append(jnp.stack(g_next))
        out_specs.append(pl.BlockSpec((n_out, tr, D), lambda i: (0, i, 0)))
        out_shape.append(jax.ShapeDtypeStruct((n_out, M, D), BF16))
    res = pl.pallas_call(
        functools.partial(_resnorm_kernel, n_out=n_out, project=w is not None),
        grid=(M // tr,),
        in_specs=in_specs,
        out_specs=out_specs,
        out_shape=out_shape,
        compiler_params=_cparams("parallel"),
        name="resnorm",
    )(*args)
    return (res[0], res[1]) if n_out else (res[0], None)


def _norm_kernel(x_ref, g_ref, o_ref):
    o_ref[...] = (_rms(x_ref[...], NORM_EPS) * g_ref[...]).astype(o_ref.dtype)


def _norm(x, g):
    M, D = x.shape
    tr = min(256, M)
    return pl.pallas_call(
        _norm_kernel,
        grid=(M // tr,),
        in_specs=[pl.BlockSpec((tr, D), lambda i: (i, 0)), pl.BlockSpec((1, D), lambda i: (0, 0))],
        out_specs=pl.BlockSpec((tr, D), lambda i: (i, 0)),
        out_shape=jax.ShapeDtypeStruct((M, D), BF16),
        compiler_params=_cparams("parallel"),
        name="norm",
    )(x, g.reshape(1, D))


def _split2(x):
    hi = x.astype(BF16)
    lo = (x - hi.astype(F32)).astype(BF16)
    return hi, lo


def _seg_sum(x, ones_bd):
    hi, lo = _split2(x)
    n = x.shape[0]
    s = _dot(jnp.concatenate([hi, lo], axis=0), ones_bd)
    return s[:n] + s[n:]


def _wkv_kernel(*refs, rows, vres, ng):
    n_tiles = 8 if vres else 6
    p_ref, s0_ref, y_ref, so_ref, s_ref = refs[n_tiles:]
    c = pl.program_id(2)
    row_head = lax.broadcasted_iota(jnp.int32, (GROUP, GROUP), 0) // RWKV_HEAD
    lane_head = lax.broadcasted_iota(jnp.int32, (GROUP, GROUP), 1) // RWKV_HEAD

    @pl.when(c == 0)
    def _():
        for gi in range(ng):
            s_ref[gi] = jnp.where(row_head == lane_head, s0_ref[gi], 0.0)

    gens = []
    for gi in range(ng):
        lanes = pl.ds(gi * GROUP, GROUP)
        vals = [ref[:, lanes] for ref in refs[:n_tiles]] + [p_ref[:, lanes], s_ref[gi]]
        gens.append(_wkv_group(*vals, rows=rows, vres=vres))
    outs = [None] * ng
    live = list(range(ng))
    while live:
        for gi in list(live):
            try:
                next(gens[gi])
            except StopIteration as done:
                outs[gi] = done.value
                live.remove(gi)
    for gi, (y, s_new) in enumerate(outs):
        y_ref[:, pl.ds(gi * GROUP, GROUP)] = y.astype(y_ref.dtype)
        s_ref[gi] = s_new

    @pl.when(c == pl.num_programs(2) - 1)
    def _():
        for gi in range(ng):
            s = s_ref[gi]
            packed = s
            for h in range(1, HEADS_PER_GROUP):
                packed = jnp.where(row_head == h, pltpu.roll(s, GROUP - h * RWKV_HEAD, 1), packed)
            so_ref[gi] = packed[:, :RWKV_HEAD]


def _wkv_group(r_ref, k_ref, v_ref, wl_ref, al_ref, g_ref, *rest, rows, vres):
    if vres:
        vf_ref, vl_ref, p, s = rest
    else:
        p, s = rest
    L = CHUNK
    W = GROUP
    HG = HEADS_PER_GROUP

    def load(val):
        x = val.astype(F32)
        if rows < L:
            x = jnp.concatenate([x, jnp.zeros((L - rows, W), F32)], axis=0)
        return x

    w0, a0, k_k, k_a = p[0:1], p[1:2], p[2:3], p[3:4]
    lnx_w, lnx_b, rk, v0 = p[4:5], p[5:6], p[6:7], p[7:8]

    lane_head = lax.broadcasted_iota(jnp.int32, (1, W), 1) // RWKV_HEAD
    rr = lax.broadcasted_iota(jnp.int32, (W, W), 0) // RWKV_HEAD
    cc = lax.broadcasted_iota(jnp.int32, (W, W), 1) // RWKV_HEAD
    bd = rr == cc
    ones_bd = jnp.where(bd, 1.0, 0.0).astype(BF16)
    trow = lax.broadcasted_iota(jnp.int32, (L, HG * L), 0)
    tcol = lax.broadcasted_iota(jnp.int32, (L, HG * L), 1) & (L - 1)
    strict = tcol < trow
    incl = tcol <= trow
    rb = lax.broadcasted_iota(jnp.int32, (HG * L, HG * L), 0) // L
    cb = lax.broadcasted_iota(jnp.int32, (HG * L, HG * L), 1) // L
    bd_l = rb == cb

    def ystack(x):
        return jnp.concatenate(
            [jnp.where(lane_head == h, x, 0.0) for h in range(HG)], axis=0).astype(BF16)

    r = load(r_ref)
    k = load(k_ref)
    v = load(v_ref)
    a_sig = jax.nn.sigmoid(load(al_ref) + a0)
    if vres:
        v = v + (load(vf_ref) - v) * jax.nn.sigmoid(load(vl_ref) + v0)
    z = -(load(wl_ref) + w0)
    softplus = jnp.maximum(z, 0.0) + jnp.log(1.0 + jnp.exp(-jnp.abs(z)))
    lw = -jnp.exp(-softplus - 0.5)
    if rows < L:
        trow1 = lax.broadcasted_iota(jnp.int32, (L, W), 0)
        lw = jnp.where(trow1 < rows, lw, 0.0)
    kkr = k * k_k
    nrm2 = _seg_sum(kkr * kkr, ones_bd)
    yield
    nrm = jnp.sqrt(nrm2)
    kk = kkr / jnp.maximum(nrm, 1e-12)
    k2 = k * (1.0 + (a_sig - 1.0) * k_a)
    a = -kk
    b = kk * a_sig

    l1 = lw.astype(BF16)
    l2r = lw - l1.astype(F32)
    l2 = l2r.astype(BF16)
    l3 = (l2r - l2.astype(F32)).astype(BF16)
    tri = (lax.broadcasted_iota(jnp.int32, (L, L), 1)
           <= lax.broadcasted_iota(jnp.int32, (L, L), 0))
    tri = jnp.where(tri, 1.0, 0.0).astype(BF16)
    cs = _dot(tri, jnp.concatenate([l1, l2, l3], axis=1))
    yield
    cum = cs[:, :W] + cs[:, W:2 * W] + cs[:, 2 * W:]
    cl = cum[L - 1:L, :]
    e_c = jnp.exp(cum)
    e_cm = jnp.exp(cum - lw)
    e_nc = jnp.exp(-cum)
    e_lc = jnp.exp(cl - cum)
    g_l = jnp.exp(cl)
    at = a * e_cm
    rt = r * e_c
    bt = b * e_nc
    kt = k2 * e_nc
    bh = b * e_lc
    kh = k2 * e_lc

    x4 = _dot_nt(jnp.concatenate([at, rt], axis=0).astype(BF16),
                 jnp.concatenate([ystack(bt), ystack(kt)], axis=0))
    yield
    n_ab = jnp.where(strict, x4[:L, :HG * L], 0.0)
    m_ak = jnp.where(strict, x4[:L, HG * L:], 0.0)
    m_rb = jnp.where(incl, x4[L:, :HG * L], 0.0)
    m_rk = jnp.where(incl, x4[L:, HG * L:], 0.0)

    def bdiag(x):
        return jnp.where(bd_l, jnp.concatenate([x] * HG, axis=0), 0.0).astype(BF16)

    t_inv = jnp.where(tcol == trow, 1.0, 0.0) + n_ab
    yv = ystack(v)
    makv = _dot(m_ak.astype(BF16), yv)
    pw = _dot(n_ab.astype(BF16), bdiag(n_ab))
    yield
    steps = int(math.log2(L)) - 1
    for it in range(steps):
        rhs = bdiag(pw)
        if it < steps - 1:
            res = _dot(jnp.concatenate([t_inv, pw], axis=0).astype(BF16), rhs)
            yield
            t_inv = t_inv + res[:L]
            pw = res[L:]
        else:
            res = _dot(t_inv.astype(BF16), rhs)
            yield
            t_inv = t_inv + res

    au = _dot(t_inv.astype(BF16), jnp.concatenate([ystack(at), ystack(makv)], axis=1))
    yield
    abar = au[:, :W]
    u0 = au[:, W:]
    y_abar = ystack(abar)
    d1 = _dot(m_rb.astype(BF16), jnp.concatenate([y_abar, ystack(u0)], axis=1))
    d2 = _dot(m_rk.astype(BF16), yv)
    s_bf = s.astype(BF16)
    sa = _dot_nt(s_bf, y_abar)
    yield
    rbar = rt + d1[:, :W]
    o0 = d1[:, W:] + d2
    o = _dot_nt(rbar.astype(BF16), s_bf) + o0
    uv_t = jnp.concatenate([u0, v], axis=0).T
    q_t = _dot(uv_t.astype(BF16), jnp.concatenate([bh, kh], axis=0).astype(BF16))
    s_new = s * g_l + _dot(sa.astype(BF16), ystack(bh)) + jnp.where(bd, q_t, 0.0)
    bonus = _seg_sum(r * k2 * rk, ones_bd) * v
    yield
    inv_n = 1.0 / RWKV_HEAD
    mean = _seg_sum(o, ones_bd) * inv_n
    yield
    dlt = o - mean
    var = _seg_sum(dlt * dlt, ones_bd) * inv_n
    yield
    yn = dlt * lax.rsqrt(var + GN_EPS) * lnx_w + lnx_b
    out = (yn + bonus) * load(g_ref)
    return out[:rows], s_new


def _wkv(rkv, wl, al, gate, params, s0, *, B, T, vres=None, out_dtype=BF16):
    _, M, D = rkv.shape
    n_groups = B * D // GROUP
    if s0 is None:
        s0_rep = jnp.zeros((n_groups, GROUP, GROUP), F32)
    else:
        s0_rep = jnp.tile(s0.astype(F32).reshape(n_groups, GROUP, RWKV_HEAD), (1, 1, HEADS_PER_GROUP))
    ng = max(d for d in range(1, WKV_GROUPS_PER_STEP + 1) if (D // GROUP) % d == 0)
    G = D // (ng * GROUP)
    rows = min(CHUNK, T)
    nc = T // rows

    def plane(pi):
        return pl.BlockSpec((None, rows, ng * GROUP), lambda b, g, c: (pi, b * nc + c, g))

    tile = pl.BlockSpec((rows, ng * GROUP), lambda b, g, c: (b * nc + c, g))
    state = pl.BlockSpec((ng, GROUP, GROUP), lambda b, g, c: (b * G + g, 0, 0))
    in_specs = [plane(0), plane(1), plane(2), tile, tile, tile]
    args = [rkv, rkv, rkv, wl, al, gate]
    if vres is not None:
        rkv_first, vl = vres
        in_specs += [plane(2), tile]
        args += [rkv_first, vl]
    in_specs += [pl.BlockSpec((8, ng * GROUP), lambda b, g, c: (0, g)), state]
    args += [params, s0_rep]
    y, s_out = pl.pallas_call(
        functools.partial(_wkv_kernel, rows=rows, vres=vres is not None, ng=ng),
        grid=(B, G, nc),
        in_specs=in_specs,
        out_specs=[tile, pl.BlockSpec((ng, GROUP, RWKV_HEAD), lambda b, g, c: (b * G + g, 0, 0))],
        out_shape=[jax.ShapeDtypeStruct((M, D), out_dtype),
                   jax.ShapeDtypeStruct((n_groups, GROUP, RWKV_HEAD), F32)],
        scratch_shapes=[pltpu.VMEM((ng, GROUP, GROUP), F32)],
        compiler_params=_cparams("parallel", "parallel", "arbitrary"),
        name="wkv7",
    )(*args)
    return y, s_out.reshape(B, D // RWKV_HEAD, RWKV_HEAD, RWKV_HEAD)


def _lambda(lp, lam_init):
    return (jnp.exp(jnp.sum(lp[0:1] * lp[1:2], keepdims=True))
            - jnp.exp(jnp.sum(lp[2:3] * lp[3:4], keepdims=True)) + lam_init)


Q_SCALE = ATT_HEAD ** -0.5 * math.log2(math.e)


def _rep(x, n):
    return x if n == LANES else jnp.concatenate([x] * (n // LANES), axis=1)


def _softmax_step(s, vt, m, l, acc, ones=None):
    m_new = jnp.maximum(m, jnp.max(s, axis=-1, keepdims=True))
    alpha = jnp.exp2(m - m_new)
    p = jnp.exp2(s - _rep(m_new, s.shape[1]))
    pb = p.astype(BF16)
    if ones is None:
        l_new = alpha * l + jnp.sum(p, axis=-1, keepdims=True)
    else:
        l_new = alpha * l + _dot(pb, ones)
    acc_new = _rep(alpha, acc.shape[1]) * acc + _dot(pb, vt)
    return m_new, l_new, acc_new


def _attn_prompt_kernel(q_ref, k_ref, v_ref, lp_ref, sg_ref, o_ref, m_ref, l_ref, acc_ref,
                        *, tq, lam_init):
    qi = pl.program_id(2)
    q = q_ref[...]
    qs = (q[:, :ATT_HEAD], q[:, ATT_HEAD:])
    m_ref[...] = jnp.full(m_ref.shape, NEG_BIG, F32)
    l_ref[...] = jnp.zeros(l_ref.shape, F32)
    acc_ref[...] = jnp.zeros(acc_ref.shape, F32)

    def scores(j):
        kt = k_ref[pl.ds(pl.multiple_of(j * tq, tq), tq), :]
        return tuple(_dot_nt(qs[mp], kt[:, mp * ATT_HEAD:(mp + 1) * ATT_HEAD]) for mp in range(2))

    def update(j, s):
        vt = v_ref[pl.ds(pl.multiple_of(j * tq, tq), tq), :]
        for mp in range(2):
            m_ref[mp], l_ref[mp], acc_ref[mp] = _softmax_step(
                s[mp], vt, m_ref[mp], l_ref[mp], acc_ref[mp])

    def body(j, s):
        s_next = scores(j + 1)
        update(j, s)
        return s_next

    s = lax.fori_loop(0, qi, body, scores(0))
    keep = (lax.broadcasted_iota(jnp.int32, (tq, tq), 1)
            <= lax.broadcasted_iota(jnp.int32, (tq, tq), 0))
    update(qi, tuple(jnp.where(keep, x, NEG_BIG) for x in s))
    lam = _lambda(lp_ref[...], lam_init)
    hw = acc_ref.shape[2]
    o = acc_ref[0] / _rep(l_ref[0], hw) - lam * (acc_ref[1] / _rep(l_ref[1], hw))
    o = _rms(o, SUBLN_EPS) * sg_ref[...] * (1.0 - lam_init)
    o_ref[...] = o.astype(o_ref.dtype)


def _attn_prompt(q, k, v, lp, sg, *, B, T, lam_init, tq=512):
    M, Wd = q.shape
    HW = 2 * ATT_HEAD
    H = Wd // HW
    tq = min(tq, T)
    nq = T // tq
    return pl.pallas_call(
        functools.partial(_attn_prompt_kernel, tq=tq, lam_init=lam_init),
        grid=(B, H, nq),
        in_specs=[pl.BlockSpec((tq, HW), lambda b, h, i: (b * nq + i, h)),
                  pl.BlockSpec((T, HW), lambda b, h, i: (b, h)),
                  pl.BlockSpec((T, HW), lambda b, h, i: (b, h)),
                  pl.BlockSpec((4, ATT_HEAD), lambda b, h, i: (0, 0)),
                  pl.BlockSpec((1, HW), lambda b, h, i: (0, 0))],
        out_specs=pl.BlockSpec((tq, HW), lambda b, h, i: (b * nq + i, h)),
        out_shape=jax.ShapeDtypeStruct((M, Wd), BF16),
        scratch_shapes=[pltpu.VMEM((2, tq, LANES), F32), pltpu.VMEM((2, tq, LANES), F32),
                        pltpu.VMEM((2, tq, HW), F32)],
        compiler_params=_cparams("parallel", "parallel", "arbitrary"),
        name="diff_attn_prompt",
    )(q, k, v, lp, sg.reshape(1, HW))


PAGES_PER_STEP = 4


def _attn_sample_kernel(pt_ref, q_ref, *rest, heads, tq, npp, new_rows, lam_init):
    k_refs = rest[:npp]
    v_refs = rest[npp:2 * npp]
    kn_ref, vn_ref, lp_ref, sg_ref, o_ref, m_ref, l_ref, acc_ref = rest[2 * npp:]
    p = pl.program_id(1)
    npg = pl.num_programs(1)
    HW = 2 * ATT_HEAD
    R = heads * tq

    @pl.when(p == 0)
    def _():
        m_ref[...] = jnp.full(m_ref.shape, NEG_BIG, F32)
        l_ref[...] = jnp.zeros(l_ref.shape, F32)
        acc_ref[...] = jnp.zeros(acc_ref.shape, F32)

    q = q_ref[...]
    qm = [jnp.concatenate([q[:, h * HW + mp * ATT_HEAD:h * HW + (mp + 1) * ATT_HEAD]
                           for h in range(heads)], axis=0).astype(BF16) for mp in range(2)]

    def scores(k0, k1, valid):
        g = jnp.concatenate([_dot_nt(qm[0], k0.astype(BF16)), _dot_nt(qm[1], k1.astype(BF16))],
                            axis=0)
        return jnp.where(valid, g, NEG_BIG)

    def head_mask(ncol):
        row = lax.broadcasted_iota(jnp.int32, (2 * R, ncol), 0)
        col = lax.broadcasted_iota(jnp.int32, (2 * R, ncol), 1)
        return (col % heads) == ((row // tq) % heads), row, col

    state = (m_ref[...], l_ref[...], acc_ref[...])
    nk = k_refs[0].shape[0] // 2
    same_head, _, _ = head_mask(nk)
    for j in range(npp):
        s = scores(k_refs[j][pl.ds(0, nk, stride=2), :], k_refs[j][pl.ds(1, nk, stride=2), :],
                   same_head)
        state = _softmax_step(s, v_refs[j][...].astype(BF16), *state)
    m_ref[...], l_ref[...], acc_ref[...] = state

    @pl.when(p == npg - 1)
    def _():
        pad_to = max(new_rows, LANES)
        k0 = kn_ref[pl.ds(0, new_rows, stride=2), :]
        k1 = kn_ref[pl.ds(1, new_rows, stride=2), :]
        vn = vn_ref[...]
        if pad_to > new_rows:
            zk = jnp.zeros((pad_to - new_rows, ATT_HEAD), F32)
            k0 = jnp.concatenate([k0, zk], axis=0)
            k1 = jnp.concatenate([k1, zk], axis=0)
            vn = jnp.concatenate([vn, jnp.zeros((pad_to - new_rows, HW), F32)], axis=0)
        same, row, col = head_mask(pad_to)
        valid = same & ((col // heads) <= (row % tq)) & (col < new_rows)
        m, l, acc = _softmax_step(scores(k0, k1, valid), vn.astype(BF16), *state)
        lam = _lambda(lp_ref[...], lam_init)
        sg = sg_ref[...]
        for h in range(heads):
            r0 = slice(h * tq, (h + 1) * tq)
            r1 = slice(R + h * tq, R + (h + 1) * tq)
            o = acc[r0] / _rep(l[r0], HW) - lam * (acc[r1] / _rep(l[r1], HW))
            o = _rms(o, SUBLN_EPS) * sg * (1.0 - lam_init)
            o_ref[:, h * HW:(h + 1) * HW] = o


def _attn_sample(q, cache_k, cache_v, k_new, v_new, page_table, lp, sg, *, lam_init):
    Bd, npg = page_table.shape
    M, Wd = q.shape
    tq = M // Bd
    HW = 2 * ATT_HEAD
    heads = Wd // HW
    npp = max(d for d in range(1, PAGES_PER_STEP + 1) if npg % d == 0)
    nsteps = npg // npp
    krows, vrows = cache_k.shape[1], cache_v.shape[1]
    new_rows = tq * heads
    rows = pl.BlockSpec((tq, Wd), lambda b, p, pt: (b, 0))

    def page_spec(nrows, width, j):
        return pl.BlockSpec((None, nrows, width), lambda b, p, pt: (pt[b * npg + p * npp + j], 0, 0))

    grid_spec = pltpu.PrefetchScalarGridSpec(
        num_scalar_prefetch=1,
        grid=(Bd, nsteps),
        in_specs=([rows] + [page_spec(krows, ATT_HEAD, j) for j in range(npp)]
                  + [page_spec(vrows, HW, j) for j in range(npp)]
                  + [pl.BlockSpec((None, 2 * new_rows, ATT_HEAD), lambda b, p, pt: (b, 0, 0)),
                     pl.BlockSpec((None, new_rows, HW), lambda b, p, pt: (b, 0, 0)),
                     pl.BlockSpec((4, ATT_HEAD), lambda b, p, pt: (0, 0)),
                     pl.BlockSpec((1, HW), lambda b, p, pt: (0, 0))]),
        out_specs=rows,
        scratch_shapes=[pltpu.VMEM((2 * new_rows, LANES), F32),
                        pltpu.VMEM((2 * new_rows, LANES), F32),
                        pltpu.VMEM((2 * new_rows, HW), F32)],
    )
    return pl.pallas_call(
        functools.partial(_attn_sample_kernel, heads=heads, tq=tq, npp=npp, new_rows=new_rows,
                          lam_init=lam_init),
        grid_spec=grid_spec,
        out_shape=jax.ShapeDtypeStruct((M, Wd), F32),
        compiler_params=_cparams("parallel", "arbitrary"),
        name="diff_attn_sample",
    )(page_table.reshape(-1), q, *([cache_k] * npp), *([cache_v] * npp),
      k_new.reshape(Bd, 2 * new_rows, ATT_HEAD), v_new.reshape(Bd, new_rows, HW),
      lp, sg.reshape(1, HW))


def _rope_tables(pos, reps):
    half = ATT_HEAD // 2
    inv = jnp.power(ROPE_THETA, -jnp.arange(half, dtype=F32) / half)
    ang = pos.astype(F32)[:, None] * inv[None, :]
    cos = jnp.concatenate([jnp.cos(ang), jnp.cos(ang)], axis=-1)
    sin = jnp.concatenate([-jnp.sin(ang), jnp.sin(ang)], axis=-1)
    return jnp.tile(cos, (reps, 1)), jnp.tile(sin, (reps, 1))


def _lambda_init(layer_idx):
    return 0.8 - 0.6 * math.exp(-0.3 * layer_idx)


def kernel(x_prompt, x_sample, cache_k, cache_v, state_shift, state_wkv, page_table, norm_mix, norm_ffn, rwkv_mu, rwkv_vec, rwkv_wr, rwkv_wk, rwkv_wv, rwkv_wo, rwkv_w1, rwkv_w2, rwkv_a1, rwkv_a2, rwkv_v0, rwkv_v1, rwkv_v2, rwkv_g1, rwkv_g2, rwkv_rk, kv_norm, kv_wk, kv_wv, attn_wq, attn_wo, attn_lambda, attn_subln, ffn_w1, ffn_w2):
    depth = norm_mix.shape[0]
    n_a = rwkv_mu.shape[0]
    D = x_prompt.shape[-1]
    Wd = kv_wk.shape[1]
    n_att_heads = Wd // (2 * ATT_HEAD)
    n_rwkv_heads = D // RWKV_HEAD

    w_rkv = [jnp.stack([rwkv_wr[l], rwkv_wk[l], rwkv_wv[l]]).astype(BF16) for l in range(n_a)]
    w_o = [rwkv_wo[l].astype(BF16) for l in range(n_a)]
    lora_w = [_pad_lora(rwkv_w1[l], rwkv_w2[l]) for l in range(n_a)]
    lora_a = [_pad_lora(rwkv_a1[l], rwkv_a2[l]) for l in range(n_a)]
    lora_g = [_pad_lora(rwkv_g1[l], rwkv_g2[l]) for l in range(n_a)]
    lora_v = [_pad_lora(rwkv_v1[l], rwkv_v2[l]) for l in range(n_a - 1)]
    wkv_params = []
    for l in range(n_a):
        v0 = rwkv_v0[l - 1] if l > 0 else jnp.zeros((D,), F32)
        wkv_params.append(jnp.concatenate(
            [rwkv_vec[l], rwkv_rk[l].reshape(1, D), v0.reshape(1, D)], axis=0))
    w_kv = jnp.stack([kv_wk, kv_wv]).astype(BF16)
    w_q = [attn_wq[j].astype(BF16) for j in range(depth - n_a)]
    w_ao = [attn_wo[j].astype(BF16) for j in range(depth - n_a)]
    w_f1 = ffn_w1.astype(BF16)
    w_f2 = ffn_w2.astype(BF16)

    def trunk(x, B, T, shift0, wkv0, rope, attend, small):
        act_dtype = F32 if small else BF16
        new_shift, new_wkv = [], []
        rkv_first = None
        k_sh = v_sh = k_att = v_att = None
        xn_next = None
        for l in range(depth):
            if l < n_a:
                mix, last = _prep(x, shift0[l], norm_mix[l, 0], rwkv_mu[l], B=B, T=T, out_dtype=act_dtype)
                new_shift.append(last)
                rkv = _matmul((mix, PL_R), w_rkv[l])
                wl = _lora((mix, PL_W), *lora_w[l], act="tanh")
                al = _lora((mix, PL_A), *lora_a[l])
                gate = _lora((mix, PL_G), *lora_g[l], act="sigmoid")
                vres = None
                if l == 0:
                    rkv_first = rkv
                else:
                    vres = (rkv_first, _lora((mix, PL_V), *lora_v[l - 1]))
                branch, s_new = _wkv(rkv, wl, al, gate, wkv_params[l],
                                     None if wkv0 is None else wkv0[l],
                                     B=B, T=T, vres=vres, out_dtype=act_dtype)
                new_wkv.append(s_new.astype(state_wkv.dtype))
                w_out = w_o[l]
            else:
                j = l - n_a
                lam_init = _lambda_init(l)
                q = _matmul(xn_next, w_q[j], out_dtypes=(act_dtype,), rope=rope, out_scale=Q_SCALE)
                branch = attend(q, k_att, v_att, attn_lambda[j], attn_subln[j], lam_init)
                w_out = w_ao[j]
            x, nrm = _resnorm(x, branch, norm_mix[l, 1], [norm_ffn[l, 0]], w=w_out)
            g_next = []
            if l == n_a - 1:
                g_next.append(kv_norm)
            if n_a - 1 <= l < depth - 1:
                g_next.append(norm_mix[l + 1, 0])
            x, nrm = _mlp((nrm, 0), (w_f1, l), (w_f2, l), x, norm_ffn[l, 1], g_next)
            if l == n_a - 1:
                if small:
                    k_sh = k_att = _matmul((nrm, 0), w_kv[0], rope=rope)
                    v_sh = v_att = _matmul((nrm, 0), w_kv[1])
                else:
                    k_sh, k_att = _matmul((nrm, 0), w_kv[0], rope=rope, out_dtypes=(F32, BF16))
                    v_sh, v_att = _matmul((nrm, 0), w_kv[1], out_dtypes=(F32, BF16))
            if g_next:
                xn_next = (nrm, len(g_next) - 1)
        return x, k_sh, v_sh, jnp.stack(new_shift), jnp.stack(new_wkv)

    Bp, Tp, _ = x_prompt.shape
    shift0_p = jnp.zeros((n_a, Bp, D), F32)
    wkv0_p = None
    rope_p = _rope_tables(jnp.arange(Tp, dtype=jnp.int32), 1)

    def attend_p(q, k, v, lp, sg, lam_init):
        return _attn_prompt(q, k, v, lp, sg, B=Bp, T=Tp, lam_init=lam_init)

    y_p, k_p, v_p, shift_p, wkv_p = trunk(
        x_prompt.reshape(Bp * Tp, D), Bp, Tp, shift0_p, wkv0_p, rope_p, attend_p, False)

    Bd, Td, _ = x_sample.shape
    n_pages = page_table.shape[1]
    page = cache_k.shape[1]
    past_len = n_pages * page
    rope_s = _rope_tables(past_len + jnp.arange(Td, dtype=jnp.int32), Bd)
    ck = cache_k.reshape(cache_k.shape[0], page * n_att_heads * 2, ATT_HEAD)
    cv = cache_v.reshape(cache_v.shape[0], page * n_att_heads, 2 * ATT_HEAD)

    def attend_s(q, k, v, lp, sg, lam_init):
        return _attn_sample(q, ck, cv, k, v, page_table, lp, sg, lam_init=lam_init)

    y_s, k_s, v_s, shift_s, wkv_s = trunk(
        x_sample.reshape(Bd * Td, D), Bd, Td, state_shift, state_wkv, rope_s, attend_s, True)

    H = n_att_heads
    return (y_p.reshape(Bp, Tp, D), y_s.reshape(Bd, Td, D),
            k_p.reshape(Bp, Tp, H, 2, ATT_HEAD), v_p.reshape(Bp, Tp, H, 2 * ATT_HEAD),
            shift_p, wkv_p,
            k_s.reshape(Bd, Td, H, 2, ATT_HEAD), v_s.reshape(Bd, Td, H, 2 * ATT_HEAD),
            shift_s, wkv_s)
```

```python
import functools
import math

import jax
import jax.numpy as jnp
from jax import lax
from jax.experimental import pallas as pl
from jax.experimental.pallas import tpu as pltpu

F32 = jnp.float32
BF16 = jnp.bfloat16

RWKV_HEAD = 64
ATT_HEAD = 128
NORM_EPS = 1e-6
GN_EPS = 64e-5
SUBLN_EPS = 1e-5
ROPE_THETA = 10000.0

LANES = 128
GROUP = 256
HEADS_PER_GROUP = GROUP // RWKV_HEAD
CHUNK = 64
WKV_GROUPS_PER_STEP = 8
VMEM_LIMIT = 48 * 1024 * 1024
MLP_VMEM_LIMIT = 56 * 1024 * 1024
NEG_BIG = -1e30


def _cparams(*sem):
    return pltpu.CompilerParams(dimension_semantics=sem, vmem_limit_bytes=VMEM_LIMIT)


def _dot(a, b):
    return jnp.dot(a, b, preferred_element_type=F32)


def _dot_nt(a, b):
    return lax.dot_general(a, b, (((1,), (1,)), ((), ())), preferred_element_type=F32)


def _rms(x, eps):
    return x * lax.rsqrt(jnp.mean(x * x, axis=-1, keepdims=True) + eps)


def _mm_kernel(x_ref, w_ref, *rest, act, rope, out_scale):
    acc = _dot(x_ref[...].astype(BF16), w_ref[...])
    if act == "tanh":
        acc = jnp.tanh(acc)
    elif act == "sigmoid":
        acc = jax.nn.sigmoid(acc)
    o_refs = rest[2:] if rope else rest
    if rope:
        cos = rest[0][...] * out_scale
        sin = rest[1][...] * out_scale
        for c in range(acc.shape[1] // LANES):
            blk = acc[:, c * LANES:(c + 1) * LANES]
            res = blk * cos + pltpu.roll(blk, LANES // 2, 1) * sin
            for o_ref in o_refs:
                o_ref[:, c * LANES:(c + 1) * LANES] = res.astype(o_ref.dtype)
    else:
        if out_scale != 1.0:
            acc = acc * out_scale
        for o_ref in o_refs:
            o_ref[...] = acc.astype(o_ref.dtype)


def _planes(x):
    return x if isinstance(x, tuple) else (x[None], 0)


MAX_WEIGHT_TILE_BYTES = 8 * 1024 * 1024


def _matmul(x, w, *, out_dtypes=(F32,), act=None, rope=None, out_scale=1.0, tm=512):
    x, p0 = _planes(x)
    squeeze = w.ndim == 2
    if squeeze:
        w = w[None]
    _, M, K = x.shape
    G, _, N = w.shape
    tm = min(tm, M)
    if rope is not None:
        tm = min(tm, rope[0].shape[0])
        assert rope[0].shape[0] % tm == 0
    tn = N
    while K * tn * w.dtype.itemsize > MAX_WEIGHT_TILE_BYTES and tn % (2 * LANES) == 0:
        tn //= 2
    assert M % tm == 0 and N % tn == 0
    in_specs = [pl.BlockSpec((None, tm, K), lambda g, i, j: (p0 + g, i, 0)),
                pl.BlockSpec((None, K, tn), lambda g, i, j: (g, 0, j))]
    args = [x, w]
    if rope is not None:
        cos, sin = rope
        nblk = cos.shape[0] // tm
        spec = pl.BlockSpec((tm, LANES), lambda g, i, j: (i % nblk, 0))
        in_specs += [spec, spec]
        args += [cos, sin]
    outs = pl.pallas_call(
        functools.partial(_mm_kernel, act=act, rope=rope is not None, out_scale=out_scale),
        grid=(G, M // tm, N // tn),
        in_specs=in_specs,
        out_specs=[pl.BlockSpec((None, tm, tn), lambda g, i, j: (g, i, j)) for _ in out_dtypes],
        out_shape=[jax.ShapeDtypeStruct((G, M, N), dt) for dt in out_dtypes],
        compiler_params=_cparams("parallel", "parallel", "arbitrary"),
        name="matmul",
    )(*args)
    outs = [o[0] if squeeze else o for o in outs]
    return outs[0] if len(outs) == 1 else outs


def _lora_kernel(x_ref, w1_ref, w2_ref, o_ref, *, act):
    h = _dot(x_ref[...].astype(BF16), w1_ref[...])
    if act == "tanh":
        h = jnp.tanh(h)
    elif act == "sigmoid":
        h = jax.nn.sigmoid(h)
    o_ref[...] = _dot(h.astype(BF16), w2_ref[...])


def _lora(x, w1, w2, *, act=None, tm=512):
    x, p0 = _planes(x)
    _, M, K = x.shape
    R = w1.shape[1]
    N = w2.shape[1]
    tm = min(tm, M)
    return pl.pallas_call(
        functools.partial(_lora_kernel, act=act),
        grid=(M // tm,),
        in_specs=[pl.BlockSpec((None, tm, K), lambda i: (p0, i, 0)),
                  pl.BlockSpec((K, R), lambda i: (0, 0)),
                  pl.BlockSpec((R, N), lambda i: (0, 0))],
        out_specs=pl.BlockSpec((tm, N), lambda i: (i, 0)),
        out_shape=jax.ShapeDtypeStruct((M, N), F32),
        compiler_params=_cparams("parallel"),
        name="lora",
    )(x, w1, w2)


def _pad_lora(w1, w2):
    r = w1.shape[1]
    rp = -(-r // LANES) * LANES
    return (jnp.pad(w1, ((0, 0), (0, rp - r))).astype(BF16),
            jnp.pad(w2, ((0, rp - r), (0, 0))).astype(BF16))


def _mlp_kernel(x_ref, w1_ref, w2_ref, xr_ref, gp_ref, *rest, n_out, nf):
    if n_out:
        gn_ref, xo_ref, no_ref = rest
    else:
        xo_ref, = rest
    f = pl.program_id(1)

    @pl.when(f == 0)
    def _():
        xo_ref[...] = jnp.zeros(xo_ref.shape, F32)

    h = jnp.maximum(_dot(x_ref[...], w1_ref[...]), 0.0)
    xo_ref[...] += _dot((h * h).astype(BF16), w2_ref[...])

    @pl.when(f == nf - 1)
    def _():
        x = xr_ref[...] + _rms(xo_ref[...], NORM_EPS) * gp_ref[...]
        xo_ref[...] = x
        if n_out:
            y = _rms(x, NORM_EPS)
            for j in range(n_out):
                no_ref[j] = (y * gn_ref[j:j + 1, :]).astype(no_ref.dtype)


def _mlp(x, w1, w2, x_res, g_post, g_next, *, tm=512, tf=1024):
    x, p0 = _planes(x)
    w1, l1 = _planes(w1)
    w2, l2 = _planes(w2)
    _, M, D = x.shape
    F = w1.shape[2]
    tm = min(tm, M)
    tf = min(tf, F)
    n_out = len(g_next)
    row = pl.BlockSpec((tm, D), lambda i, f: (i, 0))
    vec = pl.BlockSpec((1, D), lambda i, f: (0, 0))
    in_specs = [pl.BlockSpec((None, tm, D), lambda i, f: (p0, i, 0)),
                pl.BlockSpec((None, D, tf), lambda i, f: (l1, 0, f)),
                pl.BlockSpec((None, tf, D), lambda i, f: (l2, f, 0)),
                pl.BlockSpec((tm, D), lambda i, f: (i, 0), pipeline_mode=pl.Buffered(1)),
                vec]
    args = [x, w1, w2, x_res, g_post.reshape(1, D)]
    out_specs = [row]
    out_shape = [jax.ShapeDtypeStruct((M, D), F32)]
    if n_out:
        in_specs.append(pl.BlockSpec((n_out, D), lambda i, f: (0, 0)))
        args.append(jnp.stack(g_next))
        out_specs.append(pl.BlockSpec((n_out, tm, D), lambda i, f: (0, i, 0)))
        out_shape.append(jax.ShapeDtypeStruct((n_out, M, D), BF16))
    res = pl.pallas_call(
        functools.partial(_mlp_kernel, n_out=n_out, nf=F // tf),
        grid=(M // tm, F // tf),
        in_specs=in_specs,
        out_specs=out_specs,
        out_shape=out_shape,
        compiler_params=pltpu.CompilerParams(dimension_semantics=("parallel", "arbitrary"),
                                             vmem_limit_bytes=MLP_VMEM_LIMIT),
        name="mlp",
    )(*args)
    return (res[0], res[1]) if n_out else (res[0], None)


MIX_ORDER = (0, 2, 3, 1, 4, 5)
PL_R, PL_K, PL_V, PL_W, PL_A, PL_G = range(6)


def _prep_kernel(x_ref, sh_ref, g_ref, mu_ref, mix_ref, last_ref, carry_ref):
    i = pl.program_id(1)
    xn = _rms(x_ref[...], NORM_EPS) * g_ref[...]
    tr = xn.shape[0]

    @pl.when(i == 0)
    def _():
        carry_ref[...] = sh_ref[...]

    row = lax.broadcasted_iota(jnp.int32, xn.shape, 0)
    xprev = jnp.where(row == 0, carry_ref[...], pltpu.roll(xn, 1, 0))
    xx = xprev - xn
    for plane, m in enumerate(MIX_ORDER):
        mix_ref[plane] = (xn + xx * mu_ref[m:m + 1, :]).astype(mix_ref.dtype)
    last = xn[tr - 1:tr, :]
    carry_ref[...] = last
    last_ref[...] = last


def _prep(x, shift0, g, mu, *, B, T, out_dtype):
    M, D = x.shape
    tr = min(256, T)
    nt = T // tr
    mix, last = pl.pallas_call(
        _prep_kernel,
        grid=(B, nt),
        in_specs=[pl.BlockSpec((tr, D), lambda b, i: (b * nt + i, 0)),
                  pl.BlockSpec((None, 1, D), lambda b, i: (b, 0, 0)),
                  pl.BlockSpec((1, D), lambda b, i: (0, 0)),
                  pl.BlockSpec((6, D), lambda b, i: (0, 0))],
        out_specs=[pl.BlockSpec((6, tr, D), lambda b, i: (0, b * nt + i, 0)),
                   pl.BlockSpec((None, 1, D), lambda b, i: (b, 0, 0))],
        out_shape=[jax.ShapeDtypeStruct((6, M, D), out_dtype),
                   jax.ShapeDtypeStruct((B, 1, D), F32)],
        scratch_shapes=[pltpu.VMEM((1, D), F32)],
        compiler_params=_cparams("arbitrary", "arbitrary"),
        name="rwkv_prep",
    )(x, shift0.reshape(B, 1, D), g.reshape(1, D), mu)
    return mix, last.reshape(B, D)


N_RKV = 3
MIX_SUB_ROWS = 16


def _rwkv_in_kernel(x_ref, sh_ref, g_ref, mu_ref, w_ref, *rest, nt, n_lora):
    l1_refs = rest[:n_lora]
    rkv_ref = rest[n_lora]
    h_refs = rest[n_lora + 1:2 * n_lora + 1]
    last_ref, mix_ref, carry_ref = rest[2 * n_lora + 1:]
    i = pl.program_id(0)
    g = pl.program_id(1)

    @pl.when(g == 0)
    def _():
        tr = x_ref.shape[0]
        sub = min(MIX_SUB_ROWS, tr)
        row = lax.broadcasted_iota(jnp.int32, (sub, x_ref.shape[1]), 0)

        def chunk(ci, prev_row):
            rows = pl.ds(pl.multiple_of(ci * sub, sub), sub)
            xn = _rms(x_ref[rows, :], NORM_EPS) * g_ref[...]
            xx = jnp.where(row == 0, prev_row, pltpu.roll(xn, 1, 0)) - xn
            for plane, m in enumerate(MIX_ORDER):
                mix_ref[plane, rows, :] = (xn + xx * mu_ref[m:m + 1, :]).astype(mix_ref.dtype)
            return xn[sub - 1:sub, :]

        first = jnp.where(i % nt == 0, sh_ref[...], carry_ref[...])
        last = lax.fori_loop(0, tr // sub, chunk, first)
        carry_ref[...] = last
        last_ref[...] = last

    @pl.when(g < N_RKV)
    def _():
        rkv_ref[...] = _dot(mix_ref[g], w_ref[...])

    @pl.when(g == N_RKV)
    def _():
        for (plane, act), l1_ref, h_ref in zip(LORA_BRANCHES, l1_refs, h_refs):
            h = _dot(mix_ref[plane], l1_ref[...])
            if act == "tanh":
                h = jnp.tanh(h)
            elif act == "sigmoid":
                h = jax.nn.sigmoid(h)
            h_ref[...] = h.astype(h_ref.dtype)


LORA_BRANCHES = ((PL_W, "tanh"), (PL_A, None), (PL_G, "sigmoid"), (PL_V, None))


def _rwkv_in(x, shift0, g, mu, w_rkv, lora1, *, B, T, tm=512):
    M, D = x.shape
    tm = min(tm, T)
    nt = T // tm
    n_lora = len(lora1)
    in_specs = [pl.BlockSpec((tm, D), lambda i, g: (i, 0)),
                pl.BlockSpec((None, 1, D), lambda i, g: (i // nt, 0, 0)),
                pl.BlockSpec((1, D), lambda i, g: (0, 0)),
                pl.BlockSpec((6, D), lambda i, g: (0, 0)),
                pl.BlockSpec((None, D, D), lambda i, g: (jnp.minimum(g, N_RKV - 1), 0, 0))]
    in_specs += [pl.BlockSpec(w.shape, lambda i, g: (0, 0), pipeline_mode=pl.Buffered(1))
                 for w in lora1]
    out_specs = [pl.BlockSpec((None, tm, D), lambda i, g: (jnp.minimum(g, N_RKV - 1), i, 0))]
    out_specs += [pl.BlockSpec((tm, w.shape[1]), lambda i, g: (i, 0)) for w in lora1]
    out_specs.append(pl.BlockSpec((None, 1, D), lambda i, g: (i // nt, 0, 0)))
    out_shape = [jax.ShapeDtypeStruct((N_RKV, M, D), F32)]
    out_shape += [jax.ShapeDtypeStruct((M, w.shape[1]), BF16) for w in lora1]
    out_shape.append(jax.ShapeDtypeStruct((B, 1, D), F32))
    res = pl.pallas_call(
        functools.partial(_rwkv_in_kernel, nt=nt, n_lora=n_lora),
        grid=(M // tm, N_RKV + 1),
        in_specs=in_specs,
        out_specs=out_specs,
        out_shape=out_shape,
        scratch_shapes=[pltpu.VMEM((6, tm, D), BF16), pltpu.VMEM((1, D), F32)],
        compiler_params=pltpu.CompilerParams(dimension_semantics=("arbitrary", "arbitrary"),
                                             vmem_limit_bytes=MLP_VMEM_LIMIT),
        name="rwkv_in",
    )(x, shift0.reshape(B, 1, D), g.reshape(1, D), mu, w_rkv, *lora1)
    return res[0], res[1:1 + n_lora], res[-1].reshape(B, D)


def _resnorm_kernel(x_ref, h_ref, *rest, n_out, project):
    h = h_ref[...]
    if project:
        w_ref, *rest = rest
        h = _dot(h.astype(BF16), w_ref[...])
    gp_ref, *rest = rest
    x = x_ref[...] + _rms(h, NORM_EPS) * gp_ref[...]
    if n_out:
        gn_ref, xo_ref, no_ref = rest
        y = _rms(x, NORM_EPS)
        for j in range(n_out):
            no_ref[j] = (y * gn_ref[j:j + 1, :]).astype(no_ref.dtype)
    else:
        xo_ref, = rest
    xo_ref[...] = x


def _resnorm(x, h, g_post, g_next, w=None):
    M, D = x.shape
    n_out = len(g_next)
    tr = min(512 if w is not None else 256, M)
    row = pl.BlockSpec((tr, D), lambda i: (i, 0))
    vec = pl.BlockSpec((1, D), lambda i: (0, 0))
    in_specs = [row, pl.BlockSpec((tr, h.shape[1]), lambda i: (i, 0))]
    args = [x, h]
    if w is not None:
        in_specs.append(pl.BlockSpec(w.shape, lambda i: (0, 0)))
        args.append(w)
    in_specs.append(vec)
    args.append(g_post.reshape(1, D))
    out_specs = [row]
    out_shape = [jax.ShapeDtypeStruct((M, D), F32)]
    if n_out:
        in_specs.append(pl.BlockSpec((n_out, D), lambda i: (0, 0)))
        args.append(jnp.stack(g_next))
        out_specs.append(pl.BlockSpec((n_out, tr, D), lambda i: (0, i, 0)))
        out_shape.append(jax.ShapeDtypeStruct((n_out, M, D), BF16))
    res = pl.pallas_call(
        functools.partial(_resnorm_kernel, n_out=n_out, project=w is not None),
        grid=(M // tr,),
        in_specs=in_specs,
        out_specs=out_specs,
        out_shape=out_shape,
        compiler_params=_cparams("parallel"),
        name="resnorm",
    )(*args)
    return (res[0], res[1]) if n_out else (res[0], None)


def _norm_kernel(x_ref, g_ref, o_ref):
    o_ref[...] = (_rms(x_ref[...], NORM_EPS) * g_ref[...]).astype(o_ref.dtype)


def _norm(x, g):
    M, D = x.shape
    tr = min(256, M)
    return pl.pallas_call(
        _norm_kernel,
        grid=(M // tr,),
        in_specs=[pl.BlockSpec((tr, D), lambda i: (i, 0)), pl.BlockSpec((1, D), lambda i: (0, 0))],
        out_specs=pl.BlockSpec((tr, D), lambda i: (i, 0)),
        out_shape=jax.ShapeDtypeStruct((M, D), BF16),
        compiler_params=_cparams("parallel"),
        name="norm",
    )(x, g.reshape(1, D))


def _split2(x):
    hi = x.astype(BF16)
    lo = (x - hi.astype(F32)).astype(BF16)
    return hi, lo


def _seg_sum(x, ones_bd):
    hi, lo = _split2(x)
    n = x.shape[0]
    s = _dot(jnp.concatenate([hi, lo], axis=0), ones_bd)
    return s[:n] + s[n:]


def _wkv_kernel(*refs, rows, vres, ng, fused):
    n_rk = 3 + vres
    nb = 3 + vres
    rk_refs = refs[:n_rk]
    br_refs = refs[n_rk:n_rk + (2 * nb if fused else nb)]
    p_ref, s0_ref, y_ref, so_ref, s_ref = refs[n_rk + len(br_refs):]
    c = pl.program_id(2)
    row_head = lax.broadcasted_iota(jnp.int32, (GROUP, GROUP), 0) // RWKV_HEAD
    lane_head = lax.broadcasted_iota(jnp.int32, (GROUP, GROUP), 1) // RWKV_HEAD

    @pl.when(c == 0)
    def _():
        for gi in range(ng):
            s_ref[gi] = jnp.where(row_head == lane_head, s0_ref[gi], 0.0)

    gens = []
    hidden = [ref[...] for ref in br_refs[:nb]] if fused else None
    for gi in range(ng):
        lanes = pl.ds(gi * GROUP, GROUP)
        rk = [ref[:, lanes] for ref in rk_refs]
        if fused:
            br = [_dot(h, w2_ref[:, lanes]) for h, w2_ref in zip(hidden, br_refs[nb:])]
        else:
            br = [ref[:, lanes] for ref in br_refs]
        vals = rk[:3] + br[:3] + ([rk[3], br[3]] if vres else []) + [p_ref[:, lanes], s_ref[gi]]
        gens.append(_wkv_group(*vals, rows=rows, vres=vres))
    outs = [None] * ng
    live = list(range(ng))
    while live:
        for gi in list(live):
            try:
                next(gens[gi])
            except StopIteration as done:
                outs[gi] = done.value
                live.remove(gi)
    for gi, (y, s_new) in enumerate(outs):
        y_ref[:, pl.ds(gi * GROUP, GROUP)] = y.astype(y_ref.dtype)
        s_ref[gi] = s_new

    @pl.when(c == pl.num_programs(2) - 1)
    def _():
        for gi in range(ng):
            s = s_ref[gi]
            packed = s
            for h in range(1, HEADS_PER_GROUP):
                packed = jnp.where(row_head == h, pltpu.roll(s, GROUP - h * RWKV_HEAD, 1), packed)
            so_ref[gi] = packed[:, :RWKV_HEAD]


def _wkv_group(r_ref, k_ref, v_ref, wl_ref, al_ref, g_ref, *rest, rows, vres):
    if vres:
        vf_ref, vl_ref, p, s = rest
    else:
        p, s = rest
    L = CHUNK
    W = GROUP
    HG = HEADS_PER_GROUP

    def load(val):
        x = val.astype(F32)
        if rows < L:
            x = jnp.concatenate([x, jnp.zeros((L - rows, W), F32)], axis=0)
        return x

    w0, a0, k_k, k_a = p[0:1], p[1:2], p[2:3], p[3:4]
    lnx_w, lnx_b, rk, v0 = p[4:5], p[5:6], p[6:7], p[7:8]

    lane_head = lax.broadcasted_iota(jnp.int32, (1, W), 1) // RWKV_HEAD
    rr = lax.broadcasted_iota(jnp.int32, (W, W), 0) // RWKV_HEAD
    cc = lax.broadcasted_iota(jnp.int32, (W, W), 1) // RWKV_HEAD
    bd = rr == cc
    ones_bd = jnp.where(bd, 1.0, 0.0).astype(BF16)
    trow = lax.broadcasted_iota(jnp.int32, (L, HG * L), 0)
    tcol = lax.broadcasted_iota(jnp.int32, (L, HG * L), 1) & (L - 1)
    strict = tcol < trow
    incl = tcol <= trow
    rb = lax.broadcasted_iota(jnp.int32, (HG * L, HG * L), 0) // L
    cb = lax.broadcasted_iota(jnp.int32, (HG * L, HG * L), 1) // L
    bd_l = rb == cb

    def ystack(x):
        return jnp.concatenate(
            [jnp.where(lane_head == h, x, 0.0) for h in range(HG)], axis=0).astype(BF16)

    r = load(r_ref)
    k = load(k_ref)
    v = load(v_ref)
    a_sig = jax.nn.sigmoid(load(al_ref) + a0)
    if vres:
        v = v + (load(vf_ref) - v) * jax.nn.sigmoid(load(vl_ref) + v0)
    z = -(load(wl_ref) + w0)
    softplus = jnp.maximum(z, 0.0) + jnp.log(1.0 + jnp.exp(-jnp.abs(z)))
    lw = -jnp.exp(-softplus - 0.5)
    if rows < L:
        trow1 = lax.broadcasted_iota(jnp.int32, (L, W), 0)
        lw = jnp.where(trow1 < rows, lw, 0.0)
    kkr = k * k_k
    nrm2 = _seg_sum(kkr * kkr, ones_bd)
    yield
    nrm = jnp.sqrt(nrm2)
    kk = kkr / jnp.maximum(nrm, 1e-12)
    k2 = k * (1.0 + (a_sig - 1.0) * k_a)
    a = -kk
    b = kk * a_sig

    l1 = lw.astype(BF16)
    l2r = lw - l1.astype(F32)
    l2 = l2r.astype(BF16)
    l3 = (l2r - l2.astype(F32)).astype(BF16)
    tri = (lax.broadcasted_iota(jnp.int32, (L, L), 1)
           <= lax.broadcasted_iota(jnp.int32, (L, L), 0))
    tri = jnp.where(tri, 1.0, 0.0).astype(BF16)
    cs = _dot(tri, jnp.concatenate([l1, l2, l3], axis=1))
    yield
    cum = cs[:, :W] + cs[:, W:2 * W] + cs[:, 2 * W:]
    cl = cum[L - 1:L, :]
    e_c = jnp.exp(cum)
    e_cm = jnp.exp(cum - lw)
    e_nc = jnp.exp(-cum)
    e_lc = jnp.exp(cl - cum)
    g_l = jnp.exp(cl)
    at = a * e_cm
    rt = r * e_c
    bt = b * e_nc
    kt = k2 * e_nc
    bh = b * e_lc
    kh = k2 * e_lc

    x4 = _dot_nt(jnp.concatenate([at, rt], axis=0).astype(BF16),
                 jnp.concatenate([ystack(bt), ystack(kt)], axis=0))
    yield
    n_ab = jnp.where(strict, x4[:L, :HG * L], 0.0)
    m_ak = jnp.where(strict, x4[:L, HG * L:], 0.0)
    m_rb = jnp.where(incl, x4[L:, :HG * L], 0.0)
    m_rk = jnp.where(incl, x4[L:, HG * L:], 0.0)

    def bdiag(x):
        return jnp.where(bd_l, jnp.concatenate([x] * HG, axis=0), 0.0).astype(BF16)

    t_inv = jnp.where(tcol == trow, 1.0, 0.0) + n_ab
    yv = ystack(v)
    makv = _dot(m_ak.astype(BF16), yv)
    pw = _dot(n_ab.astype(BF16), bdiag(n_ab))
    yield
    steps = int(math.log2(L)) - 1
    for it in range(steps):
        rhs = bdiag(pw)
        if it < steps - 1:
            res = _dot(jnp.concatenate([t_inv, pw], axis=0).astype(BF16), rhs)
            yield
            t_inv = t_inv + res[:L]
            pw = res[L:]
        else:
            res = _dot(t_inv.astype(BF16), rhs)
            yield
            t_inv = t_inv + res

    au = _dot(t_inv.astype(BF16), jnp.concatenate([ystack(at), ystack(makv)], axis=1))
    yield
    abar = au[:, :W]
    u0 = au[:, W:]
    y_abar = ystack(abar)
    d1 = _dot(m_rb.astype(BF16), jnp.concatenate([y_abar, ystack(u0)], axis=1))
    d2 = _dot(m_rk.astype(BF16), yv)
    s_bf = s.astype(BF16)
    sa = _dot_nt(s_bf, y_abar)
    yield
    rbar = rt + d1[:, :W]
    o0 = d1[:, W:] + d2
    o = _dot_nt(rbar.astype(BF16), s_bf) + o0
    uv_t = jnp.concatenate([u0, v], axis=0).T
    q_t = _dot(uv_t.astype(BF16), jnp.concatenate([bh, kh], axis=0).astype(BF16))
    s_new = s * g_l + _dot(sa.astype(BF16), ystack(bh)) + jnp.where(bd, q_t, 0.0)
    bonus = _seg_sum(r * k2 * rk, ones_bd) * v
    yield
    inv_n = 1.0 / RWKV_HEAD
    mean = _seg_sum(o, ones_bd) * inv_n
    yield
    dlt = o - mean
    var = _seg_sum(dlt * dlt, ones_bd) * inv_n
    yield
    yn = dlt * lax.rsqrt(var + GN_EPS) * lnx_w + lnx_b
    out = (yn + bonus) * load(g_ref)
    return out[:rows], s_new


def _wkv(rkv, branches, params, s0, *, B, T, rkv_first=None, out_dtype=BF16):
    _, M, D = rkv.shape
    fused = isinstance(branches[0], tuple)
    vres = rkv_first is not None
    n_groups = B * D // GROUP
    if s0 is None:
        s0_rep = jnp.zeros((n_groups, GROUP, GROUP), F32)
    else:
        s0_rep = jnp.tile(s0.astype(F32).reshape(n_groups, GROUP, RWKV_HEAD), (1, 1, HEADS_PER_GROUP))
    ng = max(d for d in range(1, WKV_GROUPS_PER_STEP + 1) if (D // GROUP) % d == 0)
    G = D // (ng * GROUP)
    rows = min(CHUNK, T)
    nc = T // rows

    def plane(pi):
        return pl.BlockSpec((None, rows, ng * GROUP), lambda b, g, c: (pi, b * nc + c, g))

    tile = pl.BlockSpec((rows, ng * GROUP), lambda b, g, c: (b * nc + c, g))
    state = pl.BlockSpec((ng, GROUP, GROUP), lambda b, g, c: (b * G + g, 0, 0))
    in_specs = [plane(0), plane(1), plane(2)]
    args = [rkv, rkv, rkv]
    if vres:
        in_specs.append(plane(2))
        args.append(rkv_first)
    if fused:
        in_specs += [pl.BlockSpec((rows, h.shape[1]), lambda b, g, c: (b * nc + c, 0))
                     for h, _ in branches]
        in_specs += [pl.BlockSpec((w2.shape[0], ng * GROUP), lambda b, g, c: (0, g))
                     for _, w2 in branches]
        args += [h for h, _ in branches] + [w2 for _, w2 in branches]
    else:
        in_specs += [tile] * len(branches)
        args += list(branches)
    in_specs += [pl.BlockSpec((8, ng * GROUP), lambda b, g, c: (0, g)), state]
    args += [params, s0_rep]
    y, s_out = pl.pallas_call(
        functools.partial(_wkv_kernel, rows=rows, vres=vres, ng=ng, fused=fused),
        grid=(B, G, nc),
        in_specs=in_specs,
        out_specs=[tile, pl.BlockSpec((ng, GROUP, RWKV_HEAD), lambda b, g, c: (b * G + g, 0, 0))],
        out_shape=[jax.ShapeDtypeStruct((M, D), out_dtype),
                   jax.ShapeDtypeStruct((n_groups, GROUP, RWKV_HEAD), F32)],
        scratch_shapes=[pltpu.VMEM((ng, GROUP, GROUP), F32)],
        compiler_params=_cparams("parallel", "parallel", "arbitrary"),
        name="wkv7",
    )(*args)
    return y, s_out.reshape(B, D // RWKV_HEAD, RWKV_HEAD, RWKV_HEAD)


def _lambda(lp, lam_init):
    return (jnp.exp(jnp.sum(lp[0:1] * lp[1:2], keepdims=True))
            - jnp.exp(jnp.sum(lp[2:3] * lp[3:4], keepdims=True)) + lam_init)


Q_SCALE = ATT_HEAD ** -0.5 * math.log2(math.e)


def _rep(x, n):
    return x if n == LANES else jnp.concatenate([x] * (n // LANES), axis=1)


def _softmax_step(s, vt, m, l, acc, ones=None):
    m_new = jnp.maximum(m, jnp.max(s, axis=-1, keepdims=True))
    alpha = jnp.exp2(m - m_new)
    p = jnp.exp2(s - _rep(m_new, s.shape[1]))
    pb = p.astype(BF16)
    if ones is None:
        l_new = alpha * l + jnp.sum(p, axis=-1, keepdims=True)
    else:
        l_new = alpha * l + _dot(pb, ones)
    acc_new = _rep(alpha, acc.shape[1]) * acc + _dot(pb, vt)
    return m_new, l_new, acc_new


def _attn_prompt_kernel(q_ref, k_ref, v_ref, lp_ref, sg_ref, o_ref, m_ref, l_ref, acc_ref,
                        *, tq, lam_init):
    qi = pl.program_id(2)
    q = q_ref[...]
    qs = (q[:, :ATT_HEAD], q[:, ATT_HEAD:])
    m_ref[...] = jnp.full(m_ref.shape, NEG_BIG, F32)
    l_ref[...] = jnp.zeros(l_ref.shape, F32)
    acc_ref[...] = jnp.zeros(acc_ref.shape, F32)

    def scores(j):
        kt = k_ref[pl.ds(pl.multiple_of(j * tq, tq), tq), :]
        return tuple(_dot_nt(qs[mp], kt[:, mp * ATT_HEAD:(mp + 1) * ATT_HEAD]) for mp in range(2))

    def update(j, s):
        vt = v_ref[pl.ds(pl.multiple_of(j * tq, tq), tq), :]
        for mp in range(2):
            m_ref[mp], l_ref[mp], acc_ref[mp] = _softmax_step(
                s[mp], vt, m_ref[mp], l_ref[mp], acc_ref[mp])

    def body(j, s):
        s_next = scores(j + 1)
        update(j, s)
        return s_next

    s = lax.fori_loop(0, qi, body, scores(0))
    keep = (lax.broadcasted_iota(jnp.int32, (tq, tq), 1)
            <= lax.broadcasted_iota(jnp.int32, (tq, tq), 0))
    update(qi, tuple(jnp.where(keep, x, NEG_BIG) for x in s))
    lam = _lambda(lp_ref[...], lam_init)
    hw = acc_ref.shape[2]
    o = acc_ref[0] / _rep(l_ref[0], hw) - lam * (acc_ref[1] / _rep(l_ref[1], hw))
    o = _rms(o, SUBLN_EPS) * sg_ref[...] * (1.0 - lam_init)
    o_ref[...] = o.astype(o_ref.dtype)


def _attn_prompt(q, k, v, lp, sg, *, B, T, lam_init, tq=512):
    M, Wd = q.shape
    HW = 2 * ATT_HEAD
    H = Wd // HW
    tq = min(tq, T)
    nq = T // tq
    return pl.pallas_call(
        functools.partial(_attn_prompt_kernel, tq=tq, lam_init=lam_init),
        grid=(B, H, nq),
        in_specs=[pl.BlockSpec((tq, HW), lambda b, h, i: (b * nq + i, h)),
                  pl.BlockSpec((T, HW), lambda b, h, i: (b, h)),
                  pl.BlockSpec((T, HW), lambda b, h, i: (b, h)),
                  pl.BlockSpec((4, ATT_HEAD), lambda b, h, i: (0, 0)),
                  pl.BlockSpec((1, HW), lambda b, h, i: (0, 0))],
        out_specs=pl.BlockSpec((tq, HW), lambda b, h, i: (b * nq + i, h)),
        out_shape=jax.ShapeDtypeStruct((M, Wd), BF16),
        scratch_shapes=[pltpu.VMEM((2, tq, LANES), F32), pltpu.VMEM((2, tq, LANES), F32),
                        pltpu.VMEM((2, tq, HW), F32)],
        compiler_params=_cparams("parallel", "parallel", "arbitrary"),
        name="diff_attn_prompt",
    )(q, k, v, lp, sg.reshape(1, HW))


PAGES_PER_STEP = 4


def _attn_sample_kernel(pt_ref, q_ref, *rest, heads, tq, npp, new_rows, lam_init):
    k_refs = rest[:npp]
    v_refs = rest[npp:2 * npp]
    kn_ref, vn_ref, lp_ref, sg_ref, o_ref, m_ref, l_ref, acc_ref = rest[2 * npp:]
    p = pl.program_id(1)
    npg = pl.num_programs(1)
    HW = 2 * ATT_HEAD
    R = heads * tq

    @pl.when(p == 0)
    def _():
        m_ref[...] = jnp.full(m_ref.shape, NEG_BIG, F32)
        l_ref[...] = jnp.zeros(l_ref.shape, F32)
        acc_ref[...] = jnp.zeros(acc_ref.shape, F32)

    q = q_ref[...]
    qm = [jnp.concatenate([q[:, h * HW + mp * ATT_HEAD:h * HW + (mp + 1) * ATT_HEAD]
                           for h in range(heads)], axis=0).astype(BF16) for mp in range(2)]

    def scores(k0, k1, valid):
        g = jnp.concatenate([_dot_nt(qm[0], k0.astype(BF16)), _dot_nt(qm[1], k1.astype(BF16))],
                            axis=0)
        return jnp.where(valid, g, NEG_BIG)

    def head_mask(ncol):
        row = lax.broadcasted_iota(jnp.int32, (2 * R, ncol), 0)
        col = lax.broadcasted_iota(jnp.int32, (2 * R, ncol), 1)
        return (col % heads) == ((row // tq) % heads), row, col

    state = (m_ref[...], l_ref[...], acc_ref[...])
    nk = k_refs[0].shape[0] // 2
    same_head, _, _ = head_mask(nk)
    for j in range(npp):
        s = scores(k_refs[j][pl.ds(0, nk, stride=2), :], k_refs[j][pl.ds(1, nk, stride=2), :],
                   same_head)
        state = _softmax_step(s, v_refs[j][...].astype(BF16), *state)
    m_ref[...], l_ref[...], acc_ref[...] = state

    @pl.when(p == npg - 1)
    def _():
        pad_to = max(new_rows, LANES)
        k0 = kn_ref[pl.ds(0, new_rows, stride=2), :]
        k1 = kn_ref[pl.ds(1, new_rows, stride=2), :]
        vn = vn_ref[...]
        if pad_to > new_rows:
            zk = jnp.zeros((pad_to - new_rows, ATT_HEAD), F32)
            k0 = jnp.concatenate([k0, zk], axis=0)
            k1 = jnp.concatenate([k1, zk], axis=0)
            vn = jnp.concatenate([vn, jnp.zeros((pad_to - new_rows, HW), F32)], axis=0)
        same, row, col = head_mask(pad_to)
        valid = same & ((col // heads) <= (row % tq)) & (col < new_rows)
        m, l, acc = _softmax_step(scores(k0, k1, valid), vn.astype(BF16), *state)
        lam = _lambda(lp_ref[...], lam_init)
        sg = sg_ref[...]
        for h in range(heads):
            r0 = slice(h * tq, (h + 1) * tq)
            r1 = slice(R + h * tq, R + (h + 1) * tq)
            o = acc[r0] / _rep(l[r0], HW) - lam * (acc[r1] / _rep(l[r1], HW))
            o = _rms(o, SUBLN_EPS) * sg * (1.0 - lam_init)
            o_ref[:, h * HW:(h + 1) * HW] = o


def _attn_sample(q, cache_k, cache_v, k_new, v_new, page_table, lp, sg, *, lam_init):
    Bd, npg = page_table.shape
    M, Wd = q.shape
    tq = M // Bd
    HW = 2 * ATT_HEAD
    heads = Wd // HW
    npp = max(d for d in range(1, PAGES_PER_STEP + 1) if npg % d == 0)
    nsteps = npg // npp
    krows, vrows = cache_k.shape[1], cache_v.shape[1]
    new_rows = tq * heads
    rows = pl.BlockSpec((tq, Wd), lambda b, p, pt: (b, 0))

    def page_spec(nrows, width, j):
        return pl.BlockSpec((None, nrows, width), lambda b, p, pt: (pt[b * npg + p * npp + j], 0, 0))

    grid_spec = pltpu.PrefetchScalarGridSpec(
        num_scalar_prefetch=1,
        grid=(Bd, nsteps),
        in_specs=([rows] + [page_spec(krows, ATT_HEAD, j) for j in range(npp)]
                  + [page_spec(vrows, HW, j) for j in range(npp)]
                  + [pl.BlockSpec((None, 2 * new_rows, ATT_HEAD), lambda b, p, pt: (b, 0, 0)),
                     pl.BlockSpec((None, new_rows, HW), lambda b, p, pt: (b, 0, 0)),
                     pl.BlockSpec((4, ATT_HEAD), lambda b, p, pt: (0, 0)),
                     pl.BlockSpec((1, HW), lambda b, p, pt: (0, 0))]),
        out_specs=rows,
        scratch_shapes=[pltpu.VMEM((2 * new_rows, LANES), F32),
                        pltpu.VMEM((2 * new_rows, LANES), F32),
                        pltpu.VMEM((2 * new_rows, HW), F32)],
    )
    return pl.pallas_call(
        functools.partial(_attn_sample_kernel, heads=heads, tq=tq, npp=npp, new_rows=new_rows,
                          lam_init=lam_init),
        grid_spec=grid_spec,
        out_shape=jax.ShapeDtypeStruct((M, Wd), F32),
        compiler_params=_cparams("parallel", "arbitrary"),
        name="diff_attn_sample",
    )(page_table.reshape(-1), q, *([cache_k] * npp), *([cache_v] * npp),
      k_new.reshape(Bd, 2 * new_rows, ATT_HEAD), v_new.reshape(Bd, new_rows, HW),
      lp, sg.reshape(1, HW))


def _rope_tables(pos, reps):
    half = ATT_HEAD // 2
    inv = jnp.power(ROPE_THETA, -jnp.arange(half, dtype=F32) / half)
    ang = pos.astype(F32)[:, None] * inv[None, :]
    cos = jnp.concatenate([jnp.cos(ang), jnp.cos(ang)], axis=-1)
    sin = jnp.concatenate([-jnp.sin(ang), jnp.sin(ang)], axis=-1)
    return jnp.tile(cos, (reps, 1)), jnp.tile(sin, (reps, 1))


def _lambda_init(layer_idx):
    return 0.8 - 0.6 * math.exp(-0.3 * layer_idx)


def kernel(x_prompt, x_sample, cache_k, cache_v, state_shift, state_wkv, page_table, norm_mix, norm_ffn, rwkv_mu, rwkv_vec, rwkv_wr, rwkv_wk, rwkv_wv, rwkv_wo, rwkv_w1, rwkv_w2, rwkv_a1, rwkv_a2, rwkv_v0, rwkv_v1, rwkv_v2, rwkv_g1, rwkv_g2, rwkv_rk, kv_norm, kv_wk, kv_wv, attn_wq, attn_wo, attn_lambda, attn_subln, ffn_w1, ffn_w2):
    depth = norm_mix.shape[0]
    n_a = rwkv_mu.shape[0]
    D = x_prompt.shape[-1]
    Wd = kv_wk.shape[1]
    n_att_heads = Wd // (2 * ATT_HEAD)
    n_rwkv_heads = D // RWKV_HEAD

    w_rkv = [jnp.stack([rwkv_wr[l], rwkv_wk[l], rwkv_wv[l]]).astype(BF16) for l in range(n_a)]
    w_o = [rwkv_wo[l].astype(BF16) for l in range(n_a)]
    lora_w = [_pad_lora(rwkv_w1[l], rwkv_w2[l]) for l in range(n_a)]
    lora_a = [_pad_lora(rwkv_a1[l], rwkv_a2[l]) for l in range(n_a)]
    lora_g = [_pad_lora(rwkv_g1[l], rwkv_g2[l]) for l in range(n_a)]
    lora_v = [_pad_lora(rwkv_v1[l], rwkv_v2[l]) for l in range(n_a - 1)]
    wkv_params = []
    for l in range(n_a):
        v0 = rwkv_v0[l - 1] if l > 0 else jnp.zeros((D,), F32)
        wkv_params.append(jnp.concatenate(
            [rwkv_vec[l], rwkv_rk[l].reshape(1, D), v0.reshape(1, D)], axis=0))
    w_kv = jnp.stack([kv_wk, kv_wv]).astype(BF16)
    w_q = [attn_wq[j].astype(BF16) for j in range(depth - n_a)]
    w_ao = [attn_wo[j].astype(BF16) for j in range(depth - n_a)]
    w_f1 = ffn_w1.astype(BF16)
    w_f2 = ffn_w2.astype(BF16)

    def trunk(x, B, T, shift0, wkv0, rope, attend, small):
        act_dtype = F32 if small else BF16
        new_shift, new_wkv = [], []
        rkv_first = None
        k_sh = v_sh = k_att = v_att = None
        xn_next = None
        for l in range(depth):
            if l < n_a:
                loras = [lora_w[l], lora_a[l], lora_g[l]] + ([lora_v[l - 1]] if l > 0 else [])
                if small:
                    mix, last = _prep(x, shift0[l], norm_mix[l, 0], rwkv_mu[l], B=B, T=T,
                                      out_dtype=act_dtype)
                    rkv = _matmul((mix, PL_R), w_rkv[l])
                    branches = [_lora((mix, plane), w1, w2, act=act)
                                for (plane, act), (w1, w2) in zip(LORA_BRANCHES, loras)]
                else:
                    rkv, hidden, last = _rwkv_in(x, shift0[l], norm_mix[l, 0], rwkv_mu[l], w_rkv[l],
                                                 [w1 for w1, _ in loras], B=B, T=T)
                    branches = [(h, w2) for h, (_, w2) in zip(hidden, loras)]
                new_shift.append(last)
                if l == 0:
                    rkv_first = rkv
                branch, s_new = _wkv(rkv, branches, wkv_params[l],
                                     None if wkv0 is None else wkv0[l], B=B, T=T,
                                     rkv_first=rkv_first if l > 0 else None, out_dtype=act_dtype)
                new_wkv.append(s_new.astype(state_wkv.dtype))
                w_out = w_o[l]
            else:
                j = l - n_a
                lam_init = _lambda_init(l)
                q = _matmul(xn_next, w_q[j], out_dtypes=(act_dtype,), rope=rope, out_scale=Q_SCALE)
                branch = attend(q, k_att, v_att, attn_lambda[j], attn_subln[j], lam_init)
                w_out = w_ao[j]
            x, nrm = _resnorm(x, branch, norm_mix[l, 1], [norm_ffn[l, 0]], w=w_out)
            g_next = []
            if l == n_a - 1:
                g_next.append(kv_norm)
            if n_a - 1 <= l < depth - 1:
                g_next.append(norm_mix[l + 1, 0])
            x, nrm = _mlp((nrm, 0), (w_f1, l), (w_f2, l), x, norm_ffn[l, 1], g_next)
            if l == n_a - 1:
                if small:
                    k_sh = k_att = _matmul((nrm, 0), w_kv[0], rope=rope)
                    v_sh = v_att = _matmul((nrm, 0), w_kv[1])
                else:
                    k_sh, k_att = _matmul((nrm, 0), w_kv[0], rope=rope, out_dtypes=(F32, BF16))
                    v_sh, v_att = _matmul((nrm, 0), w_kv[1], out_dtypes=(F32, BF16))
            if g_next:
                xn_next = (nrm, len(g_next) - 1)
        return x, k_sh, v_sh, jnp.stack(new_shift), jnp.stack(new_wkv)

    Bp, Tp, _ = x_prompt.shape
    shift0_p = jnp.zeros((n_a, Bp, D), F32)
    wkv0_p = None
    rope_p = _rope_tables(jnp.arange(Tp, dtype=jnp.int32), 1)

    def attend_p(q, k, v, lp, sg, lam_init):
        return _attn_prompt(q, k, v, lp, sg, B=Bp, T=Tp, lam_init=lam_init)

    y_p, k_p, v_p, shift_p, wkv_p = trunk(
        x_prompt.reshape(Bp * Tp, D), Bp, Tp, shift0_p, wkv0_p, rope_p, attend_p, False)

    Bd, Td, _ = x_sample.shape
    n_pages = page_table.shape[1]
    page = cache_k.shape[1]
    past_len = n_pages * page
    rope_s = _rope_tables(past_len + jnp.arange(Td, dtype=jnp.int32), Bd)
    ck = cache_k.reshape(cache_k.shape[0], page * n_att_heads * 2, ATT_HEAD)
    cv = cache_v.reshape(cache_v.shape[0], page * n_att_heads, 2 * ATT_HEAD)

    def attend_s(q, k, v, lp, sg, lam_init):
        return _attn_sample(q, ck, cv, k, v, page_table, lp, sg, lam_init=lam_init)

    y_s, k_s, v_s, shift_s, wkv_s = trunk(
        x_sample.reshape(Bd * Td, D), Bd, Td, state_shift, state_wkv, rope_s, attend_s, True)

    H = n_att_heads
    return (y_p.reshape(Bp, Tp, D), y_s.reshape(Bd, Td, D),
            k_p.reshape(Bp, Tp, H, 2, ATT_HEAD), v_p.reshape(Bp, Tp, H, 2 * ATT_HEAD),
            shift_p, wkv_p,
            k_s.reshape(Bd, Td, H, 2, ATT_HEAD), v_s.reshape(Bd, Td, H, 2 * ATT_HEAD),
            shift_s, wkv_s)
```

```python
import functools
import math

import jax
import jax.numpy as jnp
from jax import lax
from jax.experimental import pallas as pl
from jax.experimental.pallas import tpu as pltpu

F32 = jnp.float32
BF16 = jnp.bfloat16

RWKV_HEAD = 64
ATT_HEAD = 128
NORM_EPS = 1e-6
GN_EPS = 64e-5
SUBLN_EPS = 1e-5
ROPE_THETA = 10000.0

LANES = 128
GROUP = 256
HEADS_PER_GROUP = GROUP // RWKV_HEAD
CHUNK = 64
WKV_GROUPS_PER_STEP = 8
VMEM_LIMIT = 48 * 1024 * 1024
MLP_VMEM_LIMIT = 56 * 1024 * 1024
RIDE_VMEM_LIMIT = 60 * 1024 * 1024
NEG_BIG = -1e30


def _cparams(*sem):
    return pltpu.CompilerParams(dimension_semantics=sem, vmem_limit_bytes=VMEM_LIMIT)


def _dot(a, b):
    return jnp.dot(a, b, preferred_element_type=F32)


def _dot_nt(a, b):
    return lax.dot_general(a, b, (((1,), (1,)), ((), ())), preferred_element_type=F32)


def _rms(x, eps):
    return x * lax.rsqrt(jnp.mean(x * x, axis=-1, keepdims=True) + eps)


def _mm_kernel(x_ref, w_ref, *rest, act, rope, out_scale):
    acc = _dot(x_ref[...].astype(BF16), w_ref[...])
    if act == "tanh":
        acc = jnp.tanh(acc)
    elif act == "sigmoid":
        acc = jax.nn.sigmoid(acc)
    o_refs = rest[2:] if rope else rest
    if rope:
        cos = rest[0][...] * out_scale
        sin = rest[1][...] * out_scale
        for c in range(acc.shape[1] // LANES):
            blk = acc[:, c * LANES:(c + 1) * LANES]
            res = blk * cos + pltpu.roll(blk, LANES // 2, 1) * sin
            for o_ref in o_refs:
                o_ref[:, c * LANES:(c + 1) * LANES] = res.astype(o_ref.dtype)
    else:
        if out_scale != 1.0:
            acc = acc * out_scale
        for o_ref in o_refs:
            o_ref[...] = acc.astype(o_ref.dtype)


def _planes(x):
    return x if isinstance(x, tuple) else (x[None], 0)


MAX_WEIGHT_TILE_BYTES = 8 * 1024 * 1024


def _matmul(x, w, *, out_dtypes=(F32,), act=None, rope=None, out_scale=1.0, tm=512):
    x, p0 = _planes(x)
    squeeze = w.ndim == 2
    if squeeze:
        w = w[None]
    _, M, K = x.shape
    G, _, N = w.shape
    tm = min(tm, M)
    if rope is not None:
        tm = min(tm, rope[0].shape[0])
        assert rope[0].shape[0] % tm == 0
    tn = N
    while K * tn * w.dtype.itemsize > MAX_WEIGHT_TILE_BYTES and tn % (2 * LANES) == 0:
        tn //= 2
    assert M % tm == 0 and N % tn == 0
    in_specs = [pl.BlockSpec((None, tm, K), lambda g, i, j: (p0 + g, i, 0)),
                pl.BlockSpec((None, K, tn), lambda g, i, j: (g, 0, j))]
    args = [x, w]
    if rope is not None:
        cos, sin = rope
        nblk = cos.shape[0] // tm
        spec = pl.BlockSpec((tm, LANES), lambda g, i, j: (i % nblk, 0))
        in_specs += [spec, spec]
        args += [cos, sin]
    outs = pl.pallas_call(
        functools.partial(_mm_kernel, act=act, rope=rope is not None, out_scale=out_scale),
        grid=(G, M // tm, N // tn),
        in_specs=in_specs,
        out_specs=[pl.BlockSpec((None, tm, tn), lambda g, i, j: (g, i, j)) for _ in out_dtypes],
        out_shape=[jax.ShapeDtypeStruct((G, M, N), dt) for dt in out_dtypes],
        compiler_params=_cparams("parallel", "parallel", "arbitrary"),
        name="matmul",
    )(*args)
    outs = [o[0] if squeeze else o for o in outs]
    return outs[0] if len(outs) == 1 else outs


def _lora_kernel(x_ref, w1_ref, w2_ref, o_ref, *, act):
    h = _dot(x_ref[...].astype(BF16), w1_ref[...])
    if act == "tanh":
        h = jnp.tanh(h)
    elif act == "sigmoid":
        h = jax.nn.sigmoid(h)
    o_ref[...] = _dot(h.astype(BF16), w2_ref[...])


def _lora(x, w1, w2, *, act=None, tm=512):
    x, p0 = _planes(x)
    _, M, K = x.shape
    R = w1.shape[1]
    N = w2.shape[1]
    tm = min(tm, M)
    return pl.pallas_call(
        functools.partial(_lora_kernel, act=act),
        grid=(M // tm,),
        in_specs=[pl.BlockSpec((None, tm, K), lambda i: (p0, i, 0)),
                  pl.BlockSpec((K, R), lambda i: (0, 0)),
                  pl.BlockSpec((R, N), lambda i: (0, 0))],
        out_specs=pl.BlockSpec((tm, N), lambda i: (i, 0)),
        out_shape=jax.ShapeDtypeStruct((M, N), F32),
        compiler_params=_cparams("parallel"),
        name="lora",
    )(x, w1, w2)


def _pad_lora(w1, w2):
    r = w1.shape[1]
    rp = -(-r // LANES) * LANES
    return (jnp.pad(w1, ((0, 0), (0, rp - r))).astype(BF16),
            jnp.pad(w2, ((0, rp - r), (0, 0))).astype(BF16))


def _mlp_kernel(x_ref, w1_ref, w2_ref, xr_ref, gp_ref, *rest, n_out, nf):
    if n_out:
        gn_ref, xo_ref, no_ref = rest
    else:
        xo_ref, = rest
    f = pl.program_id(1)

    @pl.when(f == 0)
    def _():
        xo_ref[...] = jnp.zeros(xo_ref.shape, F32)

    h = jnp.maximum(_dot(x_ref[...], w1_ref[...]), 0.0)
    xo_ref[...] += _dot((h * h).astype(BF16), w2_ref[...])

    @pl.when(f == nf - 1)
    def _():
        x = xr_ref[...] + _rms(xo_ref[...], NORM_EPS) * gp_ref[...]
        xo_ref[...] = x
        if n_out:
            y = _rms(x, NORM_EPS)
            for j in range(n_out):
                no_ref[j] = (y * gn_ref[j:j + 1, :]).astype(no_ref.dtype)


def _mlp(x, w1, w2, x_res, g_post, g_next, *, tm=512, tf=1024):
    x, p0 = _planes(x)
    w1, l1 = _planes(w1)
    w2, l2 = _planes(w2)
    _, M, D = x.shape
    F = w1.shape[2]
    tm = min(tm, M)
    tf = min(tf, F)
    n_out = len(g_next)
    row = pl.BlockSpec((tm, D), lambda i, f: (i, 0))
    vec = pl.BlockSpec((1, D), lambda i, f: (0, 0))
    in_specs = [pl.BlockSpec((None, tm, D), lambda i, f: (p0, i, 0)),
                pl.BlockSpec((None, D, tf), lambda i, f: (l1, 0, f)),
                pl.BlockSpec((None, tf, D), lambda i, f: (l2, f, 0)),
                pl.BlockSpec((tm, D), lambda i, f: (i, 0), pipeline_mode=pl.Buffered(1)),
                vec]
    args = [x, w1, w2, x_res, g_post.reshape(1, D)]
    out_specs = [row]
    out_shape = [jax.ShapeDtypeStruct((M, D), F32)]
    if n_out:
        in_specs.append(pl.BlockSpec((n_out, D), lambda i, f: (0, 0)))
        args.append(jnp.stack(g_next))
        out_specs.append(pl.BlockSpec((n_out, tm, D), lambda i, f: (0, i, 0)))
        out_shape.append(jax.ShapeDtypeStruct((n_out, M, D), BF16))
    res = pl.pallas_call(
        functools.partial(_mlp_kernel, n_out=n_out, nf=F // tf),
        grid=(M // tm, F // tf),
        in_specs=in_specs,
        out_specs=out_specs,
        out_shape=out_shape,
        compiler_params=pltpu.CompilerParams(dimension_semantics=("parallel", "arbitrary"),
                                             vmem_limit_bytes=MLP_VMEM_LIMIT),
        name="mlp",
    )(*args)
    return (res[0], res[1]) if n_out else (res[0], None)


MIX_ORDER = (0, 2, 3, 1, 4, 5)
PL_R, PL_K, PL_V, PL_W, PL_A, PL_G = range(6)


def _prep_kernel(x_ref, sh_ref, g_ref, mu_ref, mix_ref, last_ref, carry_ref):
    i = pl.program_id(1)
    xn = _rms(x_ref[...], NORM_EPS) * g_ref[...]
    tr = xn.shape[0]

    @pl.when(i == 0)
    def _():
        carry_ref[...] = sh_ref[...]

    row = lax.broadcasted_iota(jnp.int32, xn.shape, 0)
    xprev = jnp.where(row == 0, carry_ref[...], pltpu.roll(xn, 1, 0))
    xx = xprev - xn
    for plane, m in enumerate(MIX_ORDER):
        mix_ref[plane] = (xn + xx * mu_ref[m:m + 1, :]).astype(mix_ref.dtype)
    last = xn[tr - 1:tr, :]
    carry_ref[...] = last
    last_ref[...] = last


def _prep(x, shift0, g, mu, *, B, T, out_dtype):
    M, D = x.shape
    tr = min(256, T)
    nt = T // tr
    mix, last = pl.pallas_call(
        _prep_kernel,
        grid=(B, nt),
        in_specs=[pl.BlockSpec((tr, D), lambda b, i: (b * nt + i, 0)),
                  pl.BlockSpec((None, 1, D), lambda b, i: (b, 0, 0)),
                  pl.BlockSpec((1, D), lambda b, i: (0, 0)),
                  pl.BlockSpec((6, D), lambda b, i: (0, 0))],
        out_specs=[pl.BlockSpec((6, tr, D), lambda b, i: (0, b * nt + i, 0)),
                   pl.BlockSpec((None, 1, D), lambda b, i: (b, 0, 0))],
        out_shape=[jax.ShapeDtypeStruct((6, M, D), out_dtype),
                   jax.ShapeDtypeStruct((B, 1, D), F32)],
        scratch_shapes=[pltpu.VMEM((1, D), F32)],
        compiler_params=_cparams("arbitrary", "arbitrary"),
        name="rwkv_prep",
    )(x, shift0.reshape(B, 1, D), g.reshape(1, D), mu)
    return mix, last.reshape(B, D)


N_RKV = 3
MIX_SUB_ROWS = 16


def _rwkv_in_kernel(x_ref, sh_ref, g_ref, mu_ref, w_ref, *rest, nt, n_lora):
    l1_refs = rest[:n_lora]
    rkv_ref = rest[n_lora]
    h_refs = rest[n_lora + 1:2 * n_lora + 1]
    last_ref, mix_ref, carry_ref = rest[2 * n_lora + 1:]
    i = pl.program_id(0)
    g = pl.program_id(1)

    @pl.when(g == 0)
    def _():
        tr = x_ref.shape[0]
        sub = min(MIX_SUB_ROWS, tr)
        row = lax.broadcasted_iota(jnp.int32, (sub, x_ref.shape[1]), 0)

        def chunk(ci, prev_row):
            rows = pl.ds(pl.multiple_of(ci * sub, sub), sub)
            xn = _rms(x_ref[rows, :], NORM_EPS) * g_ref[...]
            xx = jnp.where(row == 0, prev_row, pltpu.roll(xn, 1, 0)) - xn
            for plane, m in enumerate(MIX_ORDER):
                mix_ref[plane, rows, :] = (xn + xx * mu_ref[m:m + 1, :]).astype(mix_ref.dtype)
            return xn[sub - 1:sub, :]

        first = jnp.where(i % nt == 0, sh_ref[...], carry_ref[...])
        last = lax.fori_loop(0, tr // sub, chunk, first)
        carry_ref[...] = last
        last_ref[...] = last

    @pl.when(g < N_RKV)
    def _():
        rkv_ref[...] = _dot(mix_ref[g], w_ref[...])

    @pl.when(g == N_RKV)
    def _():
        for (plane, act), l1_ref, h_ref in zip(LORA_BRANCHES, l1_refs, h_refs):
            h = _dot(mix_ref[plane], l1_ref[...])
            if act == "tanh":
                h = jnp.tanh(h)
            elif act == "sigmoid":
                h = jax.nn.sigmoid(h)
            h_ref[...] = h.astype(h_ref.dtype)


LORA_BRANCHES = ((PL_W, "tanh"), (PL_A, None), (PL_G, "sigmoid"), (PL_V, None))


def _rwkv_in(x, shift0, g, mu, w_rkv, lora1, *, B, T, tm=512):
    M, D = x.shape
    tm = min(tm, T)
    nt = T // tm
    n_lora = len(lora1)
    in_specs = [pl.BlockSpec((tm, D), lambda i, g: (i, 0)),
                pl.BlockSpec((None, 1, D), lambda i, g: (i // nt, 0, 0)),
                pl.BlockSpec((1, D), lambda i, g: (0, 0)),
                pl.BlockSpec((6, D), lambda i, g: (0, 0)),
                pl.BlockSpec((None, D, D), lambda i, g: (jnp.minimum(g, N_RKV - 1), 0, 0))]
    in_specs += [pl.BlockSpec(w.shape, lambda i, g: (0, 0), pipeline_mode=pl.Buffered(1))
                 for w in lora1]
    out_specs = [pl.BlockSpec((None, tm, D), lambda i, g: (jnp.minimum(g, N_RKV - 1), i, 0))]
    out_specs += [pl.BlockSpec((tm, w.shape[1]), lambda i, g: (i, 0)) for w in lora1]
    out_specs.append(pl.BlockSpec((None, 1, D), lambda i, g: (i // nt, 0, 0)))
    out_shape = [jax.ShapeDtypeStruct((N_RKV, M, D), F32)]
    out_shape += [jax.ShapeDtypeStruct((M, w.shape[1]), BF16) for w in lora1]
    out_shape.append(jax.ShapeDtypeStruct((B, 1, D), F32))
    res = pl.pallas_call(
        functools.partial(_rwkv_in_kernel, nt=nt, n_lora=n_lora),
        grid=(M // tm, N_RKV + 1),
        in_specs=in_specs,
        out_specs=out_specs,
        out_shape=out_shape,
        scratch_shapes=[pltpu.VMEM((6, tm, D), BF16), pltpu.VMEM((1, D), F32)],
        compiler_params=pltpu.CompilerParams(dimension_semantics=("arbitrary", "arbitrary"),
                                             vmem_limit_bytes=MLP_VMEM_LIMIT),
        name="rwkv_in",
    )(x, shift0.reshape(B, 1, D), g.reshape(1, D), mu, w_rkv, *lora1)
    return res[0], res[1:1 + n_lora], res[-1].reshape(B, D)


def _resnorm_kernel(x_ref, h_ref, *rest, n_out, project):
    h = h_ref[...]
    if project:
        w_ref, *rest = rest
        h = _dot(h.astype(BF16), w_ref[...])
    gp_ref, *rest = rest
    x = x_ref[...] + _rms(h, NORM_EPS) * gp_ref[...]
    if n_out:
        gn_ref, xo_ref, no_ref = rest
        y = _rms(x, NORM_EPS)
        for j in range(n_out):
            no_ref[j] = (y * gn_ref[j:j + 1, :]).astype(no_ref.dtype)
    else:
        xo_ref, = rest
    xo_ref[...] = x


def _resnorm(x, h, g_post, g_next, w=None):
    M, D = x.shape
    n_out = len(g_next)
    tr = min(512 if w is not None else 256, M)
    row = pl.BlockSpec((tr, D), lambda i: (i, 0))
    vec = pl.BlockSpec((1, D), lambda i: (0, 0))
    in_specs = [row, pl.BlockSpec((tr, h.shape[1]), lambda i: (i, 0))]
    args = [x, h]
    if w is not None:
        in_specs.append(pl.BlockSpec(w.shape, lambda i: (0, 0)))
        args.append(w)
    in_specs.append(vec)
    args.append(g_post.reshape(1, D))
    out_specs = [row]
    out_shape = [jax.ShapeDtypeStruct((M, D), F32)]
    if n_out:
        in_specs.append(pl.BlockSpec((n_out, D), lambda i: (0, 0)))
        args.append(jnp.stack(g_next))
        out_specs.append(pl.BlockSpec((n_out, tr, D), lambda i: (0, i, 0)))
        out_shape.append(jax.ShapeDtypeStruct((n_out, M, D), BF16))
    res = pl.pallas_call(
        functools.partial(_resnorm_kernel, n_out=n_out, project=w is not None),
        grid=(M // tr,),
        in_specs=in_specs,
        out_specs=out_specs,
        out_shape=out_shape,
        compiler_params=_cparams("parallel"),
        name="resnorm",
    )(*args)
    return (res[0], res[1]) if n_out else (res[0], None)


def _norm_kernel(x_ref, g_ref, o_ref):
    o_ref[...] = (_rms(x_ref[...], NORM_EPS) * g_ref[...]).astype(o_ref.dtype)


def _norm(x, g):
    M, D = x.shape
    tr = min(256, M)
    return pl.pallas_call(
        _norm_kernel,
        grid=(M // tr,),
        in_specs=[pl.BlockSpec((tr, D), lambda i: (i, 0)), pl.BlockSpec((1, D), lambda i: (0, 0))],
        out_specs=pl.BlockSpec((tr, D), lambda i: (i, 0)),
        out_shape=jax.ShapeDtypeStruct((M, D), BF16),
        compiler_params=_cparams("parallel"),
        name="norm",
    )(x, g.reshape(1, D))


def _split2(x):
    hi = x.astype(BF16)
    lo = (x - hi.astype(F32)).astype(BF16)
    return hi, lo


def _seg_sum(x, ones_bd):
    hi, lo = _split2(x)
    n = x.shape[0]
    s = _dot(jnp.concatenate([hi, lo], axis=0), ones_bd)
    return s[:n] + s[n:]


def _wkv_kernel(*refs, rows, vres, ng, fused):
    _wkv_body(refs, pl.program_id(2), pl.num_programs(2), rows=rows, vres=vres, ng=ng, fused=fused)


def _wkv_num_inputs(vres, fused):
    return (3 + vres) + (3 + vres) * (2 if fused else 1) + 2


def _wkv_body(refs, c, nc, *, rows, vres, ng, fused, rider=None):
    n_rk = 3 + vres
    nb = 3 + vres
    rk_refs = refs[:n_rk]
    br_refs = refs[n_rk:n_rk + (2 * nb if fused else nb)]
    p_ref, s0_ref, y_ref, so_ref, s_ref = refs[n_rk + len(br_refs):]
    row_head = lax.broadcasted_iota(jnp.int32, (GROUP, GROUP), 0) // RWKV_HEAD
    lane_head = lax.broadcasted_iota(jnp.int32, (GROUP, GROUP), 1) // RWKV_HEAD

    @pl.when(c == 0)
    def _():
        for gi in range(ng):
            s_ref[gi] = jnp.where(row_head == lane_head, s0_ref[gi], 0.0)

    if rider is not None:
        rider[0]()

    gens = []
    hidden = [ref[...] for ref in br_refs[:nb]] if fused else None
    for gi in range(ng):
        lanes = pl.ds(gi * GROUP, GROUP)
        rk = [ref[:, lanes] for ref in rk_refs]
        if fused:
            br = [_dot(h, w2_ref[:, lanes]) for h, w2_ref in zip(hidden, br_refs[nb:])]
        else:
            br = [ref[:, lanes] for ref in br_refs]
        vals = rk[:3] + br[:3] + ([rk[3], br[3]] if vres else []) + [p_ref[:, lanes], s_ref[gi]]
        gens.append(_wkv_group(*vals, rows=rows, vres=vres))
    if rider is not None:
        gens.append(rider[1]())
    outs = [None] * len(gens)
    live = list(range(len(gens)))
    while live:
        for gi in list(live):
            try:
                next(gens[gi])
            except StopIteration as done:
                outs[gi] = done.value
                live.remove(gi)
    for gi, (y, s_new) in enumerate(outs[:ng]):
        y_ref[:, pl.ds(gi * GROUP, GROUP)] = y.astype(y_ref.dtype)
        s_ref[gi] = s_new

    if rider is not None:
        rider[2]()

    @pl.when(c == nc - 1)
    def _():
        for gi in range(ng):
            s = s_ref[gi]
            packed = s
            for h in range(1, HEADS_PER_GROUP):
                packed = jnp.where(row_head == h, pltpu.roll(s, GROUP - h * RWKV_HEAD, 1), packed)
            so_ref[gi] = packed[:, :RWKV_HEAD]


def _wkv_group(r_ref, k_ref, v_ref, wl_ref, al_ref, g_ref, *rest, rows, vres):
    if vres:
        vf_ref, vl_ref, p, s = rest
    else:
        p, s = rest
    L = CHUNK
    W = GROUP
    HG = HEADS_PER_GROUP

    def load(val):
        x = val.astype(F32)
        if rows < L:
            x = jnp.concatenate([x, jnp.zeros((L - rows, W), F32)], axis=0)
        return x

    w0, a0, k_k, k_a = p[0:1], p[1:2], p[2:3], p[3:4]
    lnx_w, lnx_b, rk, v0 = p[4:5], p[5:6], p[6:7], p[7:8]

    lane_head = lax.broadcasted_iota(jnp.int32, (1, W), 1) // RWKV_HEAD
    rr = lax.broadcasted_iota(jnp.int32, (W, W), 0) // RWKV_HEAD
    cc = lax.broadcasted_iota(jnp.int32, (W, W), 1) // RWKV_HEAD
    bd = rr == cc
    ones_bd = jnp.where(bd, 1.0, 0.0).astype(BF16)
    trow = lax.broadcasted_iota(jnp.int32, (L, HG * L), 0)
    tcol = lax.broadcasted_iota(jnp.int32, (L, HG * L), 1) & (L - 1)
    strict = tcol < trow
    incl = tcol <= trow
    rb = lax.broadcasted_iota(jnp.int32, (HG * L, HG * L), 0) // L
    cb = lax.broadcasted_iota(jnp.int32, (HG * L, HG * L), 1) // L
    bd_l = rb == cb

    def ystack(x):
        return jnp.concatenate(
            [jnp.where(lane_head == h, x, 0.0) for h in range(HG)], axis=0).astype(BF16)

    r = load(r_ref)
    k = load(k_ref)
    v = load(v_ref)
    a_sig = jax.nn.sigmoid(load(al_ref) + a0)
    if vres:
        v = v + (load(vf_ref) - v) * jax.nn.sigmoid(load(vl_ref) + v0)
    z = -(load(wl_ref) + w0)
    softplus = jnp.maximum(z, 0.0) + jnp.log(1.0 + jnp.exp(-jnp.abs(z)))
    lw = -jnp.exp(-softplus - 0.5)
    if rows < L:
        trow1 = lax.broadcasted_iota(jnp.int32, (L, W), 0)
        lw = jnp.where(trow1 < rows, lw, 0.0)
    kkr = k * k_k
    nrm2 = _seg_sum(kkr * kkr, ones_bd)
    yield
    nrm = jnp.sqrt(nrm2)
    kk = kkr / jnp.maximum(nrm, 1e-12)
    k2 = k * (1.0 + (a_sig - 1.0) * k_a)
    a = -kk
    b = kk * a_sig

    l1 = lw.astype(BF16)
    l2r = lw - l1.astype(F32)
    l2 = l2r.astype(BF16)
    l3 = (l2r - l2.astype(F32)).astype(BF16)
    tri = (lax.broadcasted_iota(jnp.int32, (L, L), 1)
           <= lax.broadcasted_iota(jnp.int32, (L, L), 0))
    tri = jnp.where(tri, 1.0, 0.0).astype(BF16)
    cs = _dot(tri, jnp.concatenate([l1, l2, l3], axis=1))
    yield
    cum = cs[:, :W] + cs[:, W:2 * W] + cs[:, 2 * W:]
    cl = cum[L - 1:L, :]
    e_c = jnp.exp(cum)
    e_cm = jnp.exp(cum - lw)
    e_nc = jnp.exp(-cum)
    e_lc = jnp.exp(cl - cum)
    g_l = jnp.exp(cl)
    at = a * e_cm
    rt = r * e_c
    bt = b * e_nc
    kt = k2 * e_nc
    bh = b * e_lc
    kh = k2 * e_lc

    x4 = _dot_nt(jnp.concatenate([at, rt], axis=0).astype(BF16),
                 jnp.concatenate([ystack(bt), ystack(kt)], axis=0))
    yield
    n_ab = jnp.where(strict, x4[:L, :HG * L], 0.0)
    m_ak = jnp.where(strict, x4[:L, HG * L:], 0.0)
    m_rb = jnp.where(incl, x4[L:, :HG * L], 0.0)
    m_rk = jnp.where(incl, x4[L:, HG * L:], 0.0)

    def bdiag(x):
        return jnp.where(bd_l, jnp.concatenate([x] * HG, axis=0), 0.0).astype(BF16)

    t_inv = jnp.where(tcol == trow, 1.0, 0.0) + n_ab
    yv = ystack(v)
    makv = _dot(m_ak.astype(BF16), yv)
    pw = _dot(n_ab.astype(BF16), bdiag(n_ab))
    yield
    steps = int(math.log2(L)) - 1
    for it in range(steps):
        rhs = bdiag(pw)
        if it < steps - 1:
            res = _dot(jnp.concatenate([t_inv, pw], axis=0).astype(BF16), rhs)
            yield
            t_inv = t_inv + res[:L]
            pw = res[L:]
        else:
            res = _dot(t_inv.astype(BF16), rhs)
            yield
            t_inv = t_inv + res

    au = _dot(t_inv.astype(BF16), jnp.concatenate([ystack(at), ystack(makv)], axis=1))
    yield
    abar = au[:, :W]
    u0 = au[:, W:]
    y_abar = ystack(abar)
    d1 = _dot(m_rb.astype(BF16), jnp.concatenate([y_abar, ystack(u0)], axis=1))
    d2 = _dot(m_rk.astype(BF16), yv)
    s_bf = s.astype(BF16)
    sa = _dot_nt(s_bf, y_abar)
    yield
    rbar = rt + d1[:, :W]
    o0 = d1[:, W:] + d2
    o = _dot_nt(rbar.astype(BF16), s_bf) + o0
    uv_t = jnp.concatenate([u0, v], axis=0).T
    q_t = _dot(uv_t.astype(BF16), jnp.concatenate([bh, kh], axis=0).astype(BF16))
    s_new = s * g_l + _dot(sa.astype(BF16), ystack(bh)) + jnp.where(bd, q_t, 0.0)
    bonus = _seg_sum(r * k2 * rk, ones_bd) * v
    yield
    inv_n = 1.0 / RWKV_HEAD
    mean = _seg_sum(o, ones_bd) * inv_n
    yield
    dlt = o - mean
    var = _seg_sum(dlt * dlt, ones_bd) * inv_n
    yield
    yn = dlt * lax.rsqrt(var + GN_EPS) * lnx_w + lnx_b
    out = (yn + bonus) * load(g_ref)
    return out[:rows], s_new


def _wkv(rkv, branches, params, s0, *, B, T, rkv_first=None, out_dtype=BF16, decode=None):
    _, M, D = rkv.shape
    fused = isinstance(branches[0], tuple)
    vres = rkv_first is not None
    n_groups = B * D // GROUP
    if s0 is None:
        s0_rep = jnp.zeros((n_groups, GROUP, GROUP), F32)
    else:
        s0_rep = jnp.tile(s0.astype(F32).reshape(n_groups, GROUP, RWKV_HEAD), (1, 1, HEADS_PER_GROUP))
    ng, G, rows, nc = _wkv_grid(D, T)
    mk = (lambda f: f) if decode is None else (lambda f: (lambda b, g, c, pt: f(b, g, c)))

    def plane(pi):
        return pl.BlockSpec((None, rows, ng * GROUP), mk(lambda b, g, c: (pi, b * nc + c, g)))

    tile = pl.BlockSpec((rows, ng * GROUP), mk(lambda b, g, c: (b * nc + c, g)))
    lanes = mk(lambda b, g, c: (0, g))
    state = pl.BlockSpec((ng, GROUP, GROUP), mk(lambda b, g, c: (b * G + g, 0, 0)))
    in_specs = [plane(0), plane(1), plane(2)]
    args = [rkv, rkv, rkv]
    if vres:
        in_specs.append(plane(2))
        args.append(rkv_first)
    if fused:
        in_specs += [pl.BlockSpec((rows, h.shape[1]), mk(lambda b, g, c: (b * nc + c, 0)))
                     for h, _ in branches]
        in_specs += [pl.BlockSpec((w2.shape[0], ng * GROUP), lanes) for _, w2 in branches]
        args += [h for h, _ in branches] + [w2 for _, w2 in branches]
    else:
        in_specs += [tile] * len(branches)
        args += list(branches)
    in_specs += [pl.BlockSpec((8, ng * GROUP), lanes), state]
    args += [params, s0_rep]
    out_specs = [tile, pl.BlockSpec((ng, GROUP, RWKV_HEAD), mk(lambda b, g, c: (b * G + g, 0, 0)))]
    out_shape = [jax.ShapeDtypeStruct((M, D), out_dtype),
                 jax.ShapeDtypeStruct((n_groups, GROUP, RWKV_HEAD), F32)]
    scratch = [pltpu.VMEM((ng, GROUP, GROUP), F32)]
    wkv_kw = dict(rows=rows, vres=vres, ng=ng, fused=fused)
    if decode is None:
        y, s_out = pl.pallas_call(
            functools.partial(_wkv_kernel, **wkv_kw),
            grid=(B, G, nc),
            in_specs=in_specs,
            out_specs=out_specs,
            out_shape=out_shape,
            scratch_shapes=scratch,
            compiler_params=_cparams("parallel", "parallel", "arbitrary"),
            name="wkv7",
        )(*args)
        return y, s_out.reshape(B, D // RWKV_HEAD, RWKV_HEAD, RWKV_HEAD)

    spb, npp = _ride_shape(B, D, T, decode["page_table"])
    step = lambda b, g, c: (b * G + g) * nc + c
    dec = _decode_call_parts(decode, npp, lambda b, g, c, pt: step(b, g, c) // spb,
                             lambda b, g, c, pt: (step(b, g, c) % spb) * npp)
    n_wkv_in, n_dec_in = len(args), len(dec["args"])
    grid_spec = pltpu.PrefetchScalarGridSpec(
        num_scalar_prefetch=1,
        grid=(B, G, nc),
        in_specs=in_specs + dec["in_specs"],
        out_specs=out_specs + [dec["out_spec"]],
        scratch_shapes=scratch + dec["scratch"],
    )
    y, s_out, o = pl.pallas_call(
        functools.partial(_wkv_decode_kernel, n_wkv_in=n_wkv_in, n_dec_in=n_dec_in, G=G, nc=nc,
                          spb=spb, wkv_kw=wkv_kw, dec_kw=dec["kw"]),
        grid_spec=grid_spec,
        out_shape=out_shape + [dec["out_shape"]],
        compiler_params=pltpu.CompilerParams(
            dimension_semantics=("arbitrary", "arbitrary", "arbitrary"),
            vmem_limit_bytes=RIDE_VMEM_LIMIT),
        name="wkv7_decode",
    )(decode["page_table"].reshape(-1), *args, *dec["args"])
    return y, s_out.reshape(B, D // RWKV_HEAD, RWKV_HEAD, RWKV_HEAD), o


def _wkv_grid(D, T):
    ng = max(d for d in range(1, WKV_GROUPS_PER_STEP + 1) if (D // GROUP) % d == 0)
    rows = min(CHUNK, T)
    return ng, D // (ng * GROUP), rows, T // rows


MAX_RIDING_PAGES = 8


def _ride_shape(B, D, T, page_table):
    _, G, _, nc = _wkv_grid(D, T)
    steps = B * G * nc
    Bd, npg = page_table.shape
    if steps % Bd or npg % (steps // Bd) or npg // (steps // Bd) > MAX_RIDING_PAGES:
        return None
    return steps // Bd, npg // (steps // Bd)


def _wkv_decode_kernel(pt_ref, *refs, n_wkv_in, n_dec_in, G, nc, spb, wkv_kw, dec_kw):
    wkv_in = refs[:n_wkv_in]
    dec_in = refs[n_wkv_in:n_wkv_in + n_dec_in]
    y_ref, so_ref, o_ref, s_ref, m_ref, l_ref, acc_ref = refs[n_wkv_in + n_dec_in:]
    c = pl.program_id(2)
    step = (pl.program_id(0) * G + pl.program_id(1)) * nc + c
    rider = _decode_parts(tuple(dec_in) + (o_ref, m_ref, l_ref, acc_ref), step % spb, spb, **dec_kw)
    _wkv_body(tuple(wkv_in) + (y_ref, so_ref, s_ref), c, nc, rider=rider, **wkv_kw)


def _lambda(lp, lam_init):
    return (jnp.exp(jnp.sum(lp[0:1] * lp[1:2], keepdims=True))
            - jnp.exp(jnp.sum(lp[2:3] * lp[3:4], keepdims=True)) + lam_init)


Q_SCALE = ATT_HEAD ** -0.5 * math.log2(math.e)


def _rep(x, n):
    return x if n == LANES else jnp.concatenate([x] * (n // LANES), axis=1)


def _softmax_step(s, vt, m, l, acc, ones=None):
    m_new = jnp.maximum(m, jnp.max(s, axis=-1, keepdims=True))
    alpha = jnp.exp2(m - m_new)
    p = jnp.exp2(s - _rep(m_new, s.shape[1]))
    pb = p.astype(BF16)
    if ones is None:
        l_new = alpha * l + jnp.sum(p, axis=-1, keepdims=True)
    else:
        l_new = alpha * l + _dot(pb, ones)
    acc_new = _rep(alpha, acc.shape[1]) * acc + _dot(pb, vt)
    return m_new, l_new, acc_new


def _attn_prompt_kernel(q_ref, k_ref, v_ref, lp_ref, sg_ref, o_ref, m_ref, l_ref, acc_ref,
                        *, tq, lam_init):
    qi = pl.program_id(2)
    q = q_ref[...]
    qs = (q[:, :ATT_HEAD], q[:, ATT_HEAD:])
    m_ref[...] = jnp.full(m_ref.shape, NEG_BIG, F32)
    l_ref[...] = jnp.zeros(l_ref.shape, F32)
    acc_ref[...] = jnp.zeros(acc_ref.shape, F32)

    def scores(j):
        kt = k_ref[pl.ds(pl.multiple_of(j * tq, tq), tq), :]
        return tuple(_dot_nt(qs[mp], kt[:, mp * ATT_HEAD:(mp + 1) * ATT_HEAD]) for mp in range(2))

    def update(j, s):
        vt = v_ref[pl.ds(pl.multiple_of(j * tq, tq), tq), :]
        for mp in range(2):
            m_ref[mp], l_ref[mp], acc_ref[mp] = _softmax_step(
                s[mp], vt, m_ref[mp], l_ref[mp], acc_ref[mp])

    def body(j, s):
        s_next = scores(j + 1)
        update(j, s)
        return s_next

    s = lax.fori_loop(0, qi, body, scores(0))
    keep = (lax.broadcasted_iota(jnp.int32, (tq, tq), 1)
            <= lax.broadcasted_iota(jnp.int32, (tq, tq), 0))
    update(qi, tuple(jnp.where(keep, x, NEG_BIG) for x in s))
    lam = _lambda(lp_ref[...], lam_init)
    hw = acc_ref.shape[2]
    o = acc_ref[0] / _rep(l_ref[0], hw) - lam * (acc_ref[1] / _rep(l_ref[1], hw))
    o = _rms(o, SUBLN_EPS) * sg_ref[...] * (1.0 - lam_init)
    o_ref[...] = o.astype(o_ref.dtype)


def _attn_prompt(q, k, v, lp, sg, *, B, T, lam_init, tq=512):
    M, Wd = q.shape
    HW = 2 * ATT_HEAD
    H = Wd // HW
    tq = min(tq, T)
    nq = T // tq
    return pl.pallas_call(
        functools.partial(_attn_prompt_kernel, tq=tq, lam_init=lam_init),
        grid=(B, H, nq),
        in_specs=[pl.BlockSpec((tq, HW), lambda b, h, i: (b * nq + i, h)),
                  pl.BlockSpec((T, HW), lambda b, h, i: (b, h)),
                  pl.BlockSpec((T, HW), lambda b, h, i: (b, h)),
                  pl.BlockSpec((4, ATT_HEAD), lambda b, h, i: (0, 0)),
                  pl.BlockSpec((1, HW), lambda b, h, i: (0, 0))],
        out_specs=pl.BlockSpec((tq, HW), lambda b, h, i: (b * nq + i, h)),
        out_shape=jax.ShapeDtypeStruct((M, Wd), BF16),
        scratch_shapes=[pltpu.VMEM((2, tq, LANES), F32), pltpu.VMEM((2, tq, LANES), F32),
                        pltpu.VMEM((2, tq, HW), F32)],
        compiler_params=_cparams("parallel", "parallel", "arbitrary"),
        name="diff_attn_prompt",
    )(q, k, v, lp, sg.reshape(1, HW))


PAGES_PER_STEP = 4


def _decode_parts(refs, p, nsteps, *, heads, tq, npp, new_rows, lam_init):
    q_ref = refs[0]
    k_refs = refs[1:1 + npp]
    v_refs = refs[1 + npp:1 + 2 * npp]
    kn_ref, vn_ref, lp_ref, sg_ref, o_ref, m_ref, l_ref, acc_ref = refs[1 + 2 * npp:]
    HW = 2 * ATT_HEAD
    R = heads * tq

    def queries():
        q = q_ref[...]
        return [jnp.concatenate([q[:, h * HW + mp * ATT_HEAD:h * HW + (mp + 1) * ATT_HEAD]
                                 for h in range(heads)], axis=0).astype(BF16) for mp in range(2)]

    def raw_scores(qm, k0, k1):
        return jnp.concatenate([_dot_nt(qm[0], k0.astype(BF16)), _dot_nt(qm[1], k1.astype(BF16))],
                               axis=0)

    def head_mask(ncol):
        row = lax.broadcasted_iota(jnp.int32, (2 * R, ncol), 0)
        col = lax.broadcasted_iota(jnp.int32, (2 * R, ncol), 1)
        return (col % heads) == ((row // tq) % heads), row, col

    def init():
        @pl.when(p == 0)
        def _():
            m_ref[...] = jnp.full(m_ref.shape, NEG_BIG, F32)
            l_ref[...] = jnp.zeros(l_ref.shape, F32)
            acc_ref[...] = jnp.zeros(acc_ref.shape, F32)

    def stages():
        qm = queries()
        state = (m_ref[...], l_ref[...], acc_ref[...])
        nk = k_refs[0].shape[0] // 2
        same_head, _, _ = head_mask(nk)
        for j in range(npp):
            g = raw_scores(qm, k_refs[j][pl.ds(0, nk, stride=2), :],
                           k_refs[j][pl.ds(1, nk, stride=2), :])
            yield
            state = _softmax_step(jnp.where(same_head, g, NEG_BIG),
                                  v_refs[j][...].astype(BF16), *state)
            yield
        m_ref[...], l_ref[...], acc_ref[...] = state

    def final():
        @pl.when(p == nsteps - 1)
        def _():
            pad_to = max(new_rows, LANES)
            k0 = kn_ref[pl.ds(0, new_rows, stride=2), :]
            k1 = kn_ref[pl.ds(1, new_rows, stride=2), :]
            vn = vn_ref[...]
            if pad_to > new_rows:
                zk = jnp.zeros((pad_to - new_rows, ATT_HEAD), F32)
                k0 = jnp.concatenate([k0, zk], axis=0)
                k1 = jnp.concatenate([k1, zk], axis=0)
                vn = jnp.concatenate([vn, jnp.zeros((pad_to - new_rows, HW), F32)], axis=0)
            same, row, col = head_mask(pad_to)
            valid = same & ((col // heads) <= (row % tq)) & (col < new_rows)
            s = jnp.where(valid, raw_scores(queries(), k0, k1), NEG_BIG)
            m, l, acc = _softmax_step(s, vn.astype(BF16), m_ref[...], l_ref[...], acc_ref[...])
            lam = _lambda(lp_ref[...], lam_init)
            sg = sg_ref[...]
            for h in range(heads):
                r0 = slice(h * tq, (h + 1) * tq)
                r1 = slice(R + h * tq, R + (h + 1) * tq)
                o = acc[r0] / _rep(l[r0], HW) - lam * (acc[r1] / _rep(l[r1], HW))
                o = _rms(o, SUBLN_EPS) * sg * (1.0 - lam_init)
                o_ref[:, h * HW:(h + 1) * HW] = o

    return init, stages, final


def _attn_sample_kernel(pt_ref, *refs, **kw):
    init, stages, final = _decode_parts(refs, pl.program_id(1), pl.num_programs(1), **kw)
    init()
    for _ in stages():
        pass
    final()


def _decode_call_parts(d, npp, seq_of, first_page_of):
    Bd, npg = d["page_table"].shape
    M, Wd = d["q"].shape
    tq = M // Bd
    HW = 2 * ATT_HEAD
    heads = Wd // HW
    krows, vrows = d["cache_k"].shape[1], d["cache_v"].shape[1]
    new_rows = tq * heads
    rows = pl.BlockSpec((tq, Wd), lambda *ids: (seq_of(*ids), 0))
    per_seq = lambda *ids: (seq_of(*ids), 0, 0)
    const = lambda *ids: (0, 0)

    def page_spec(nrows, width, j):
        return pl.BlockSpec(
            (None, nrows, width),
            lambda *ids: (ids[-1][seq_of(*ids) * npg + first_page_of(*ids) + j], 0, 0))

    return dict(
        in_specs=([rows] + [page_spec(krows, ATT_HEAD, j) for j in range(npp)]
                  + [page_spec(vrows, HW, j) for j in range(npp)]
                  + [pl.BlockSpec((None, 2 * new_rows, ATT_HEAD), per_seq),
                     pl.BlockSpec((None, new_rows, HW), per_seq),
                     pl.BlockSpec((4, ATT_HEAD), const),
                     pl.BlockSpec((1, HW), const)]),
        args=[d["q"], *([d["cache_k"]] * npp), *([d["cache_v"]] * npp),
              d["k_new"].reshape(Bd, 2 * new_rows, ATT_HEAD), d["v_new"].reshape(Bd, new_rows, HW),
              d["lp"], d["sg"].reshape(1, HW)],
        out_spec=rows,
        out_shape=jax.ShapeDtypeStruct((M, Wd), F32),
        scratch=[pltpu.VMEM((2 * new_rows, LANES), F32),
                 pltpu.VMEM((2 * new_rows, LANES), F32),
                 pltpu.VMEM((2 * new_rows, HW), F32)],
        kw=dict(heads=heads, tq=tq, npp=npp, new_rows=new_rows, lam_init=d["lam_init"]),
    )


def _attn_sample(d):
    Bd, npg = d["page_table"].shape
    npp = max(n for n in range(1, PAGES_PER_STEP + 1) if npg % n == 0)
    parts = _decode_call_parts(d, npp, lambda b, p, pt: b, lambda b, p, pt: p * npp)
    grid_spec = pltpu.PrefetchScalarGridSpec(
        num_scalar_prefetch=1,
        grid=(Bd, npg // npp),
        in_specs=parts["in_specs"],
        out_specs=parts["out_spec"],
        scratch_shapes=parts["scratch"],
    )
    return pl.pallas_call(
        functools.partial(_attn_sample_kernel, **parts["kw"]),
        grid_spec=grid_spec,
        out_shape=parts["out_shape"],
        compiler_params=_cparams("parallel", "arbitrary"),
        name="diff_attn_sample",
    )(d["page_table"].reshape(-1), *parts["args"])


def _rope_tables(pos, reps):
    half = ATT_HEAD // 2
    inv = jnp.power(ROPE_THETA, -jnp.arange(half, dtype=F32) / half)
    ang = pos.astype(F32)[:, None] * inv[None, :]
    cos = jnp.concatenate([jnp.cos(ang), jnp.cos(ang)], axis=-1)
    sin = jnp.concatenate([-jnp.sin(ang), jnp.sin(ang)], axis=-1)
    return jnp.tile(cos, (reps, 1)), jnp.tile(sin, (reps, 1))


def _lambda_init(layer_idx):
    return 0.8 - 0.6 * math.exp(-0.3 * layer_idx)


def kernel(x_prompt, x_sample, cache_k, cache_v, state_shift, state_wkv, page_table, norm_mix, norm_ffn, rwkv_mu, rwkv_vec, rwkv_wr, rwkv_wk, rwkv_wv, rwkv_wo, rwkv_w1, rwkv_w2, rwkv_a1, rwkv_a2, rwkv_v0, rwkv_v1, rwkv_v2, rwkv_g1, rwkv_g2, rwkv_rk, kv_norm, kv_wk, kv_wv, attn_wq, attn_wo, attn_lambda, attn_subln, ffn_w1, ffn_w2):
    depth = norm_mix.shape[0]
    n_a = rwkv_mu.shape[0]
    D = x_prompt.shape[-1]
    Wd = kv_wk.shape[1]
    n_att_heads = Wd // (2 * ATT_HEAD)
    n_rwkv_heads = D // RWKV_HEAD

    w_rkv = [jnp.stack([rwkv_wr[l], rwkv_wk[l], rwkv_wv[l]]).astype(BF16) for l in range(n_a)]
    w_o = [rwkv_wo[l].astype(BF16) for l in range(n_a)]
    lora_w = [_pad_lora(rwkv_w1[l], rwkv_w2[l]) for l in range(n_a)]
    lora_a = [_pad_lora(rwkv_a1[l], rwkv_a2[l]) for l in range(n_a)]
    lora_g = [_pad_lora(rwkv_g1[l], rwkv_g2[l]) for l in range(n_a)]
    lora_v = [_pad_lora(rwkv_v1[l], rwkv_v2[l]) for l in range(n_a - 1)]
    wkv_params = []
    for l in range(n_a):
        v0 = rwkv_v0[l - 1] if l > 0 else jnp.zeros((D,), F32)
        wkv_params.append(jnp.concatenate(
            [rwkv_vec[l], rwkv_rk[l].reshape(1, D), v0.reshape(1, D)], axis=0))
    w_kv = jnp.stack([kv_wk, kv_wv]).astype(BF16)
    w_q = [attn_wq[j].astype(BF16) for j in range(depth - n_a)]
    w_ao = [attn_wo[j].astype(BF16) for j in range(depth - n_a)]
    w_f1 = ffn_w1.astype(BF16)
    w_f2 = ffn_w2.astype(BF16)

    def trunk(x, B, T, shift0, wkv0, rope, attend, small):
        act_dtype = F32 if small else BF16
        new_shift, new_wkv = [], []
        rkv_first = None
        k_sh = v_sh = k_att = v_att = None
        xn_next = None
        for l in range(depth):
            if l < n_a:
                loras = [lora_w[l], lora_a[l], lora_g[l]] + ([lora_v[l - 1]] if l > 0 else [])
                if small:
                    mix, last = _prep(x, shift0[l], norm_mix[l, 0], rwkv_mu[l], B=B, T=T,
                                      out_dtype=act_dtype)
                    rkv = _matmul((mix, PL_R), w_rkv[l])
                    branches = [_lora((mix, plane), w1, w2, act=act)
                                for (plane, act), (w1, w2) in zip(LORA_BRANCHES, loras)]
                else:
                    rkv, hidden, last = _rwkv_in(x, shift0[l], norm_mix[l, 0], rwkv_mu[l], w_rkv[l],
                                                 [w1 for w1, _ in loras], B=B, T=T)
                    branches = [(h, w2) for h, (_, w2) in zip(hidden, loras)]
                new_shift.append(last)
                if l == 0:
                    rkv_first = rkv
                wkv_call = ((rkv, branches, wkv_params[l], None if wkv0 is None else wkv0[l]),
                            dict(B=B, T=T, rkv_first=rkv_first if l > 0 else None,
                                 out_dtype=act_dtype))
                if small:
                    branch, s_new = _wkv(*wkv_call[0], **wkv_call[1])
                else:
                    branch, s_new = yield ("wkv", wkv_call)
                new_wkv.append(s_new.astype(state_wkv.dtype))
                w_out = w_o[l]
            else:
                j = l - n_a
                lam_init = _lambda_init(l)
                q = _matmul(xn_next, w_q[j], out_dtypes=(act_dtype,), rope=rope, out_scale=Q_SCALE)
                if small:
                    branch = yield ("attn", dict(q=q, k_new=k_att, v_new=v_att, lp=attn_lambda[j],
                                                 sg=attn_subln[j], lam_init=lam_init))
                else:
                    branch = attend(q, k_att, v_att, attn_lambda[j], attn_subln[j], lam_init)
                w_out = w_ao[j]
            x, nrm = _resnorm(x, branch, norm_mix[l, 1], [norm_ffn[l, 0]], w=w_out)
            g_next = []
            if l == n_a - 1:
                g_next.append(kv_norm)
            if n_a - 1 <= l < depth - 1:
                g_next.append(norm_mix[l + 1, 0])
            x, nrm = _mlp((nrm, 0), (w_f1, l), (w_f2, l), x, norm_ffn[l, 1], g_next)
            if l == n_a - 1:
                if small:
                    k_sh = k_att = _matmul((nrm, 0), w_kv[0], rope=rope)
                    v_sh = v_att = _matmul((nrm, 0), w_kv[1])
                else:
                    k_sh, k_att = _matmul((nrm, 0), w_kv[0], rope=rope, out_dtypes=(F32, BF16))
                    v_sh, v_att = _matmul((nrm, 0), w_kv[1], out_dtypes=(F32, BF16))
            if g_next:
                xn_next = (nrm, len(g_next) - 1)
        return x, k_sh, v_sh, jnp.stack(new_shift), jnp.stack(new_wkv)

    Bp, Tp, _ = x_prompt.shape
    shift0_p = jnp.zeros((n_a, Bp, D), F32)
    wkv0_p = None
    rope_p = _rope_tables(jnp.arange(Tp, dtype=jnp.int32), 1)

    def attend_p(q, k, v, lp, sg, lam_init):
        return _attn_prompt(q, k, v, lp, sg, B=Bp, T=Tp, lam_init=lam_init)

    prompt = trunk(x_prompt.reshape(Bp * Tp, D), Bp, Tp, shift0_p, wkv0_p, rope_p, attend_p, False)

    Bd, Td, _ = x_sample.shape
    n_pages = page_table.shape[1]
    page = cache_k.shape[1]
    past_len = n_pages * page
    rope_s = _rope_tables(past_len + jnp.arange(Td, dtype=jnp.int32), Bd)
    ck = cache_k.reshape(cache_k.shape[0], page * n_att_heads * 2, ATT_HEAD)
    cv = cache_v.reshape(cache_v.shape[0], page * n_att_heads, 2 * ATT_HEAD)
    sample = trunk(x_sample.reshape(Bd * Td, D), Bd, Td, state_shift, state_wkv, rope_s, None, True)

    def advance(gen, value):
        try:
            return gen.send(value), None
        except StopIteration as done:
            return None, done.value

    can_ride = _ride_shape(Bp, D, Tp, page_table) is not None
    req_p, out_p = advance(prompt, None)
    req_s, out_s = advance(sample, None)
    while req_p is not None or req_s is not None:
        dec = None
        if req_s is not None:
            dec = dict(req_s[1], cache_k=ck, cache_v=cv, page_table=page_table)
        if req_p is not None and dec is not None and can_ride:
            branch, s_new, o = _wkv(*req_p[1][0], **req_p[1][1], decode=dec)
            req_p, out_p = advance(prompt, (branch, s_new))
            req_s, out_s = advance(sample, o)
        elif req_p is not None:
            req_p, out_p = advance(prompt, _wkv(*req_p[1][0], **req_p[1][1]))
        else:
            req_s, out_s = advance(sample, _attn_sample(dec))
    y_p, k_p, v_p, shift_p, wkv_p = out_p
    y_s, k_s, v_s, shift_s, wkv_s = out_s

    H = n_att_heads
    return (y_p.reshape(Bp, Tp, D), y_s.reshape(Bd, Td, D),
            k_p.reshape(Bp, Tp, H, 2, ATT_HEAD), v_p.reshape(Bp, Tp, H, 2 * ATT_HEAD),
            shift_p, wkv_p,
            k_s.reshape(Bd, Td, H, 2, ATT_HEAD), v_s.reshape(Bd, Td, H, 2 * ATT_HEAD),
            shift_s, wkv_s)
```

```python
import functools
import math

import jax
import jax.numpy as jnp
from jax import lax
from jax.experimental import pallas as pl
from jax.experimental.pallas import tpu as pltpu

F32 = jnp.float32
BF16 = jnp.bfloat16

RWKV_HEAD = 64
ATT_HEAD = 128
NORM_EPS = 1e-6
GN_EPS = 64e-5
SUBLN_EPS = 1e-5
ROPE_THETA = 10000.0

LANES = 128
GROUP = 256
HEADS_PER_GROUP = GROUP // RWKV_HEAD
CHUNK = 64
WKV_GROUPS_PER_STEP = 8
VMEM_LIMIT = 48 * 1024 * 1024
MLP_VMEM_LIMIT = 56 * 1024 * 1024
RIDE_VMEM_LIMIT = 60 * 1024 * 1024
NEG_BIG = -1e30


def _cparams(*sem):
    return pltpu.CompilerParams(dimension_semantics=sem, vmem_limit_bytes=VMEM_LIMIT)


def _dot(a, b):
    return jnp.dot(a, b, preferred_element_type=F32)


def _dot_nt(a, b):
    return lax.dot_general(a, b, (((1,), (1,)), ((), ())), preferred_element_type=F32)


def _rms(x, eps):
    return x * lax.rsqrt(jnp.mean(x * x, axis=-1, keepdims=True) + eps)


def _mm_kernel(x_ref, w_ref, *rest, act, rope, out_scale):
    acc = _dot(x_ref[...].astype(BF16), w_ref[...])
    if act == "tanh":
        acc = jnp.tanh(acc)
    elif act == "sigmoid":
        acc = jax.nn.sigmoid(acc)
    o_refs = rest[2:] if rope else rest
    if rope:
        cos = rest[0][...] * out_scale
        sin = rest[1][...] * out_scale
        for c in range(acc.shape[1] // LANES):
            blk = acc[:, c * LANES:(c + 1) * LANES]
            res = blk * cos + pltpu.roll(blk, LANES // 2, 1) * sin
            for o_ref in o_refs:
                o_ref[:, c * LANES:(c + 1) * LANES] = res.astype(o_ref.dtype)
    else:
        if out_scale != 1.0:
            acc = acc * out_scale
        for o_ref in o_refs:
            o_ref[...] = acc.astype(o_ref.dtype)


def _planes(x):
    return x if isinstance(x, tuple) else (x[None], 0)


MAX_WEIGHT_TILE_BYTES = 8 * 1024 * 1024


def _matmul(x, w, *, out_dtypes=(F32,), act=None, rope=None, out_scale=1.0, tm=512):
    x, p0 = _planes(x)
    squeeze = w.ndim == 2
    if squeeze:
        w = w[None]
    _, M, K = x.shape
    G, _, N = w.shape
    tm = min(tm, M)
    if rope is not None:
        tm = min(tm, rope[0].shape[0])
        assert rope[0].shape[0] % tm == 0
    tn = N
    while K * tn * w.dtype.itemsize > MAX_WEIGHT_TILE_BYTES and tn % (2 * LANES) == 0:
        tn //= 2
    assert M % tm == 0 and N % tn == 0
    in_specs = [pl.BlockSpec((None, tm, K), lambda g, i, j: (p0 + g, i, 0)),
                pl.BlockSpec((None, K, tn), lambda g, i, j: (g, 0, j))]
    args = [x, w]
    if rope is not None:
        cos, sin = rope
        nblk = cos.shape[0] // tm
        spec = pl.BlockSpec((tm, LANES), lambda g, i, j: (i % nblk, 0))
        in_specs += [spec, spec]
        args += [cos, sin]
    outs = pl.pallas_call(
        functools.partial(_mm_kernel, act=act, rope=rope is not None, out_scale=out_scale),
        grid=(G, M // tm, N // tn),
        in_specs=in_specs,
        out_specs=[pl.BlockSpec((None, tm, tn), lambda g, i, j: (g, i, j)) for _ in out_dtypes],
        out_shape=[jax.ShapeDtypeStruct((G, M, N), dt) for dt in out_dtypes],
        compiler_params=_cparams("parallel", "parallel", "arbitrary"),
        name="matmul",
    )(*args)
    outs = [o[0] if squeeze else o for o in outs]
    return outs[0] if len(outs) == 1 else outs


def _lora_kernel(x_ref, w1_ref, w2_ref, o_ref, *, act):
    h = _dot(x_ref[...].astype(BF16), w1_ref[...])
    if act == "tanh":
        h = jnp.tanh(h)
    elif act == "sigmoid":
        h = jax.nn.sigmoid(h)
    o_ref[...] = _dot(h.astype(BF16), w2_ref[...])


def _lora(x, w1, w2, *, act=None, tm=512):
    x, p0 = _planes(x)
    _, M, K = x.shape
    R = w1.shape[1]
    N = w2.shape[1]
    tm = min(tm, M)
    return pl.pallas_call(
        functools.partial(_lora_kernel, act=act),
        grid=(M // tm,),
        in_specs=[pl.BlockSpec((None, tm, K), lambda i: (p0, i, 0)),
                  pl.BlockSpec((K, R), lambda i: (0, 0)),
                  pl.BlockSpec((R, N), lambda i: (0, 0))],
        out_specs=pl.BlockSpec((tm, N), lambda i: (i, 0)),
        out_shape=jax.ShapeDtypeStruct((M, N), F32),
        compiler_params=_cparams("parallel"),
        name="lora",
    )(x, w1, w2)


def _pad_lora(w1, w2):
    r = w1.shape[1]
    rp = -(-r // LANES) * LANES
    return (jnp.pad(w1, ((0, 0), (0, rp - r))).astype(BF16),
            jnp.pad(w2, ((0, rp - r), (0, 0))).astype(BF16))


def _mlp_kernel(x_ref, w1_ref, w2_ref, xr_ref, gp_ref, *rest, n_out, nf, n_cast):
    rest = list(rest)
    gn_ref = rest.pop(0) if n_out else None
    cast_in = [rest.pop(0) for _ in range(n_cast)]
    xo_ref = rest.pop(0)
    no_ref = rest.pop(0) if n_out else None
    cast_out = rest
    f = pl.program_id(1)

    @pl.when(f == 0)
    def _():
        xo_ref[...] = jnp.zeros(xo_ref.shape, F32)

    h = jnp.maximum(_dot(x_ref[...], w1_ref[...].astype(BF16)), 0.0)
    xo_ref[...] += _dot((h * h).astype(BF16), w2_ref[...].astype(BF16))
    for src, dst in zip(cast_in, cast_out):
        dst[...] = src[...].astype(dst.dtype)

    @pl.when(f == nf - 1)
    def _():
        x = xr_ref[...] + _rms(xo_ref[...], NORM_EPS) * gp_ref[...]
        xo_ref[...] = x
        if n_out:
            y = _rms(x, NORM_EPS)
            for j in range(n_out):
                no_ref[j] = (y * gn_ref[j:j + 1, :]).astype(no_ref.dtype)


def _mlp(x, w1, w2, x_res, g_post, g_next, *, cast=(), tm=512, tf=1024):
    x, p0 = _planes(x)
    w1, l1 = _planes(w1)
    w2, l2 = _planes(w2)
    _, M, D = x.shape
    F = w1.shape[2]
    tm = min(tm, M)
    tf = min(tf, F)
    n_out = len(g_next)
    nf = F // tf
    steps = (M // tm) * nf
    row = pl.BlockSpec((tm, D), lambda i, f: (i, 0))
    vec = pl.BlockSpec((1, D), lambda i, f: (0, 0))
    in_specs = [pl.BlockSpec((None, tm, D), lambda i, f: (p0, i, 0)),
                pl.BlockSpec((None, D, tf), lambda i, f: (l1, 0, f)),
                pl.BlockSpec((None, tf, D), lambda i, f: (l2, f, 0)),
                pl.BlockSpec((tm, D), lambda i, f: (i, 0), pipeline_mode=pl.Buffered(1)),
                vec]
    args = [x, w1, w2, x_res, g_post.reshape(1, D)]
    out_specs = [row]
    out_shape = [jax.ShapeDtypeStruct((M, D), F32)]
    if n_out:
        in_specs.append(pl.BlockSpec((n_out, D), lambda i, f: (0, 0)))
        args.append(jnp.stack(g_next))
        out_specs.append(pl.BlockSpec((n_out, tm, D), lambda i, f: (0, i, 0)))
        out_shape.append(jax.ShapeDtypeStruct((n_out, M, D), BF16))
    for cw, plane in cast:
        slab = cw.shape[1] // steps
        assert slab * steps == cw.shape[1] and slab % 16 == 0
        in_specs.append(pl.BlockSpec((None, slab, cw.shape[2]),
                                     lambda i, f, plane=plane: (plane, i * nf + f, 0)))
        args.append(cw)
        out_specs.append(pl.BlockSpec((slab, cw.shape[2]), lambda i, f: (i * nf + f, 0)))
        out_shape.append(jax.ShapeDtypeStruct(cw.shape[1:], BF16))
    res = pl.pallas_call(
        functools.partial(_mlp_kernel, n_out=n_out, nf=nf, n_cast=len(cast)),
        grid=(M // tm, nf),
        in_specs=in_specs,
        out_specs=out_specs,
        out_shape=out_shape,
        compiler_params=pltpu.CompilerParams(dimension_semantics=("arbitrary", "arbitrary"),
                                             vmem_limit_bytes=MLP_VMEM_LIMIT),
        name="mlp",
    )(*args)
    n_main = 1 + bool(n_out)
    return res[0], (res[1] if n_out else None), list(res[n_main:])


MIX_ORDER = (0, 2, 3, 1, 4, 5)
PL_R, PL_K, PL_V, PL_W, PL_A, PL_G = range(6)


def _prep_kernel(x_ref, sh_ref, g_ref, mu_ref, mix_ref, last_ref, carry_ref):
    i = pl.program_id(1)
    xn = _rms(x_ref[...], NORM_EPS) * g_ref[...]
    tr = xn.shape[0]

    @pl.when(i == 0)
    def _():
        carry_ref[...] = sh_ref[...]

    row = lax.broadcasted_iota(jnp.int32, xn.shape, 0)
    xprev = jnp.where(row == 0, carry_ref[...], pltpu.roll(xn, 1, 0))
    xx = xprev - xn
    for plane, m in enumerate(MIX_ORDER):
        mix_ref[plane] = (xn + xx * mu_ref[m:m + 1, :]).astype(mix_ref.dtype)
    last = xn[tr - 1:tr, :]
    carry_ref[...] = last
    last_ref[...] = last


def _prep(x, shift0, g, mu, *, B, T, out_dtype):
    M, D = x.shape
    tr = min(256, T)
    nt = T // tr
    mix, last = pl.pallas_call(
        _prep_kernel,
        grid=(B, nt),
        in_specs=[pl.BlockSpec((tr, D), lambda b, i: (b * nt + i, 0)),
                  pl.BlockSpec((None, 1, D), lambda b, i: (b, 0, 0)),
                  pl.BlockSpec((1, D), lambda b, i: (0, 0)),
                  pl.BlockSpec((6, D), lambda b, i: (0, 0))],
        out_specs=[pl.BlockSpec((6, tr, D), lambda b, i: (0, b * nt + i, 0)),
                   pl.BlockSpec((None, 1, D), lambda b, i: (b, 0, 0))],
        out_shape=[jax.ShapeDtypeStruct((6, M, D), out_dtype),
                   jax.ShapeDtypeStruct((B, 1, D), F32)],
        scratch_shapes=[pltpu.VMEM((1, D), F32)],
        compiler_params=_cparams("arbitrary", "arbitrary"),
        name="rwkv_prep",
    )(x, shift0.reshape(B, 1, D), g.reshape(1, D), mu)
    return mix, last.reshape(B, D)


N_RKV = 3
MIX_SUB_ROWS = 16


def _rwkv_in_kernel(x_ref, sh_ref, g_ref, mu_ref, w_ref, *rest, nt, n_lora):
    l1_refs = rest[:n_lora]
    rkv_ref = rest[n_lora]
    h_refs = rest[n_lora + 1:2 * n_lora + 1]
    last_ref, mix_ref, carry_ref = rest[2 * n_lora + 1:]
    i = pl.program_id(0)
    g = pl.program_id(1)

    @pl.when(g == 0)
    def _():
        tr = x_ref.shape[0]
        sub = min(MIX_SUB_ROWS, tr)
        row = lax.broadcasted_iota(jnp.int32, (sub, x_ref.shape[1]), 0)

        def chunk(ci, prev_row):
            rows = pl.ds(pl.multiple_of(ci * sub, sub), sub)
            xn = _rms(x_ref[rows, :], NORM_EPS) * g_ref[...]
            xx = jnp.where(row == 0, prev_row, pltpu.roll(xn, 1, 0)) - xn
            for plane, m in enumerate(MIX_ORDER):
                mix_ref[plane, rows, :] = (xn + xx * mu_ref[m:m + 1, :]).astype(mix_ref.dtype)
            return xn[sub - 1:sub, :]

        first = jnp.where(i % nt == 0, sh_ref[...], carry_ref[...])
        last = lax.fori_loop(0, tr // sub, chunk, first)
        carry_ref[...] = last
        last_ref[...] = last

    @pl.when(g < N_RKV)
    def _():
        rkv_ref[...] = _dot(mix_ref[g], w_ref[...])

    @pl.when(g == N_RKV)
    def _():
        for (plane, act), l1_ref, h_ref in zip(LORA_BRANCHES, l1_refs, h_refs):
            h = _dot(mix_ref[plane], l1_ref[...])
            if act == "tanh":
                h = jnp.tanh(h)
            elif act == "sigmoid":
                h = jax.nn.sigmoid(h)
            h_ref[...] = h.astype(h_ref.dtype)


LORA_BRANCHES = ((PL_W, "tanh"), (PL_A, None), (PL_G, "sigmoid"), (PL_V, None))


def _rwkv_in(x, shift0, g, mu, w_rkv, lora1, *, B, T, tm=512):
    M, D = x.shape
    tm = min(tm, T)
    nt = T // tm
    n_lora = len(lora1)
    in_specs = [pl.BlockSpec((tm, D), lambda i, g: (i, 0)),
                pl.BlockSpec((None, 1, D), lambda i, g: (i // nt, 0, 0)),
                pl.BlockSpec((1, D), lambda i, g: (0, 0)),
                pl.BlockSpec((6, D), lambda i, g: (0, 0)),
                pl.BlockSpec((None, D, D), lambda i, g: (jnp.minimum(g, N_RKV - 1), 0, 0))]
    in_specs += [pl.BlockSpec(w.shape, lambda i, g: (0, 0), pipeline_mode=pl.Buffered(1))
                 for w in lora1]
    out_specs = [pl.BlockSpec((None, tm, D), lambda i, g: (jnp.minimum(g, N_RKV - 1), i, 0))]
    out_specs += [pl.BlockSpec((tm, w.shape[1]), lambda i, g: (i, 0)) for w in lora1]
    out_specs.append(pl.BlockSpec((None, 1, D), lambda i, g: (i // nt, 0, 0)))
    out_shape = [jax.ShapeDtypeStruct((N_RKV, M, D), F32)]
    out_shape += [jax.ShapeDtypeStruct((M, w.shape[1]), BF16) for w in lora1]
    out_shape.append(jax.ShapeDtypeStruct((B, 1, D), F32))
    res = pl.pallas_call(
        functools.partial(_rwkv_in_kernel, nt=nt, n_lora=n_lora),
        grid=(M // tm, N_RKV + 1),
        in_specs=in_specs,
        out_specs=out_specs,
        out_shape=out_shape,
        scratch_shapes=[pltpu.VMEM((6, tm, D), BF16), pltpu.VMEM((1, D), F32)],
        compiler_params=pltpu.CompilerParams(dimension_semantics=("arbitrary", "arbitrary"),
                                             vmem_limit_bytes=MLP_VMEM_LIMIT),
        name="rwkv_in",
    )(x, shift0.reshape(B, 1, D), g.reshape(1, D), mu, w_rkv, *lora1)
    return res[0], res[1:1 + n_lora], res[-1].reshape(B, D)


def _resnorm_kernel(x_ref, h_ref, *rest, n_out, project):
    w_ref = None
    if project:
        w_ref, *rest = rest
    gp_ref, *rest = rest
    if n_out:
        gn_ref, xo_ref, no_ref = rest
    else:
        xo_ref, = rest
    tr = x_ref.shape[0]
    halves = 2 if project and tr % 32 == 0 else 1
    bounds = [slice(k * tr // halves, (k + 1) * tr // halves) for k in range(halves)]
    hs = [h_ref[rows, :] for rows in bounds]
    if project:
        hs = [_dot(h.astype(BF16), w_ref[...]) for h in hs]
    for rows, h in zip(bounds, hs):
        x = x_ref[rows, :] + _rms(h, NORM_EPS) * gp_ref[...]
        if n_out:
            y = _rms(x, NORM_EPS)
            for j in range(n_out):
                no_ref[j, rows, :] = (y * gn_ref[j:j + 1, :]).astype(no_ref.dtype)
        xo_ref[rows, :] = x


def _resnorm(x, h, g_post, g_next, w=None):
    M, D = x.shape
    n_out = len(g_next)
    tr = min(512 if w is not None else 256, M)
    row = pl.BlockSpec((tr, D), lambda i: (i, 0))
    vec = pl.BlockSpec((1, D), lambda i: (0, 0))
    in_specs = [row, pl.BlockSpec((tr, h.shape[1]), lambda i: (i, 0))]
    args = [x, h]
    if w is not None:
        in_specs.append(pl.BlockSpec(w.shape, lambda i: (0, 0)))
        args.append(w)
    in_specs.append(vec)
    args.append(g_post.reshape(1, D))
    out_specs = [row]
    out_shape = [jax.ShapeDtypeStruct((M, D), F32)]
    if n_out:
        in_specs.append(pl.BlockSpec((n_out, D), lambda i: (0, 0)))
        args.append(jnp.stack(g_next))
        out_specs.append(pl.BlockSpec((n_out, tr, D), lambda i: (0, i, 0)))
        out_shape.append(jax.ShapeDtypeStruct((n_out, M, D), BF16))
    res = pl.pallas_call(
        functools.partial(_resnorm_kernel, n_out=n_out, project=w is not None),
        grid=(M // tr,),
        in_specs=in_specs,
        out_specs=out_specs,
        out_shape=out_shape,
        compiler_params=_cparams("parallel"),
        name="resnorm",
    )(*args)
    return (res[0], res[1]) if n_out else (res[0], None)


def _norm_kernel(x_ref, g_ref, o_ref):
    o_ref[...] = (_rms(x_ref[...], NORM_EPS) * g_ref[...]).astype(o_ref.dtype)


def _norm(x, g):
    M, D = x.shape
    tr = min(256, M)
    return pl.pallas_call(
        _norm_kernel,
        grid=(M // tr,),
        in_specs=[pl.BlockSpec((tr, D), lambda i: (i, 0)), pl.BlockSpec((1, D), lambda i: (0, 0))],
        out_specs=pl.BlockSpec((tr, D), lambda i: (i, 0)),
        out_shape=jax.ShapeDtypeStruct((M, D), BF16),
        compiler_params=_cparams("parallel"),
        name="norm",
    )(x, g.reshape(1, D))


def _split2(x):
    hi = x.astype(BF16)
    lo = (x - hi.astype(F32)).astype(BF16)
    return hi, lo


def _seg_sum(x, ones_bd):
    hi, lo = _split2(x)
    n = x.shape[0]
    s = _dot(jnp.concatenate([hi, lo], axis=0), ones_bd)
    return s[:n] + s[n:]


def _wkv_kernel(*refs, rows, vres, ng, fused):
    _wkv_body(refs, pl.program_id(2), pl.num_programs(2), rows=rows, vres=vres, ng=ng, fused=fused)


def _wkv_num_inputs(vres, fused):
    return (3 + vres) + (3 + vres) * (2 if fused else 1) + 2


def _wkv_body(refs, c, nc, *, rows, vres, ng, fused, rider=None):
    n_rk = 3 + vres
    nb = 3 + vres
    rk_refs = refs[:n_rk]
    br_refs = refs[n_rk:n_rk + (2 * nb if fused else nb)]
    p_ref, s0_ref, y_ref, so_ref, s_ref = refs[n_rk + len(br_refs):]
    row_head = lax.broadcasted_iota(jnp.int32, (GROUP, GROUP), 0) // RWKV_HEAD
    lane_head = lax.broadcasted_iota(jnp.int32, (GROUP, GROUP), 1) // RWKV_HEAD

    @pl.when(c == 0)
    def _():
        for gi in range(ng):
            s_ref[gi] = jnp.where(row_head == lane_head, s0_ref[gi], 0.0)

    if rider is not None:
        rider[0]()

    gens = []
    hidden = [ref[...] for ref in br_refs[:nb]] if fused else None
    for gi in range(ng):
        lanes = pl.ds(gi * GROUP, GROUP)
        rk = [ref[:, lanes] for ref in rk_refs]
        if fused:
            br = [_dot(h, w2_ref[:, lanes]) for h, w2_ref in zip(hidden, br_refs[nb:])]
        else:
            br = [ref[:, lanes] for ref in br_refs]
        vals = rk[:3] + br[:3] + ([rk[3], br[3]] if vres else []) + [p_ref[:, lanes], s_ref[gi]]
        gens.append(_wkv_group(*vals, rows=rows, vres=vres))
    if rider is not None:
        gens.append(rider[1]())
    outs = [None] * len(gens)
    live = list(range(len(gens)))
    while live:
        for gi in list(live):
            try:
                next(gens[gi])
            except StopIteration as done:
                outs[gi] = done.value
                live.remove(gi)
    for gi, (y, s_new) in enumerate(outs[:ng]):
        y_ref[:, pl.ds(gi * GROUP, GROUP)] = y.astype(y_ref.dtype)
        s_ref[gi] = s_new

    if rider is not None:
        rider[2]()

    @pl.when(c == nc - 1)
    def _():
        for gi in range(ng):
            s = s_ref[gi]
            packed = s
            for h in range(1, HEADS_PER_GROUP):
                packed = jnp.where(row_head == h, pltpu.roll(s, GROUP - h * RWKV_HEAD, 1), packed)
            so_ref[gi] = packed[:, :RWKV_HEAD]


def _wkv_group(r_ref, k_ref, v_ref, wl_ref, al_ref, g_ref, *rest, rows, vres):
    if vres:
        vf_ref, vl_ref, p, s = rest
    else:
        p, s = rest
    L = CHUNK
    W = GROUP
    HG = HEADS_PER_GROUP

    def load(val):
        x = val.astype(F32)
        if rows < L:
            x = jnp.concatenate([x, jnp.zeros((L - rows, W), F32)], axis=0)
        return x

    w0, a0, k_k, k_a = p[0:1], p[1:2], p[2:3], p[3:4]
    lnx_w, lnx_b, rk, v0 = p[4:5], p[5:6], p[6:7], p[7:8]

    lane_head = lax.broadcasted_iota(jnp.int32, (1, W), 1) // RWKV_HEAD
    rr = lax.broadcasted_iota(jnp.int32, (W, W), 0) // RWKV_HEAD
    cc = lax.broadcasted_iota(jnp.int32, (W, W), 1) // RWKV_HEAD
    bd = rr == cc
    ones_bd = jnp.where(bd, 1.0, 0.0).astype(BF16)
    trow = lax.broadcasted_iota(jnp.int32, (L, HG * L), 0)
    tcol = lax.broadcasted_iota(jnp.int32, (L, HG * L), 1) & (L - 1)
    strict = tcol < trow
    incl = tcol <= trow
    rb = lax.broadcasted_iota(jnp.int32, (HG * L, HG * L), 0) // L
    cb = lax.broadcasted_iota(jnp.int32, (HG * L, HG * L), 1) // L
    bd_l = rb == cb

    def ystack(x):
        return jnp.concatenate(
            [jnp.where(lane_head == h, x, 0.0) for h in range(HG)], axis=0).astype(BF16)

    r = load(r_ref)
    k = load(k_ref)
    v = load(v_ref)
    a_sig = jax.nn.sigmoid(load(al_ref) + a0)
    if vres:
        v = v + (load(vf_ref) - v) * jax.nn.sigmoid(load(vl_ref) + v0)
    z = -(load(wl_ref) + w0)
    softplus = jnp.maximum(z, 0.0) + jnp.log(1.0 + jnp.exp(-jnp.abs(z)))
    lw = -jnp.exp(-softplus - 0.5)
    if rows < L:
        trow1 = lax.broadcasted_iota(jnp.int32, (L, W), 0)
        lw = jnp.where(trow1 < rows, lw, 0.0)
    kkr = k * k_k
    nrm2 = _seg_sum(kkr * kkr, ones_bd)
    yield
    nrm = jnp.sqrt(nrm2)
    kk = kkr / jnp.maximum(nrm, 1e-12)
    k2 = k * (1.0 + (a_sig - 1.0) * k_a)
    a = -kk
    b = kk * a_sig

    l1 = lw.astype(BF16)
    l2r = lw - l1.astype(F32)
    l2 = l2r.astype(BF16)
    l3 = (l2r - l2.astype(F32)).astype(BF16)
    tri = (lax.broadcasted_iota(jnp.int32, (L, L), 1)
           <= lax.broadcasted_iota(jnp.int32, (L, L), 0))
    tri = jnp.where(tri, 1.0, 0.0).astype(BF16)
    cs = _dot(tri, jnp.concatenate([l1, l2, l3], axis=1))
    yield
    cum = cs[:, :W] + cs[:, W:2 * W] + cs[:, 2 * W:]
    cl = cum[L - 1:L, :]
    e_c = jnp.exp(cum)
    e_cm = jnp.exp(cum - lw)
    e_nc = jnp.exp(-cum)
    e_lc = jnp.exp(cl - cum)
    g_l = jnp.exp(cl)
    at = a * e_cm
    rt = r * e_c
    bt = b * e_nc
    kt = k2 * e_nc
    bh = b * e_lc
    kh = k2 * e_lc

    x4 = _dot_nt(jnp.concatenate([at, rt], axis=0).astype(BF16),
                 jnp.concatenate([ystack(bt), ystack(kt)], axis=0))
    yield
    n_ab = jnp.where(strict, x4[:L, :HG * L], 0.0)
    m_ak = jnp.where(strict, x4[:L, HG * L:], 0.0)
    m_rb = jnp.where(incl, x4[L:, :HG * L], 0.0)
    m_rk = jnp.where(incl, x4[L:, HG * L:], 0.0)

    def bdiag(x):
        return jnp.where(bd_l, jnp.concatenate([x] * HG, axis=0), 0.0).astype(BF16)

    t_inv = jnp.where(tcol == trow, 1.0, 0.0) + n_ab
    yv = ystack(v)
    makv = _dot(m_ak.astype(BF16), yv)
    pw = _dot(n_ab.astype(BF16), bdiag(n_ab))
    yield
    steps = int(math.log2(L)) - 1
    for it in range(steps):
        rhs = bdiag(pw)
        if it < steps - 1:
            res = _dot(jnp.concatenate([t_inv, pw], axis=0).astype(BF16), rhs)
            yield
            t_inv = t_inv + res[:L]
            pw = res[L:]
        else:
            res = _dot(t_inv.astype(BF16), rhs)
            yield
            t_inv = t_inv + res

    au = _dot(t_inv.astype(BF16), jnp.concatenate([ystack(at), ystack(makv)], axis=1))
    yield
    abar = au[:, :W]
    u0 = au[:, W:]
    y_abar = ystack(abar)
    d1 = _dot(m_rb.astype(BF16), jnp.concatenate([y_abar, ystack(u0)], axis=1))
    d2 = _dot(m_rk.astype(BF16), yv)
    s_bf = s.astype(BF16)
    sa = _dot_nt(s_bf, y_abar)
    yield
    rbar = rt + d1[:, :W]
    o0 = d1[:, W:] + d2
    o = _dot_nt(rbar.astype(BF16), s_bf) + o0
    uv_t = jnp.concatenate([u0, v], axis=0).T
    q_t = _dot(uv_t.astype(BF16), jnp.concatenate([bh, kh], axis=0).astype(BF16))
    s_new = s * g_l + _dot(sa.astype(BF16), ystack(bh)) + jnp.where(bd, q_t, 0.0)
    bonus = _seg_sum(r * k2 * rk, ones_bd) * v
    yield
    inv_n = 1.0 / RWKV_HEAD
    mean = _seg_sum(o, ones_bd) * inv_n
    yield
    dlt = o - mean
    var = _seg_sum(dlt * dlt, ones_bd) * inv_n
    yield
    yn = dlt * lax.rsqrt(var + GN_EPS) * lnx_w + lnx_b
    out = (yn + bonus) * load(g_ref)
    return out[:rows], s_new


def _wkv(rkv, branches, params, s0, *, B, T, rkv_first=None, out_dtype=BF16, decode=None):
    _, M, D = rkv.shape
    fused = isinstance(branches[0], tuple)
    vres = rkv_first is not None
    n_groups = B * D // GROUP
    if s0 is None:
        s0_rep = jnp.zeros((n_groups, GROUP, GROUP), F32)
    else:
        s0_rep = jnp.tile(s0.astype(F32).reshape(n_groups, GROUP, RWKV_HEAD), (1, 1, HEADS_PER_GROUP))
    ng, G, rows, nc = _wkv_grid(D, T)
    mk = (lambda f: f) if decode is None else (lambda f: (lambda b, g, c, pt: f(b, g, c)))

    def plane(pi):
        return pl.BlockSpec((None, rows, ng * GROUP), mk(lambda b, g, c: (pi, b * nc + c, g)))

    tile = pl.BlockSpec((rows, ng * GROUP), mk(lambda b, g, c: (b * nc + c, g)))
    lanes = mk(lambda b, g, c: (0, g))
    state = pl.BlockSpec((ng, GROUP, GROUP), mk(lambda b, g, c: (b * G + g, 0, 0)))
    in_specs = [plane(0), plane(1), plane(2)]
    args = [rkv, rkv, rkv]
    if vres:
        in_specs.append(plane(2))
        args.append(rkv_first)
    if fused:
        in_specs += [pl.BlockSpec((rows, h.shape[1]), mk(lambda b, g, c: (b * nc + c, 0)))
                     for h, _ in branches]
        in_specs += [pl.BlockSpec((w2.shape[0], ng * GROUP), lanes) for _, w2 in branches]
        args += [h for h, _ in branches] + [w2 for _, w2 in branches]
    else:
        in_specs += [tile] * len(branches)
        args += list(branches)
    in_specs += [pl.BlockSpec((8, ng * GROUP), lanes), state]
    args += [params, s0_rep]
    out_specs = [tile, pl.BlockSpec((ng, GROUP, RWKV_HEAD), mk(lambda b, g, c: (b * G + g, 0, 0)))]
    out_shape = [jax.ShapeDtypeStruct((M, D), out_dtype),
                 jax.ShapeDtypeStruct((n_groups, GROUP, RWKV_HEAD), F32)]
    scratch = [pltpu.VMEM((ng, GROUP, GROUP), F32)]
    wkv_kw = dict(rows=rows, vres=vres, ng=ng, fused=fused)
    if decode is None:
        y, s_out = pl.pallas_call(
            functools.partial(_wkv_kernel, **wkv_kw),
            grid=(B, G, nc),
            in_specs=in_specs,
            out_specs=out_specs,
            out_shape=out_shape,
            scratch_shapes=scratch,
            compiler_params=_cparams("parallel", "parallel", "arbitrary"),
            name="wkv7",
        )(*args)
        return y, s_out.reshape(B, D // RWKV_HEAD, RWKV_HEAD, RWKV_HEAD)

    spb, npp = _ride_shape(B, D, T, decode["page_table"])
    step = lambda b, g, c: (b * G + g) * nc + c
    dec = _decode_call_parts(decode, npp, lambda b, g, c, pt: step(b, g, c) // spb,
                             lambda b, g, c, pt: (step(b, g, c) % spb) * npp)
    n_wkv_in, n_dec_in = len(args), len(dec["args"])
    grid_spec = pltpu.PrefetchScalarGridSpec(
        num_scalar_prefetch=1,
        grid=(B, G, nc),
        in_specs=in_specs + dec["in_specs"],
        out_specs=out_specs + [dec["out_spec"]],
        scratch_shapes=scratch + dec["scratch"],
    )
    y, s_out, o = pl.pallas_call(
        functools.partial(_wkv_decode_kernel, n_wkv_in=n_wkv_in, n_dec_in=n_dec_in, G=G, nc=nc,
                          spb=spb, wkv_kw=wkv_kw, dec_kw=dec["kw"]),
        grid_spec=grid_spec,
        out_shape=out_shape + [dec["out_shape"]],
        compiler_params=pltpu.CompilerParams(
            dimension_semantics=("arbitrary", "arbitrary", "arbitrary"),
            vmem_limit_bytes=RIDE_VMEM_LIMIT),
        name="wkv7_decode",
    )(decode["page_table"].reshape(-1), *args, *dec["args"])
    return y, s_out.reshape(B, D // RWKV_HEAD, RWKV_HEAD, RWKV_HEAD), o


def _wkv_grid(D, T):
    ng = max(d for d in range(1, WKV_GROUPS_PER_STEP + 1) if (D // GROUP) % d == 0)
    rows = min(CHUNK, T)
    return ng, D // (ng * GROUP), rows, T // rows


MAX_RIDING_PAGES = 8


def _ride_shape(B, D, T, page_table):
    _, G, _, nc = _wkv_grid(D, T)
    steps = B * G * nc
    Bd, npg = page_table.shape
    if steps % Bd or npg % (steps // Bd) or npg // (steps // Bd) > MAX_RIDING_PAGES:
        return None
    return steps // Bd, npg // (steps // Bd)


def _wkv_decode_kernel(pt_ref, *refs, n_wkv_in, n_dec_in, G, nc, spb, wkv_kw, dec_kw):
    wkv_in = refs[:n_wkv_in]
    dec_in = refs[n_wkv_in:n_wkv_in + n_dec_in]
    y_ref, so_ref, o_ref, s_ref, m_ref, l_ref, acc_ref = refs[n_wkv_in + n_dec_in:]
    c = pl.program_id(2)
    step = (pl.program_id(0) * G + pl.program_id(1)) * nc + c
    rider = _decode_parts(tuple(dec_in) + (o_ref, m_ref, l_ref, acc_ref), step % spb, spb, **dec_kw)
    _wkv_body(tuple(wkv_in) + (y_ref, so_ref, s_ref), c, nc, rider=rider, **wkv_kw)


def _lambda(lp, lam_init):
    return (jnp.exp(jnp.sum(lp[0:1] * lp[1:2], keepdims=True))
            - jnp.exp(jnp.sum(lp[2:3] * lp[3:4], keepdims=True)) + lam_init)


Q_SCALE = ATT_HEAD ** -0.5 * math.log2(math.e)


def _rep(x, n):
    return x if n == LANES else jnp.concatenate([x] * (n // LANES), axis=1)


def _softmax_step(s, vt, m, l, acc, ones=None):
    m_new = jnp.maximum(m, jnp.max(s, axis=-1, keepdims=True))
    alpha = jnp.exp2(m - m_new)
    p = jnp.exp2(s - _rep(m_new, s.shape[1]))
    pb = p.astype(BF16)
    if ones is None:
        l_new = alpha * l + jnp.sum(p, axis=-1, keepdims=True)
    else:
        l_new = alpha * l + _dot(pb, ones)
    acc_new = _rep(alpha, acc.shape[1]) * acc + _dot(pb, vt)
    return m_new, l_new, acc_new


def _attn_prompt_kernel(q_ref, k_ref, v_ref, lp_ref, sg_ref, o_ref, m_ref, l_ref, acc_ref,
                        *, tq, lam_init):
    qi = pl.program_id(2)
    q = q_ref[...]
    qs = (q[:, :ATT_HEAD], q[:, ATT_HEAD:])
    m_ref[...] = jnp.full(m_ref.shape, NEG_BIG, F32)
    l_ref[...] = jnp.zeros(l_ref.shape, F32)
    acc_ref[...] = jnp.zeros(acc_ref.shape, F32)

    def scores(j):
        kt = k_ref[pl.ds(pl.multiple_of(j * tq, tq), tq), :]
        return tuple(_dot_nt(qs[mp], kt[:, mp * ATT_HEAD:(mp + 1) * ATT_HEAD]) for mp in range(2))

    def update(j, s):
        vt = v_ref[pl.ds(pl.multiple_of(j * tq, tq), tq), :]
        for mp in range(2):
            m_ref[mp], l_ref[mp], acc_ref[mp] = _softmax_step(
                s[mp], vt, m_ref[mp], l_ref[mp], acc_ref[mp])

    def body(j, s):
        s_next = scores(j + 1)
        update(j, s)
        return s_next

    s = lax.fori_loop(0, qi, body, scores(0))
    keep = (lax.broadcasted_iota(jnp.int32, (tq, tq), 1)
            <= lax.broadcasted_iota(jnp.int32, (tq, tq), 0))
    update(qi, tuple(jnp.where(keep, x, NEG_BIG) for x in s))
    lam = _lambda(lp_ref[...], lam_init)
    hw = acc_ref.shape[2]
    o = acc_ref[0] / _rep(l_ref[0], hw) - lam * (acc_ref[1] / _rep(l_ref[1], hw))
    o = _rms(o, SUBLN_EPS) * sg_ref[...] * (1.0 - lam_init)
    o_ref[...] = o.astype(o_ref.dtype)


def _attn_prompt(q, k, v, lp, sg, *, B, T, lam_init, tq=512):
    M, Wd = q.shape
    HW = 2 * ATT_HEAD
    H = Wd // HW
    tq = min(tq, T)
    nq = T // tq
    return pl.pallas_call(
        functools.partial(_attn_prompt_kernel, tq=tq, lam_init=lam_init),
        grid=(B, H, nq),
        in_specs=[pl.BlockSpec((tq, HW), lambda b, h, i: (b * nq + i, h)),
                  pl.BlockSpec((T, HW), lambda b, h, i: (b, h)),
                  pl.BlockSpec((T, HW), lambda b, h, i: (b, h)),
                  pl.BlockSpec((4, ATT_HEAD), lambda b, h, i: (0, 0)),
                  pl.BlockSpec((1, HW), lambda b, h, i: (0, 0))],
        out_specs=pl.BlockSpec((tq, HW), lambda b, h, i: (b * nq + i, h)),
        out_shape=jax.ShapeDtypeStruct((M, Wd), BF16),
        scratch_shapes=[pltpu.VMEM((2, tq, LANES), F32), pltpu.VMEM((2, tq, LANES), F32),
                        pltpu.VMEM((2, tq, HW), F32)],
        compiler_params=_cparams("parallel", "parallel", "arbitrary"),
        name="diff_attn_prompt",
    )(q, k, v, lp, sg.reshape(1, HW))


PAGES_PER_STEP = 4


def _decode_parts(refs, p, nsteps, *, heads, tq, npp, new_rows, lam_init):
    q_ref = refs[0]
    k_refs = refs[1:1 + npp]
    v_refs = refs[1 + npp:1 + 2 * npp]
    kn_ref, vn_ref, lp_ref, sg_ref, o_ref, m_ref, l_ref, acc_ref = refs[1 + 2 * npp:]
    HW = 2 * ATT_HEAD
    R = heads * tq

    def queries():
        q = q_ref[...]
        return [jnp.concatenate([q[:, h * HW + mp * ATT_HEAD:h * HW + (mp + 1) * ATT_HEAD]
                                 for h in range(heads)], axis=0).astype(BF16) for mp in range(2)]

    def raw_scores(qm, k0, k1):
        return jnp.concatenate([_dot_nt(qm[0], k0.astype(BF16)), _dot_nt(qm[1], k1.astype(BF16))],
                               axis=0)

    def head_mask(ncol):
        row = lax.broadcasted_iota(jnp.int32, (2 * R, ncol), 0)
        col = lax.broadcasted_iota(jnp.int32, (2 * R, ncol), 1)
        return (col % heads) == ((row // tq) % heads), row, col

    def init():
        @pl.when(p == 0)
        def _():
            m_ref[...] = jnp.full(m_ref.shape, NEG_BIG, F32)
            l_ref[...] = jnp.zeros(l_ref.shape, F32)
            acc_ref[...] = jnp.zeros(acc_ref.shape, F32)

    def stages():
        qm = queries()
        state = (m_ref[...], l_ref[...], acc_ref[...])
        nk = k_refs[0].shape[0] // 2
        same_head, _, _ = head_mask(nk)
        for j in range(npp):
            g = raw_scores(qm, k_refs[j][pl.ds(0, nk, stride=2), :],
                           k_refs[j][pl.ds(1, nk, stride=2), :])
            yield
            state = _softmax_step(jnp.where(same_head, g, NEG_BIG),
                                  v_refs[j][...].astype(BF16), *state)
            yield
        m_ref[...], l_ref[...], acc_ref[...] = state

    def final():
        @pl.when(p == nsteps - 1)
        def _():
            pad_to = max(new_rows, LANES)
            k0 = kn_ref[pl.ds(0, new_rows, stride=2), :]
            k1 = kn_ref[pl.ds(1, new_rows, stride=2), :]
            vn = vn_ref[...]
            if pad_to > new_rows:
                zk = jnp.zeros((pad_to - new_rows, ATT_HEAD), F32)
                k0 = jnp.concatenate([k0, zk], axis=0)
                k1 = jnp.concatenate([k1, zk], axis=0)
                vn = jnp.concatenate([vn, jnp.zeros((pad_to - new_rows, HW), F32)], axis=0)
            same, row, col = head_mask(pad_to)
            valid = same & ((col // heads) <= (row % tq)) & (col < new_rows)
            s = jnp.where(valid, raw_scores(queries(), k0, k1), NEG_BIG)
            m, l, acc = _softmax_step(s, vn.astype(BF16), m_ref[...], l_ref[...], acc_ref[...])
            lam = _lambda(lp_ref[...], lam_init)
            sg = sg_ref[...]
            for h in range(heads):
                r0 = slice(h * tq, (h + 1) * tq)
                r1 = slice(R + h * tq, R + (h + 1) * tq)
                o = acc[r0] / _rep(l[r0], HW) - lam * (acc[r1] / _rep(l[r1], HW))
                o = _rms(o, SUBLN_EPS) * sg * (1.0 - lam_init)
                o_ref[:, h * HW:(h + 1) * HW] = o

    return init, stages, final


def _attn_sample_kernel(pt_ref, *refs, **kw):
    init, stages, final = _decode_parts(refs, pl.program_id(1), pl.num_programs(1), **kw)
    init()
    for _ in stages():
        pass
    final()


def _decode_call_parts(d, npp, seq_of, first_page_of):
    Bd, npg = d["page_table"].shape
    M, Wd = d["q"].shape
    tq = M // Bd
    HW = 2 * ATT_HEAD
    heads = Wd // HW
    krows, vrows = d["cache_k"].shape[1], d["cache_v"].shape[1]
    new_rows = tq * heads
    rows = pl.BlockSpec((tq, Wd), lambda *ids: (seq_of(*ids), 0))
    per_seq = lambda *ids: (seq_of(*ids), 0, 0)
    const = lambda *ids: (0, 0)

    def page_spec(nrows, width, j):
        return pl.BlockSpec(
            (None, nrows, width),
            lambda *ids: (ids[-1][seq_of(*ids) * npg + first_page_of(*ids) + j], 0, 0))

    return dict(
        in_specs=([rows] + [page_spec(krows, ATT_HEAD, j) for j in range(npp)]
                  + [page_spec(vrows, HW, j) for j in range(npp)]
                  + [pl.BlockSpec((None, 2 * new_rows, ATT_HEAD), per_seq),
                     pl.BlockSpec((None, new_rows, HW), per_seq),
                     pl.BlockSpec((4, ATT_HEAD), const),
                     pl.BlockSpec((1, HW), const)]),
        args=[d["q"], *([d["cache_k"]] * npp), *([d["cache_v"]] * npp),
              d["k_new"].reshape(Bd, 2 * new_rows, ATT_HEAD), d["v_new"].reshape(Bd, new_rows, HW),
              d["lp"], d["sg"].reshape(1, HW)],
        out_spec=rows,
        out_shape=jax.ShapeDtypeStruct((M, Wd), F32),
        scratch=[pltpu.VMEM((2 * new_rows, LANES), F32),
                 pltpu.VMEM((2 * new_rows, LANES), F32),
                 pltpu.VMEM((2 * new_rows, HW), F32)],
        kw=dict(heads=heads, tq=tq, npp=npp, new_rows=new_rows, lam_init=d["lam_init"]),
    )


def _attn_sample(d):
    Bd, npg = d["page_table"].shape
    npp = max(n for n in range(1, PAGES_PER_STEP + 1) if npg % n == 0)
    parts = _decode_call_parts(d, npp, lambda b, p, pt: b, lambda b, p, pt: p * npp)
    grid_spec = pltpu.PrefetchScalarGridSpec(
        num_scalar_prefetch=1,
        grid=(Bd, npg // npp),
        in_specs=parts["in_specs"],
        out_specs=parts["out_spec"],
        scratch_shapes=parts["scratch"],
    )
    return pl.pallas_call(
        functools.partial(_attn_sample_kernel, **parts["kw"]),
        grid_spec=grid_spec,
        out_shape=parts["out_shape"],
        compiler_params=_cparams("parallel", "arbitrary"),
        name="diff_attn_sample",
    )(d["page_table"].reshape(-1), *parts["args"])


def _rope_tables(pos, reps):
    half = ATT_HEAD // 2
    inv = jnp.power(ROPE_THETA, -jnp.arange(half, dtype=F32) / half)
    ang = pos.astype(F32)[:, None] * inv[None, :]
    cos = jnp.concatenate([jnp.cos(ang), jnp.cos(ang)], axis=-1)
    sin = jnp.concatenate([-jnp.sin(ang), jnp.sin(ang)], axis=-1)
    return jnp.tile(cos, (reps, 1)), jnp.tile(sin, (reps, 1))


def _lambda_init(layer_idx):
    return 0.8 - 0.6 * math.exp(-0.3 * layer_idx)


def kernel(x_prompt, x_sample, cache_k, cache_v, state_shift, state_wkv, page_table, norm_mix, norm_ffn, rwkv_mu, rwkv_vec, rwkv_wr, rwkv_wk, rwkv_wv, rwkv_wo, rwkv_w1, rwkv_w2, rwkv_a1, rwkv_a2, rwkv_v0, rwkv_v1, rwkv_v2, rwkv_g1, rwkv_g2, rwkv_rk, kv_norm, kv_wk, kv_wv, attn_wq, attn_wo, attn_lambda, attn_subln, ffn_w1, ffn_w2):
    depth = norm_mix.shape[0]
    n_a = rwkv_mu.shape[0]
    D = x_prompt.shape[-1]
    Wd = kv_wk.shape[1]
    n_att_heads = Wd // (2 * ATT_HEAD)
    n_rwkv_heads = D // RWKV_HEAD

    w_rkv = [jnp.stack([rwkv_wr[l], rwkv_wk[l], rwkv_wv[l]]).astype(BF16) for l in range(n_a)]
    w_o = [rwkv_wo[l].astype(BF16) for l in range(n_a)]
    lora_w = [_pad_lora(rwkv_w1[l], rwkv_w2[l]) for l in range(n_a)]
    lora_a = [_pad_lora(rwkv_a1[l], rwkv_a2[l]) for l in range(n_a)]
    lora_g = [_pad_lora(rwkv_g1[l], rwkv_g2[l]) for l in range(n_a)]
    lora_v = [_pad_lora(rwkv_v1[l], rwkv_v2[l]) for l in range(n_a - 1)]
    wkv_params = []
    for l in range(n_a):
        v0 = rwkv_v0[l - 1] if l > 0 else jnp.zeros((D,), F32)
        wkv_params.append(jnp.concatenate(
            [rwkv_vec[l], rwkv_rk[l].reshape(1, D), v0.reshape(1, D)], axis=0))
    w_kv = jnp.stack([kv_wk, kv_wv]).astype(BF16)
    w_q = [attn_wq[j].astype(BF16) for j in range(depth - n_a)]
    w_ao = [attn_wo[j].astype(BF16) for j in range(depth - n_a)]

    def trunk(x, B, T, shift0, wkv0, rope, attend, small):
        act_dtype = F32 if small else BF16
        new_shift, new_wkv = [], []
        rkv_first = None
        k_sh = v_sh = k_att = v_att = None
        xn_next = None
        for l in range(depth):
            if l < n_a:
                loras = [lora_w[l], lora_a[l], lora_g[l]] + ([lora_v[l - 1]] if l > 0 else [])
                if small:
                    mix, last = _prep(x, shift0[l], norm_mix[l, 0], rwkv_mu[l], B=B, T=T,
                                      out_dtype=act_dtype)
                    rkv = _matmul((mix, PL_R), w_rkv[l])
                    branches = [_lora((mix, plane), w1, w2, act=act)
                                for (plane, act), (w1, w2) in zip(LORA_BRANCHES, loras)]
                else:
                    rkv, hidden, last = _rwkv_in(x, shift0[l], norm_mix[l, 0], rwkv_mu[l], w_rkv[l],
                                                 [w1 for w1, _ in loras], B=B, T=T)
                    branches = [(h, w2) for h, (_, w2) in zip(hidden, loras)]
                new_shift.append(last)
                if l == 0:
                    rkv_first = rkv
                wkv_call = ((rkv, branches, wkv_params[l], None if wkv0 is None else wkv0[l]),
                            dict(B=B, T=T, rkv_first=rkv_first if l > 0 else None,
                                 out_dtype=act_dtype))
                if small:
                    branch, s_new = _wkv(*wkv_call[0], **wkv_call[1])
                else:
                    branch, s_new = yield ("wkv", wkv_call)
                new_wkv.append(s_new.astype(state_wkv.dtype))
                w_out = w_o[l]
            else:
                j = l - n_a
                lam_init = _lambda_init(l)
                q = _matmul(xn_next, w_q[j], out_dtypes=(act_dtype,), rope=rope, out_scale=Q_SCALE)
                if small:
                    branch = yield ("attn", dict(q=q, k_new=k_att, v_new=v_att, lp=attn_lambda[j],
                                                 sg=attn_subln[j], lam_init=lam_init))
                else:
                    branch = attend(q, k_att, v_att, attn_lambda[j], attn_subln[j], lam_init)
                w_out = w_ao[j]
            x, nrm = _resnorm(x, branch, norm_mix[l, 1], [norm_ffn[l, 0]], w=w_out)
            g_next = []
            if l == n_a - 1:
                g_next.append(kv_norm)
            if n_a - 1 <= l < depth - 1:
                g_next.append(norm_mix[l + 1, 0])
            if small:
                x, nrm, _ = _mlp((nrm, 0), (ffn_w1, l), (ffn_w2, l), x, norm_ffn[l, 1], g_next)
            else:
                if l == 0:
                    w_ffn = [ffn_w1[0].astype(BF16), ffn_w2[0].astype(BF16)]
                cast = [(ffn_w1, l + 1), (ffn_w2, l + 1)] if l + 1 < depth else []
                x, nrm, w_next = _mlp((nrm, 0), w_ffn[0], w_ffn[1], x, norm_ffn[l, 1], g_next,
                                      cast=cast)
                w_ffn = w_next
            if l == n_a - 1:
                if small:
                    k_sh = k_att = _matmul((nrm, 0), w_kv[0], rope=rope)
                    v_sh = v_att = _matmul((nrm, 0), w_kv[1])
                else:
                    k_sh, k_att = _matmul((nrm, 0), w_kv[0], rope=rope, out_dtypes=(F32, BF16))
                    v_sh, v_att = _matmul((nrm, 0), w_kv[1], out_dtypes=(F32, BF16))
            if g_next:
                xn_next = (nrm, len(g_next) - 1)
        return x, k_sh, v_sh, jnp.stack(new_shift), jnp.stack(new_wkv)

    Bp, Tp, _ = x_prompt.shape
    shift0_p = jnp.zeros((n_a, Bp, D), F32)
    wkv0_p = None
    rope_p = _rope_tables(jnp.arange(Tp, dtype=jnp.int32), 1)

    def attend_p(q, k, v, lp, sg, lam_init):
        return _attn_prompt(q, k, v, lp, sg, B=Bp, T=Tp, lam_init=lam_init)

    prompt = trunk(x_prompt.reshape(Bp * Tp, D), Bp, Tp, shift0_p, wkv0_p, rope_p, attend_p, False)

    Bd, Td, _ = x_sample.shape
    n_pages = page_table.shape[1]
    page = cache_k.shape[1]
    past_len = n_pages * page
    rope_s = _rope_tables(past_len + jnp.arange(Td, dtype=jnp.int32), Bd)
    ck = cache_k.reshape(cache_k.shape[0], page * n_att_heads * 2, ATT_HEAD)
    cv = cache_v.reshape(cache_v.shape[0], page * n_att_heads, 2 * ATT_HEAD)
    sample = trunk(x_sample.reshape(Bd * Td, D), Bd, Td, state_shift, state_wkv, rope_s, None, True)

    def advance(gen, value):
        try:
            return gen.send(value), None
        except StopIteration as done:
            return None, done.value

    can_ride = _ride_shape(Bp, D, Tp, page_table) is not None
    req_p, out_p = advance(prompt, None)
    req_s, out_s = advance(sample, None)
    while req_p is not None or req_s is not None:
        dec = None
        if req_s is not None:
            dec = dict(req_s[1], cache_k=ck, cache_v=cv, page_table=page_table)
        if req_p is not None and dec is not None and can_ride:
            branch, s_new, o = _wkv(*req_p[1][0], **req_p[1][1], decode=dec)
            req_p, out_p = advance(prompt, (branch, s_new))
            req_s, out_s = advance(sample, o)
        elif req_p is not None:
            req_p, out_p = advance(prompt, _wkv(*req_p[1][0], **req_p[1][1]))
        else:
            req_s, out_s = advance(sample, _attn_sample(dec))
    y_p, k_p, v_p, shift_p, wkv_p = out_p
    y_s, k_s, v_s, shift_s, wkv_s = out_s

    H = n_att_heads
    return (y_p.reshape(Bp, Tp, D), y_s.reshape(Bd, Td, D),
            k_p.reshape(Bp, Tp, H, 2, ATT_HEAD), v_p.reshape(Bp, Tp, H, 2 * ATT_HEAD),
            shift_p, wkv_p,
            k_s.reshape(Bd, Td, H, 2, ATT_HEAD), v_s.reshape(Bd, Td, H, 2 * ATT_HEAD),
            shift_s, wkv_s)
```

```python
import functools
import math

import jax
import jax.numpy as jnp
from jax import lax
from jax.experimental import pallas as pl
from jax.experimental.pallas import tpu as pltpu

F32 = jnp.float32
BF16 = jnp.bfloat16

RWKV_HEAD = 64
ATT_HEAD = 128
NORM_EPS = 1e-6
GN_EPS = 64e-5
SUBLN_EPS = 1e-5
ROPE_THETA = 10000.0

LANES = 128
GROUP = 256
HEADS_PER_GROUP = GROUP // RWKV_HEAD
CHUNK = 64
WKV_GROUPS_PER_STEP = 8
VMEM_LIMIT = 48 * 1024 * 1024
MLP_VMEM_LIMIT = 56 * 1024 * 1024
RIDE_VMEM_LIMIT = 60 * 1024 * 1024
NEG_BIG = -1e30


def _cparams(*sem):
    return pltpu.CompilerParams(dimension_semantics=sem, vmem_limit_bytes=VMEM_LIMIT)


def _dot(a, b):
    return jnp.dot(a, b, preferred_element_type=F32)


def _dot_nt(a, b):
    return lax.dot_general(a, b, (((1,), (1,)), ((), ())), preferred_element_type=F32)


def _rms(x, eps):
    return x * lax.rsqrt(jnp.mean(x * x, axis=-1, keepdims=True) + eps)


def _mm_kernel(x_ref, w_ref, *rest, act, rope, out_scale, chunk_rows):
    acc = _dot(x_ref[...].astype(BF16), w_ref[...])
    if act == "tanh":
        acc = jnp.tanh(acc)
    elif act == "sigmoid":
        acc = jax.nn.sigmoid(acc)
    o_refs = rest[2:] if rope else rest
    if rope:
        cos = rest[0][...] * out_scale
        sin = rest[1][...] * out_scale
        tm = acc.shape[0]
        nchunk = acc.shape[1] // LANES
        for c in range(nchunk):
            blk = acc[:, c * LANES:(c + 1) * LANES]
            res = blk * cos + pltpu.roll(blk, LANES // 2, 1) * sin
            for k, o_ref in enumerate(o_refs):
                if chunk_rows and k == 0:
                    o_ref[pl.ds(c, tm, stride=nchunk), :] = res.astype(o_ref.dtype)
                else:
                    o_ref[:, c * LANES:(c + 1) * LANES] = res.astype(o_ref.dtype)
    else:
        if out_scale != 1.0:
            acc = acc * out_scale
        for o_ref in o_refs:
            o_ref[...] = acc.astype(o_ref.dtype)


def _planes(x):
    return x if isinstance(x, tuple) else (x[None], 0)


MAX_WEIGHT_TILE_BYTES = 8 * 1024 * 1024


def _matmul(x, w, *, out_dtypes=(F32,), act=None, rope=None, out_scale=1.0, chunk_rows=False,
            tm=512):
    x, p0 = _planes(x)
    squeeze = w.ndim == 2
    if squeeze:
        w = w[None]
    _, M, K = x.shape
    G, _, N = w.shape
    tm = min(tm, M)
    if rope is not None:
        tm = min(tm, rope[0].shape[0])
        assert rope[0].shape[0] % tm == 0
    tn = N
    while K * tn * w.dtype.itemsize > MAX_WEIGHT_TILE_BYTES and tn % (2 * LANES) == 0:
        tn //= 2
    assert M % tm == 0 and N % tn == 0
    in_specs = [pl.BlockSpec((None, tm, K), lambda g, i, j: (p0 + g, i, 0)),
                pl.BlockSpec((None, K, tn), lambda g, i, j: (g, 0, j))]
    args = [x, w]
    if rope is not None:
        cos, sin = rope
        nblk = cos.shape[0] // tm
        spec = pl.BlockSpec((tm, LANES), lambda g, i, j: (i % nblk, 0))
        in_specs += [spec, spec]
        args += [cos, sin]
    out_specs = [pl.BlockSpec((None, tm, tn), lambda g, i, j: (g, i, j)) for _ in out_dtypes]
    out_shape = [jax.ShapeDtypeStruct((G, M, N), dt) for dt in out_dtypes]
    if chunk_rows:
        assert rope is not None and tn == N and G == 1
        nchunk = N // LANES
        out_specs[0] = pl.BlockSpec((None, tm * nchunk, LANES), lambda g, i, j: (g, i, 0))
        out_shape[0] = jax.ShapeDtypeStruct((G, M * nchunk, LANES), out_dtypes[0])
    outs = pl.pallas_call(
        functools.partial(_mm_kernel, act=act, rope=rope is not None, out_scale=out_scale,
                          chunk_rows=chunk_rows),
        grid=(G, M // tm, N // tn),
        in_specs=in_specs,
        out_specs=out_specs,
        out_shape=out_shape,
        compiler_params=_cparams("parallel", "parallel", "arbitrary"),
        name="matmul",
    )(*args)
    outs = [o[0] if squeeze else o for o in outs]
    return outs[0] if len(outs) == 1 else outs


def _lora_kernel(x_ref, w1_ref, w2_ref, o_ref, *, act):
    h = _dot(x_ref[...].astype(BF16), w1_ref[...])
    if act == "tanh":
        h = jnp.tanh(h)
    elif act == "sigmoid":
        h = jax.nn.sigmoid(h)
    o_ref[...] = _dot(h.astype(BF16), w2_ref[...])


def _lora(x, w1, w2, *, act=None, tm=512):
    x, p0 = _planes(x)
    _, M, K = x.shape
    R = w1.shape[1]
    N = w2.shape[1]
    tm = min(tm, M)
    return pl.pallas_call(
        functools.partial(_lora_kernel, act=act),
        grid=(M // tm,),
        in_specs=[pl.BlockSpec((None, tm, K), lambda i: (p0, i, 0)),
                  pl.BlockSpec((K, R), lambda i: (0, 0)),
                  pl.BlockSpec((R, N), lambda i: (0, 0))],
        out_specs=pl.BlockSpec((tm, N), lambda i: (i, 0)),
        out_shape=jax.ShapeDtypeStruct((M, N), F32),
        compiler_params=_cparams("parallel"),
        name="lora",
    )(x, w1, w2)


def _pad_lora(w1, w2):
    r = w1.shape[1]
    rp = -(-r // LANES) * LANES
    return (jnp.pad(w1, ((0, 0), (0, rp - r))).astype(BF16),
            jnp.pad(w2, ((0, rp - r), (0, 0))).astype(BF16))


def _mlp_kernel(x_ref, w1_ref, w2_ref, xr_ref, gp_ref, *rest, n_out, nf):
    if n_out:
        gn_ref, xo_ref, no_ref = rest
    else:
        xo_ref, = rest
    f = pl.program_id(1)

    @pl.when(f == 0)
    def _():
        xo_ref[...] = jnp.zeros(xo_ref.shape, F32)

    h = jnp.maximum(_dot(x_ref[...], w1_ref[...]), 0.0)
    xo_ref[...] += _dot((h * h).astype(BF16), w2_ref[...])

    @pl.when(f == nf - 1)
    def _():
        x = xr_ref[...] + _rms(xo_ref[...], NORM_EPS) * gp_ref[...]
        xo_ref[...] = x
        if n_out:
            y = _rms(x, NORM_EPS)
            for j in range(n_out):
                no_ref[j] = (y * gn_ref[j:j + 1, :]).astype(no_ref.dtype)


def _mlp(x, w1, w2, x_res, g_post, g_next, *, tm=512, tf=1024):
    x, p0 = _planes(x)
    w1, l1 = _planes(w1)
    w2, l2 = _planes(w2)
    _, M, D = x.shape
    F = w1.shape[2]
    tm = min(tm, M)
    tf = min(tf, F)
    n_out = len(g_next)
    nf = F // tf
    row = pl.BlockSpec((tm, D), lambda i, f: (i, 0))
    vec = pl.BlockSpec((1, D), lambda i, f: (0, 0))
    in_specs = [pl.BlockSpec((None, tm, D), lambda i, f: (p0, i, 0)),
                pl.BlockSpec((None, D, tf), lambda i, f: (l1, 0, f)),
                pl.BlockSpec((None, tf, D), lambda i, f: (l2, f, 0)),
                pl.BlockSpec((tm, D), lambda i, f: (i, 0), pipeline_mode=pl.Buffered(1)),
                vec]
    args = [x, w1, w2, x_res, g_post.reshape(1, D)]
    out_specs = [row]
    out_shape = [jax.ShapeDtypeStruct((M, D), F32)]
    if n_out:
        in_specs.append(pl.BlockSpec((n_out, D), lambda i, f: (0, 0)))
        args.append(jnp.stack(g_next))
        out_specs.append(pl.BlockSpec((n_out, tm, D), lambda i, f: (0, i, 0)))
        out_shape.append(jax.ShapeDtypeStruct((n_out, M, D), BF16))
    res = pl.pallas_call(
        functools.partial(_mlp_kernel, n_out=n_out, nf=nf),
        grid=(M // tm, nf),
        in_specs=in_specs,
        out_specs=out_specs,
        out_shape=out_shape,
        compiler_params=pltpu.CompilerParams(dimension_semantics=("parallel", "arbitrary"),
                                             vmem_limit_bytes=MLP_VMEM_LIMIT),
        name="mlp",
    )(*args)
    return (res[0], res[1]) if n_out else (res[0], None)


MIX_ORDER = (0, 2, 3, 1, 4, 5)
PL_R, PL_K, PL_V, PL_W, PL_A, PL_G = range(6)


def _prep_kernel(x_ref, sh_ref, g_ref, mu_ref, mix_ref, last_ref, carry_ref):
    i = pl.program_id(1)
    xn = _rms(x_ref[...], NORM_EPS) * g_ref[...]
    tr = xn.shape[0]

    @pl.when(i == 0)
    def _():
        carry_ref[...] = sh_ref[...]

    row = lax.broadcasted_iota(jnp.int32, xn.shape, 0)
    xprev = jnp.where(row == 0, carry_ref[...], pltpu.roll(xn, 1, 0))
    xx = xprev - xn
    for plane, m in enumerate(MIX_ORDER):
        mix_ref[plane] = (xn + xx * mu_ref[m:m + 1, :]).astype(mix_ref.dtype)
    last = xn[tr - 1:tr, :]
    carry_ref[...] = last
    last_ref[...] = last


def _prep(x, shift0, g, mu, *, B, T, out_dtype):
    M, D = x.shape
    tr = min(256, T)
    nt = T // tr
    mix, last = pl.pallas_call(
        _prep_kernel,
        grid=(B, nt),
        in_specs=[pl.BlockSpec((tr, D), lambda b, i: (b * nt + i, 0)),
                  pl.BlockSpec((None, 1, D), lambda b, i: (b, 0, 0)),
                  pl.BlockSpec((1, D), lambda b, i: (0, 0)),
                  pl.BlockSpec((6, D), lambda b, i: (0, 0))],
        out_specs=[pl.BlockSpec((6, tr, D), lambda b, i: (0, b * nt + i, 0)),
                   pl.BlockSpec((None, 1, D), lambda b, i: (b, 0, 0))],
        out_shape=[jax.ShapeDtypeStruct((6, M, D), out_dtype),
                   jax.ShapeDtypeStruct((B, 1, D), F32)],
        scratch_shapes=[pltpu.VMEM((1, D), F32)],
        compiler_params=_cparams("arbitrary", "arbitrary"),
        name="rwkv_prep",
    )(x, shift0.reshape(B, 1, D), g.reshape(1, D), mu)
    return mix, last.reshape(B, D)


N_RKV = 3
MIX_SUB_ROWS = 16


def _rwkv_in_kernel(x_ref, sh_ref, g_ref, mu_ref, w_ref, *rest, nt, n_lora):
    l1_refs = rest[:n_lora]
    rkv_ref = rest[n_lora]
    h_refs = rest[n_lora + 1:2 * n_lora + 1]
    last_ref, mix_ref, carry_ref = rest[2 * n_lora + 1:]
    i = pl.program_id(0)
    g = pl.program_id(1)

    @pl.when(g == 0)
    def _():
        tr = x_ref.shape[0]
        sub = min(MIX_SUB_ROWS, tr)
        row = lax.broadcasted_iota(jnp.int32, (sub, x_ref.shape[1]), 0)

        def chunk(ci, prev_row):
            rows = pl.ds(pl.multiple_of(ci * sub, sub), sub)
            xn = _rms(x_ref[rows, :], NORM_EPS) * g_ref[...]
            xx = jnp.where(row == 0, prev_row, pltpu.roll(xn, 1, 0)) - xn
            for plane, m in enumerate(MIX_ORDER):
                mix_ref[plane, rows, :] = (xn + xx * mu_ref[m:m + 1, :]).astype(mix_ref.dtype)
            return xn[sub - 1:sub, :]

        first = jnp.where(i % nt == 0, sh_ref[...], carry_ref[...])
        last = lax.fori_loop(0, tr // sub, chunk, first)
        carry_ref[...] = last
        last_ref[...] = last

    @pl.when(g < N_RKV)
    def _():
        rkv_ref[...] = _dot(mix_ref[g], w_ref[...])

    @pl.when(g == N_RKV)
    def _():
        for (plane, act), l1_ref, h_ref in zip(LORA_BRANCHES, l1_refs, h_refs):
            h = _dot(mix_ref[plane], l1_ref[...])
            if act == "tanh":
                h = jnp.tanh(h)
            elif act == "sigmoid":
                h = jax.nn.sigmoid(h)
            h_ref[...] = h.astype(h_ref.dtype)


LORA_BRANCHES = ((PL_W, "tanh"), (PL_A, None), (PL_G, "sigmoid"), (PL_V, None))


def _rwkv_in(x, shift0, g, mu, w_rkv, lora1, *, B, T, tm=512):
    M, D = x.shape
    tm = min(tm, T)
    nt = T // tm
    n_lora = len(lora1)
    in_specs = [pl.BlockSpec((tm, D), lambda i, g: (i, 0)),
                pl.BlockSpec((None, 1, D), lambda i, g: (i // nt, 0, 0)),
                pl.BlockSpec((1, D), lambda i, g: (0, 0)),
                pl.BlockSpec((6, D), lambda i, g: (0, 0)),
                pl.BlockSpec((None, D, D), lambda i, g: (jnp.minimum(g, N_RKV - 1), 0, 0))]
    in_specs += [pl.BlockSpec(w.shape, lambda i, g: (0, 0), pipeline_mode=pl.Buffered(1))
                 for w in lora1]
    out_specs = [pl.BlockSpec((None, tm, D), lambda i, g: (jnp.minimum(g, N_RKV - 1), i, 0))]
    out_specs += [pl.BlockSpec((tm, w.shape[1]), lambda i, g: (i, 0)) for w in lora1]
    out_specs.append(pl.BlockSpec((None, 1, D), lambda i, g: (i // nt, 0, 0)))
    out_shape = [jax.ShapeDtypeStruct((N_RKV, M, D), F32)]
    out_shape += [jax.ShapeDtypeStruct((M, w.shape[1]), BF16) for w in lora1]
    out_shape.append(jax.ShapeDtypeStruct((B, 1, D), F32))
    res = pl.pallas_call(
        functools.partial(_rwkv_in_kernel, nt=nt, n_lora=n_lora),
        grid=(M // tm, N_RKV + 1),
        in_specs=in_specs,
        out_specs=out_specs,
        out_shape=out_shape,
        scratch_shapes=[pltpu.VMEM((6, tm, D), BF16), pltpu.VMEM((1, D), F32)],
        compiler_params=pltpu.CompilerParams(dimension_semantics=("arbitrary", "arbitrary"),
                                             vmem_limit_bytes=MLP_VMEM_LIMIT),
        name="rwkv_in",
    )(x, shift0.reshape(B, 1, D), g.reshape(1, D), mu, w_rkv, *lora1)
    return res[0], res[1:1 + n_lora], res[-1].reshape(B, D)


def _resnorm_kernel(x_ref, h_ref, *rest, n_out, project):
    w_ref = None
    if project:
        w_ref, *rest = rest
    gp_ref, *rest = rest
    if n_out:
        gn_ref, xo_ref, no_ref = rest
    else:
        xo_ref, = rest
    tr = x_ref.shape[0]
    halves = 2 if project and tr % 32 == 0 else 1
    bounds = [slice(k * tr // halves, (k + 1) * tr // halves) for k in range(halves)]
    hs = [h_ref[rows, :] for rows in bounds]
    if project:
        hs = [_dot(h.astype(BF16), w_ref[...]) for h in hs]
    for rows, h in zip(bounds, hs):
        x = x_ref[rows, :] + _rms(h, NORM_EPS) * gp_ref[...]
        if n_out:
            y = _rms(x, NORM_EPS)
            for j in range(n_out):
                no_ref[j, rows, :] = (y * gn_ref[j:j + 1, :]).astype(no_ref.dtype)
        xo_ref[rows, :] = x


def _resnorm(x, h, g_post, g_next, w=None):
    M, D = x.shape
    n_out = len(g_next)
    tr = min(512 if w is not None else 256, M)
    row = pl.BlockSpec((tr, D), lambda i: (i, 0))
    vec = pl.BlockSpec((1, D), lambda i: (0, 0))
    in_specs = [row, pl.BlockSpec((tr, h.shape[1]), lambda i: (i, 0))]
    args = [x, h]
    if w is not None:
        in_specs.append(pl.BlockSpec(w.shape, lambda i: (0, 0)))
        args.append(w)
    in_specs.append(vec)
    args.append(g_post.reshape(1, D))
    out_specs = [row]
    out_shape = [jax.ShapeDtypeStruct((M, D), F32)]
    if n_out:
        in_specs.append(pl.BlockSpec((n_out, D), lambda i: (0, 0)))
        args.append(jnp.stack(g_next))
        out_specs.append(pl.BlockSpec((n_out, tr, D), lambda i: (0, i, 0)))
        out_shape.append(jax.ShapeDtypeStruct((n_out, M, D), BF16))
    res = pl.pallas_call(
        functools.partial(_resnorm_kernel, n_out=n_out, project=w is not None),
        grid=(M // tr,),
        in_specs=in_specs,
        out_specs=out_specs,
        out_shape=out_shape,
        compiler_params=_cparams("parallel"),
        name="resnorm",
    )(*args)
    return (res[0], res[1]) if n_out else (res[0], None)


def _norm_kernel(x_ref, g_ref, o_ref):
    o_ref[...] = (_rms(x_ref[...], NORM_EPS) * g_ref[...]).astype(o_ref.dtype)


def _norm(x, g):
    M, D = x.shape
    tr = min(256, M)
    return pl.pallas_call(
        _norm_kernel,
        grid=(M // tr,),
        in_specs=[pl.BlockSpec((tr, D), lambda i: (i, 0)), pl.BlockSpec((1, D), lambda i: (0, 0))],
        out_specs=pl.BlockSpec((tr, D), lambda i: (i, 0)),
        out_shape=jax.ShapeDtypeStruct((M, D), BF16),
        compiler_params=_cparams("parallel"),
        name="norm",
    )(x, g.reshape(1, D))


def _split2(x):
    hi = x.astype(BF16)
    lo = (x - hi.astype(F32)).astype(BF16)
    return hi, lo


def _seg_sum(x, ones_bd):
    hi, lo = _split2(x)
    n = x.shape[0]
    s = _dot(jnp.concatenate([hi, lo], axis=0), ones_bd)
    return s[:n] + s[n:]


def _wkv_kernel(*refs, rows, vres, ng, fused):
    _wkv_body(refs, pl.program_id(2), pl.num_programs(2), rows=rows, vres=vres, ng=ng, fused=fused)


def _wkv_num_inputs(vres, fused):
    return (3 + vres) + (3 + vres) * (2 if fused else 1) + 2


def _wkv_body(refs, c, nc, *, rows, vres, ng, fused, rider=None):
    n_rk = 3 + vres
    nb = 3 + vres
    rk_refs = refs[:n_rk]
    br_refs = refs[n_rk:n_rk + (2 * nb if fused else nb)]
    p_ref, s0_ref, y_ref, so_ref, s_ref = refs[n_rk + len(br_refs):]
    row_head = lax.broadcasted_iota(jnp.int32, (GROUP, GROUP), 0) // RWKV_HEAD
    lane_head = lax.broadcasted_iota(jnp.int32, (GROUP, GROUP), 1) // RWKV_HEAD

    @pl.when(c == 0)
    def _():
        for gi in range(ng):
            s_ref[gi] = jnp.where(row_head == lane_head, s0_ref[gi], 0.0)

    if rider is not None:
        rider[0]()

    gens = []
    hidden = [ref[...] for ref in br_refs[:nb]] if fused else None
    for gi in range(ng):
        lanes = pl.ds(gi * GROUP, GROUP)
        rk = [ref[:, lanes] for ref in rk_refs]
        if fused:
            br = [_dot(h, w2_ref[:, lanes]) for h, w2_ref in zip(hidden, br_refs[nb:])]
        else:
            br = [ref[:, lanes] for ref in br_refs]
        vals = rk[:3] + br[:3] + ([rk[3], br[3]] if vres else []) + [p_ref[:, lanes], s_ref[gi]]
        gens.append(_wkv_group(*vals, rows=rows, vres=vres))
    if rider is not None:
        gens.append(rider[1]())
    outs = [None] * len(gens)
    live = list(range(len(gens)))
    while live:
        for gi in list(live):
            try:
                next(gens[gi])
            except StopIteration as done:
                outs[gi] = done.value
                live.remove(gi)
    for gi, (y, s_new) in enumerate(outs[:ng]):
        y_ref[:, pl.ds(gi * GROUP, GROUP)] = y.astype(y_ref.dtype)
        s_ref[gi] = s_new

    if rider is not None:
        rider[2]()

    @pl.when(c == nc - 1)
    def _():
        for gi in range(ng):
            s = s_ref[gi]
            packed = s
            for h in range(1, HEADS_PER_GROUP):
                packed = jnp.where(row_head == h, pltpu.roll(s, GROUP - h * RWKV_HEAD, 1), packed)
            so_ref[gi] = packed[:, :RWKV_HEAD]


def _wkv_group(r_ref, k_ref, v_ref, wl_ref, al_ref, g_ref, *rest, rows, vres):
    if vres:
        vf_ref, vl_ref, p, s = rest
    else:
        p, s = rest
    L = CHUNK
    W = GROUP
    HG = HEADS_PER_GROUP

    def load(val):
        x = val.astype(F32)
        if rows < L:
            x = jnp.concatenate([x, jnp.zeros((L - rows, W), F32)], axis=0)
        return x

    w0, a0, k_k, k_a = p[0:1], p[1:2], p[2:3], p[3:4]
    lnx_w, lnx_b, rk, v0 = p[4:5], p[5:6], p[6:7], p[7:8]

    lane_head = lax.broadcasted_iota(jnp.int32, (1, W), 1) // RWKV_HEAD
    rr = lax.broadcasted_iota(jnp.int32, (W, W), 0) // RWKV_HEAD
    cc = lax.broadcasted_iota(jnp.int32, (W, W), 1) // RWKV_HEAD
    bd = rr == cc
    ones_bd = jnp.where(bd, 1.0, 0.0).astype(BF16)
    trow = lax.broadcasted_iota(jnp.int32, (L, HG * L), 0)
    tcol = lax.broadcasted_iota(jnp.int32, (L, HG * L), 1) & (L - 1)
    strict = tcol < trow
    incl = tcol <= trow
    rb = lax.broadcasted_iota(jnp.int32, (HG * L, HG * L), 0) // L
    cb = lax.broadcasted_iota(jnp.int32, (HG * L, HG * L), 1) // L
    bd_l = rb == cb

    def ystack(x):
        return jnp.concatenate(
            [jnp.where(lane_head == h, x, 0.0) for h in range(HG)], axis=0).astype(BF16)

    r = load(r_ref)
    k = load(k_ref)
    v = load(v_ref)
    a_sig = jax.nn.sigmoid(load(al_ref) + a0)
    if vres:
        v = v + (load(vf_ref) - v) * jax.nn.sigmoid(load(vl_ref) + v0)
    z = -(load(wl_ref) + w0)
    softplus = jnp.maximum(z, 0.0) + jnp.log(1.0 + jnp.exp(-jnp.abs(z)))
    lw = -jnp.exp(-softplus - 0.5)
    if rows < L:
        trow1 = lax.broadcasted_iota(jnp.int32, (L, W), 0)
        lw = jnp.where(trow1 < rows, lw, 0.0)
    kkr = k * k_k
    nrm2 = _seg_sum(kkr * kkr, ones_bd)
    yield
    nrm = jnp.sqrt(nrm2)
    kk = kkr / jnp.maximum(nrm, 1e-12)
    k2 = k * (1.0 + (a_sig - 1.0) * k_a)
    a = -kk
    b = kk * a_sig

    l1 = lw.astype(BF16)
    l2r = lw - l1.astype(F32)
    l2 = l2r.astype(BF16)
    l3 = (l2r - l2.astype(F32)).astype(BF16)
    tri = (lax.broadcasted_iota(jnp.int32, (L, L), 1)
           <= lax.broadcasted_iota(jnp.int32, (L, L), 0))
    tri = jnp.where(tri, 1.0, 0.0).astype(BF16)
    cs = _dot(tri, jnp.concatenate([l1, l2, l3], axis=1))
    yield
    cum = cs[:, :W] + cs[:, W:2 * W] + cs[:, 2 * W:]
    cl = cum[L - 1:L, :]
    e_c = jnp.exp(cum)
    e_cm = jnp.exp(cum - lw)
    e_nc = jnp.exp(-cum)
    e_lc = jnp.exp(cl - cum)
    g_l = jnp.exp(cl)
    at = a * e_cm
    rt = r * e_c
    bt = b * e_nc
    kt = k2 * e_nc
    bh = b * e_lc
    kh = k2 * e_lc

    x4 = _dot_nt(jnp.concatenate([at, rt], axis=0).astype(BF16),
                 jnp.concatenate([ystack(bt), ystack(kt)], axis=0))
    yield
    n_ab = jnp.where(strict, x4[:L, :HG * L], 0.0)
    m_ak = jnp.where(strict, x4[:L, HG * L:], 0.0)
    m_rb = jnp.where(incl, x4[L:, :HG * L], 0.0)
    m_rk = jnp.where(incl, x4[L:, HG * L:], 0.0)

    def bdiag(x):
        return jnp.where(bd_l, jnp.concatenate([x] * HG, axis=0), 0.0).astype(BF16)

    t_inv = jnp.where(tcol == trow, 1.0, 0.0) + n_ab
    yv = ystack(v)
    makv = _dot(m_ak.astype(BF16), yv)
    pw = _dot(n_ab.astype(BF16), bdiag(n_ab))
    yield
    steps = int(math.log2(L)) - 1
    for it in range(steps):
        rhs = bdiag(pw)
        if it < steps - 1:
            res = _dot(jnp.concatenate([t_inv, pw], axis=0).astype(BF16), rhs)
            yield
            t_inv = t_inv + res[:L]
            pw = res[L:]
        else:
            res = _dot(t_inv.astype(BF16), rhs)
            yield
            t_inv = t_inv + res

    au = _dot(t_inv.astype(BF16), jnp.concatenate([ystack(at), ystack(makv)], axis=1))
    yield
    abar = au[:, :W]
    u0 = au[:, W:]
    y_abar = ystack(abar)
    d1 = _dot(m_rb.astype(BF16), jnp.concatenate([y_abar, ystack(u0)], axis=1))
    d2 = _dot(m_rk.astype(BF16), yv)
    s_bf = s.astype(BF16)
    sa = _dot_nt(s_bf, y_abar)
    yield
    rbar = rt + d1[:, :W]
    o0 = d1[:, W:] + d2
    o = _dot_nt(rbar.astype(BF16), s_bf) + o0
    uv_t = jnp.concatenate([u0, v], axis=0).T
    q_t = _dot(uv_t.astype(BF16), jnp.concatenate([bh, kh], axis=0).astype(BF16))
    s_new = s * g_l + _dot(sa.astype(BF16), ystack(bh)) + jnp.where(bd, q_t, 0.0)
    bonus = _seg_sum(r * k2 * rk, ones_bd) * v
    yield
    inv_n = 1.0 / RWKV_HEAD
    mean = _seg_sum(o, ones_bd) * inv_n
    yield
    dlt = o - mean
    var = _seg_sum(dlt * dlt, ones_bd) * inv_n
    yield
    yn = dlt * lax.rsqrt(var + GN_EPS) * lnx_w + lnx_b
    out = (yn + bonus) * load(g_ref)
    return out[:rows], s_new


def _wkv(rkv, branches, params, s0, *, B, T, rkv_first=None, out_dtype=BF16, decode=None):
    _, M, D = rkv.shape
    fused = isinstance(branches[0], tuple)
    vres = rkv_first is not None
    n_groups = B * D // GROUP
    if s0 is None:
        s0_rep = jnp.zeros((n_groups, GROUP, GROUP), F32)
    else:
        s0_rep = jnp.tile(s0.astype(F32).reshape(n_groups, GROUP, RWKV_HEAD), (1, 1, HEADS_PER_GROUP))
    ng, G, rows, nc = _wkv_grid(D, T)
    mk = (lambda f: f) if decode is None else (lambda f: (lambda b, g, c, pt: f(b, g, c)))

    def plane(pi):
        return pl.BlockSpec((None, rows, ng * GROUP), mk(lambda b, g, c: (pi, b * nc + c, g)))

    tile = pl.BlockSpec((rows, ng * GROUP), mk(lambda b, g, c: (b * nc + c, g)))
    lanes = mk(lambda b, g, c: (0, g))
    state = pl.BlockSpec((ng, GROUP, GROUP), mk(lambda b, g, c: (b * G + g, 0, 0)))
    in_specs = [plane(0), plane(1), plane(2)]
    args = [rkv, rkv, rkv]
    if vres:
        in_specs.append(plane(2))
        args.append(rkv_first)
    if fused:
        in_specs += [pl.BlockSpec((rows, h.shape[1]), mk(lambda b, g, c: (b * nc + c, 0)))
                     for h, _ in branches]
        in_specs += [pl.BlockSpec((w2.shape[0], ng * GROUP), lanes) for _, w2 in branches]
        args += [h for h, _ in branches] + [w2 for _, w2 in branches]
    else:
        in_specs += [tile] * len(branches)
        args += list(branches)
    in_specs += [pl.BlockSpec((8, ng * GROUP), lanes), state]
    args += [params, s0_rep]
    out_specs = [tile, pl.BlockSpec((ng, GROUP, RWKV_HEAD), mk(lambda b, g, c: (b * G + g, 0, 0)))]
    out_shape = [jax.ShapeDtypeStruct((M, D), out_dtype),
                 jax.ShapeDtypeStruct((n_groups, GROUP, RWKV_HEAD), F32)]
    scratch = [pltpu.VMEM((ng, GROUP, GROUP), F32)]
    wkv_kw = dict(rows=rows, vres=vres, ng=ng, fused=fused)
    if decode is None:
        y, s_out = pl.pallas_call(
            functools.partial(_wkv_kernel, **wkv_kw),
            grid=(B, G, nc),
            in_specs=in_specs,
            out_specs=out_specs,
            out_shape=out_shape,
            scratch_shapes=scratch,
            compiler_params=_cparams("parallel", "parallel", "arbitrary"),
            name="wkv7",
        )(*args)
        return y, s_out.reshape(B, D // RWKV_HEAD, RWKV_HEAD, RWKV_HEAD)

    spb, npp = _ride_shape(B, D, T, decode["page_table"])
    step = lambda b, g, c: (b * G + g) * nc + c
    dec = _decode_call_parts(decode, npp, lambda b, g, c, pt: step(b, g, c) // spb,
                             lambda b, g, c, pt: (step(b, g, c) % spb) * npp)
    n_wkv_in, n_dec_in = len(args), len(dec["args"])
    grid_spec = pltpu.PrefetchScalarGridSpec(
        num_scalar_prefetch=1,
        grid=(B, G, nc),
        in_specs=in_specs + dec["in_specs"],
        out_specs=out_specs + [dec["out_spec"]],
        scratch_shapes=scratch + dec["scratch"],
    )
    y, s_out, o = pl.pallas_call(
        functools.partial(_wkv_decode_kernel, n_wkv_in=n_wkv_in, n_dec_in=n_dec_in, G=G, nc=nc,
                          spb=spb, wkv_kw=wkv_kw, dec_kw=dec["kw"]),
        grid_spec=grid_spec,
        out_shape=out_shape + [dec["out_shape"]],
        compiler_params=pltpu.CompilerParams(
            dimension_semantics=("arbitrary", "arbitrary", "arbitrary"),
            vmem_limit_bytes=RIDE_VMEM_LIMIT),
        name="wkv7_decode",
    )(decode["page_table"].reshape(-1), *args, *dec["args"])
    return y, s_out.reshape(B, D // RWKV_HEAD, RWKV_HEAD, RWKV_HEAD), o


def _wkv_grid(D, T):
    ng = max(d for d in range(1, WKV_GROUPS_PER_STEP + 1) if (D // GROUP) % d == 0)
    rows = min(CHUNK, T)
    return ng, D // (ng * GROUP), rows, T // rows


MAX_RIDING_PAGES = 8


def _ride_shape(B, D, T, page_table):
    _, G, _, nc = _wkv_grid(D, T)
    steps = B * G * nc
    Bd, npg = page_table.shape
    if steps % Bd or npg % (steps // Bd) or npg // (steps // Bd) > MAX_RIDING_PAGES:
        return None
    return steps // Bd, npg // (steps // Bd)


def _wkv_decode_kernel(pt_ref, *refs, n_wkv_in, n_dec_in, G, nc, spb, wkv_kw, dec_kw):
    wkv_in = refs[:n_wkv_in]
    dec_in = refs[n_wkv_in:n_wkv_in + n_dec_in]
    y_ref, so_ref, o_ref, s_ref, m_ref, l_ref, acc_ref = refs[n_wkv_in + n_dec_in:]
    c = pl.program_id(2)
    step = (pl.program_id(0) * G + pl.program_id(1)) * nc + c
    rider = _decode_parts(tuple(dec_in) + (o_ref, m_ref, l_ref, acc_ref), step % spb, spb, **dec_kw)
    _wkv_body(tuple(wkv_in) + (y_ref, so_ref, s_ref), c, nc, rider=rider, **wkv_kw)


def _lambda(lp, lam_init):
    return (jnp.exp(jnp.sum(lp[0:1] * lp[1:2], keepdims=True))
            - jnp.exp(jnp.sum(lp[2:3] * lp[3:4], keepdims=True)) + lam_init)


Q_SCALE = ATT_HEAD ** -0.5 * math.log2(math.e)


def _rep(x, n):
    return x if n == LANES else jnp.concatenate([x] * (n // LANES), axis=1)


def _softmax_step(s, vt, m, l, acc, ones=None):
    m_new = jnp.maximum(m, jnp.max(s, axis=-1, keepdims=True))
    alpha = jnp.exp2(m - m_new)
    p = jnp.exp2(s - _rep(m_new, s.shape[1]))
    pb = p.astype(BF16)
    if ones is None:
        l_new = alpha * l + jnp.sum(p, axis=-1, keepdims=True)
    else:
        l_new = alpha * l + _dot(pb, ones)
    acc_new = _rep(alpha, acc.shape[1]) * acc + _dot(pb, vt)
    return m_new, l_new, acc_new


def _attn_prompt_kernel(q_ref, k_ref, v_ref, lp_ref, sg_ref, o_ref, m_ref, l_ref, acc_ref,
                        *, tq, lam_init):
    qi = pl.program_id(2)
    q = q_ref[...]
    qs = (q[:, :ATT_HEAD], q[:, ATT_HEAD:])
    m_ref[...] = jnp.full(m_ref.shape, NEG_BIG, F32)
    l_ref[...] = jnp.zeros(l_ref.shape, F32)
    acc_ref[...] = jnp.zeros(acc_ref.shape, F32)

    def scores(j):
        kt = k_ref[pl.ds(pl.multiple_of(j * tq, tq), tq), :]
        return tuple(_dot_nt(qs[mp], kt[:, mp * ATT_HEAD:(mp + 1) * ATT_HEAD]) for mp in range(2))

    def update(j, s):
        vt = v_ref[pl.ds(pl.multiple_of(j * tq, tq), tq), :]
        for mp in range(2):
            m_ref[mp], l_ref[mp], acc_ref[mp] = _softmax_step(
                s[mp], vt, m_ref[mp], l_ref[mp], acc_ref[mp])

    def body(j, s):
        s_next = scores(j + 1)
        update(j, s)
        return s_next

    s = lax.fori_loop(0, qi, body, scores(0))
    keep = (lax.broadcasted_iota(jnp.int32, (tq, tq), 1)
            <= lax.broadcasted_iota(jnp.int32, (tq, tq), 0))
    update(qi, tuple(jnp.where(keep, x, NEG_BIG) for x in s))
    lam = _lambda(lp_ref[...], lam_init)
    hw = acc_ref.shape[2]
    o = acc_ref[0] * _rep(1.0 / l_ref[0], hw) - lam * (acc_ref[1] * _rep(1.0 / l_ref[1], hw))
    o = _rms(o, SUBLN_EPS) * sg_ref[...] * (1.0 - lam_init)
    o_ref[...] = o.astype(o_ref.dtype)


def _attn_prompt(q, k, v, lp, sg, *, B, T, lam_init, tq=512):
    M, Wd = q.shape
    HW = 2 * ATT_HEAD
    H = Wd // HW
    tq = min(tq, T)
    nq = T // tq
    return pl.pallas_call(
        functools.partial(_attn_prompt_kernel, tq=tq, lam_init=lam_init),
        grid=(B, H, nq),
        in_specs=[pl.BlockSpec((tq, HW), lambda b, h, i: (b * nq + i, h)),
                  pl.BlockSpec((T, HW), lambda b, h, i: (b, h)),
                  pl.BlockSpec((T, HW), lambda b, h, i: (b, h)),
                  pl.BlockSpec((4, ATT_HEAD), lambda b, h, i: (0, 0)),
                  pl.BlockSpec((1, HW), lambda b, h, i: (0, 0))],
        out_specs=pl.BlockSpec((tq, HW), lambda b, h, i: (b * nq + i, h)),
        out_shape=jax.ShapeDtypeStruct((M, Wd), BF16),
        scratch_shapes=[pltpu.VMEM((2, tq, LANES), F32), pltpu.VMEM((2, tq, LANES), F32),
                        pltpu.VMEM((2, tq, HW), F32)],
        compiler_params=_cparams("parallel", "parallel", "arbitrary"),
        name="diff_attn_prompt",
    )(q, k, v, lp, sg.reshape(1, HW))


PAGES_PER_STEP = 4


def _decode_parts(refs, p, nsteps, *, heads, tq, npp, new_rows, lam_init):
    q_ref = refs[0]
    k_refs = refs[1:1 + npp]
    v_refs = refs[1 + npp:1 + 2 * npp]
    kn_ref, vn_ref, lp_ref, sg_ref, o_ref, m_ref, l_ref, acc_ref = refs[1 + 2 * npp:]
    HW = 2 * ATT_HEAD
    R = heads * tq

    def queries():
        q = q_ref[...]
        return [jnp.concatenate([q[:, h * HW + mp * ATT_HEAD:h * HW + (mp + 1) * ATT_HEAD]
                                 for h in range(heads)], axis=0).astype(BF16) for mp in range(2)]

    def raw_scores(qm, k0, k1):
        return jnp.concatenate([_dot_nt(qm[0], k0.astype(BF16)), _dot_nt(qm[1], k1.astype(BF16))],
                               axis=0)

    def head_mask(ncol):
        row = lax.broadcasted_iota(jnp.int32, (2 * R, ncol), 0)
        col = lax.broadcasted_iota(jnp.int32, (2 * R, ncol), 1)
        return (col % heads) == ((row // tq) % heads), row, col

    def init():
        @pl.when(p == 0)
        def _():
            m_ref[...] = jnp.full(m_ref.shape, NEG_BIG, F32)
            l_ref[...] = jnp.zeros(l_ref.shape, F32)
            acc_ref[...] = jnp.zeros(acc_ref.shape, F32)

    def stages():
        qm = queries()
        state = (m_ref[...], l_ref[...], acc_ref[...])
        nk = k_refs[0].shape[0] // 2
        same_head, _, _ = head_mask(nk)
        for j in range(npp):
            g = raw_scores(qm, k_refs[j][pl.ds(0, nk, stride=2), :],
                           k_refs[j][pl.ds(1, nk, stride=2), :])
            yield
            state = _softmax_step(jnp.where(same_head, g, NEG_BIG),
                                  v_refs[j][...].astype(BF16), *state)
            yield
        m_ref[...], l_ref[...], acc_ref[...] = state

    def final():
        @pl.when(p == nsteps - 1)
        def _():
            pad_to = max(new_rows, LANES)
            k0 = kn_ref[pl.ds(0, new_rows, stride=2), :]
            k1 = kn_ref[pl.ds(1, new_rows, stride=2), :]
            vn = vn_ref[...]
            if pad_to > new_rows:
                zk = jnp.zeros((pad_to - new_rows, ATT_HEAD), F32)
                k0 = jnp.concatenate([k0, zk], axis=0)
                k1 = jnp.concatenate([k1, zk], axis=0)
                vn = jnp.concatenate([vn, jnp.zeros((pad_to - new_rows, HW), F32)], axis=0)
            same, row, col = head_mask(pad_to)
            valid = same & ((col // heads) <= (row % tq)) & (col < new_rows)
            s = jnp.where(valid, raw_scores(queries(), k0, k1), NEG_BIG)
            m, l, acc = _softmax_step(s, vn.astype(BF16), m_ref[...], l_ref[...], acc_ref[...])
            lam = _lambda(lp_ref[...], lam_init)
            sg = sg_ref[...]
            for h in range(heads):
                r0 = slice(h * tq, (h + 1) * tq)
                r1 = slice(R + h * tq, R + (h + 1) * tq)
                o = acc[r0] / _rep(l[r0], HW) - lam * (acc[r1] / _rep(l[r1], HW))
                o = _rms(o, SUBLN_EPS) * sg * (1.0 - lam_init)
                o_ref[:, h * HW:(h + 1) * HW] = o

    return init, stages, final


def _attn_sample_kernel(pt_ref, *refs, **kw):
    init, stages, final = _decode_parts(refs, pl.program_id(1), pl.num_programs(1), **kw)
    init()
    for _ in stages():
        pass
    final()


def _decode_call_parts(d, npp, seq_of, first_page_of):
    Bd, npg = d["page_table"].shape
    M, Wd = d["q"].shape
    tq = M // Bd
    HW = 2 * ATT_HEAD
    heads = Wd // HW
    krows, vrows = d["cache_k"].shape[1], d["cache_v"].shape[1]
    new_rows = tq * heads
    rows = pl.BlockSpec((tq, Wd), lambda *ids: (seq_of(*ids), 0))
    per_seq = lambda *ids: (seq_of(*ids), 0, 0)
    const = lambda *ids: (0, 0)

    def page_spec(nrows, width, j):
        return pl.BlockSpec(
            (None, nrows, width),
            lambda *ids: (ids[-1][seq_of(*ids) * npg + first_page_of(*ids) + j], 0, 0))

    return dict(
        in_specs=([rows] + [page_spec(krows, ATT_HEAD, j) for j in range(npp)]
                  + [page_spec(vrows, HW, j) for j in range(npp)]
                  + [pl.BlockSpec((None, 2 * new_rows, ATT_HEAD), per_seq),
                     pl.BlockSpec((None, new_rows, HW), per_seq),
                     pl.BlockSpec((4, ATT_HEAD), const),
                     pl.BlockSpec((1, HW), const)]),
        args=[d["q"], *([d["cache_k"]] * npp), *([d["cache_v"]] * npp),
              d["k_new"].reshape(Bd, 2 * new_rows, ATT_HEAD), d["v_new"].reshape(Bd, new_rows, HW),
              d["lp"], d["sg"].reshape(1, HW)],
        out_spec=rows,
        out_shape=jax.ShapeDtypeStruct((M, Wd), F32),
        scratch=[pltpu.VMEM((2 * new_rows, LANES), F32),
                 pltpu.VMEM((2 * new_rows, LANES), F32),
                 pltpu.VMEM((2 * new_rows, HW), F32)],
        kw=dict(heads=heads, tq=tq, npp=npp, new_rows=new_rows, lam_init=d["lam_init"]),
    )


def _attn_sample(d):
    Bd, npg = d["page_table"].shape
    npp = max(n for n in range(1, PAGES_PER_STEP + 1) if npg % n == 0)
    parts = _decode_call_parts(d, npp, lambda b, p, pt: b, lambda b, p, pt: p * npp)
    grid_spec = pltpu.PrefetchScalarGridSpec(
        num_scalar_prefetch=1,
        grid=(Bd, npg // npp),
        in_specs=parts["in_specs"],
        out_specs=parts["out_spec"],
        scratch_shapes=parts["scratch"],
    )
    return pl.pallas_call(
        functools.partial(_attn_sample_kernel, **parts["kw"]),
        grid_spec=grid_spec,
        out_shape=parts["out_shape"],
        compiler_params=_cparams("parallel", "arbitrary"),
        name="diff_attn_sample",
    )(d["page_table"].reshape(-1), *parts["args"])


def _rope_tables(pos, reps):
    half = ATT_HEAD // 2
    inv = jnp.power(ROPE_THETA, -jnp.arange(half, dtype=F32) / half)
    ang = pos.astype(F32)[:, None] * inv[None, :]
    cos = jnp.concatenate([jnp.cos(ang), jnp.cos(ang)], axis=-1)
    sin = jnp.concatenate([-jnp.sin(ang), jnp.sin(ang)], axis=-1)
    return jnp.tile(cos, (reps, 1)), jnp.tile(sin, (reps, 1))


def _lambda_init(layer_idx):
    return 0.8 - 0.6 * math.exp(-0.3 * layer_idx)


def kernel(x_prompt, x_sample, cache_k, cache_v, state_shift, state_wkv, page_table, norm_mix, norm_ffn, rwkv_mu, rwkv_vec, rwkv_wr, rwkv_wk, rwkv_wv, rwkv_wo, rwkv_w1, rwkv_w2, rwkv_a1, rwkv_a2, rwkv_v0, rwkv_v1, rwkv_v2, rwkv_g1, rwkv_g2, rwkv_rk, kv_norm, kv_wk, kv_wv, attn_wq, attn_wo, attn_lambda, attn_subln, ffn_w1, ffn_w2):
    depth = norm_mix.shape[0]
    n_a = rwkv_mu.shape[0]
    D = x_prompt.shape[-1]
    Wd = kv_wk.shape[1]
    n_att_heads = Wd // (2 * ATT_HEAD)
    n_rwkv_heads = D // RWKV_HEAD

    w_rkv = [jnp.stack([rwkv_wr[l], rwkv_wk[l], rwkv_wv[l]]).astype(BF16) for l in range(n_a)]
    w_o = [rwkv_wo[l].astype(BF16) for l in range(n_a)]
    lora_w = [_pad_lora(rwkv_w1[l], rwkv_w2[l]) for l in range(n_a)]
    lora_a = [_pad_lora(rwkv_a1[l], rwkv_a2[l]) for l in range(n_a)]
    lora_g = [_pad_lora(rwkv_g1[l], rwkv_g2[l]) for l in range(n_a)]
    lora_v = [_pad_lora(rwkv_v1[l], rwkv_v2[l]) for l in range(n_a - 1)]
    wkv_params = []
    for l in range(n_a):
        v0 = rwkv_v0[l - 1] if l > 0 else jnp.zeros((D,), F32)
        wkv_params.append(jnp.concatenate(
            [rwkv_vec[l], rwkv_rk[l].reshape(1, D), v0.reshape(1, D)], axis=0))
    w_kv = jnp.stack([kv_wk, kv_wv]).astype(BF16)
    w_f1 = ffn_w1.astype(BF16)
    w_f2 = ffn_w2.astype(BF16)
    w_q = [attn_wq[j].astype(BF16) for j in range(depth - n_a)]
    w_ao = [attn_wo[j].astype(BF16) for j in range(depth - n_a)]

    def trunk(x, B, T, shift0, wkv0, rope, attend, small):
        act_dtype = F32 if small else BF16
        new_shift, new_wkv = [], []
        rkv_first = None
        k_sh = v_sh = k_att = v_att = None
        xn_next = None
        for l in range(depth):
            if l < n_a:
                loras = [lora_w[l], lora_a[l], lora_g[l]] + ([lora_v[l - 1]] if l > 0 else [])
                if small:
                    mix, last = _prep(x, shift0[l], norm_mix[l, 0], rwkv_mu[l], B=B, T=T,
                                      out_dtype=act_dtype)
                    rkv = _matmul((mix, PL_R), w_rkv[l])
                    branches = [_lora((mix, plane), w1, w2, act=act)
                                for (plane, act), (w1, w2) in zip(LORA_BRANCHES, loras)]
                else:
                    rkv, hidden, last = _rwkv_in(x, shift0[l], norm_mix[l, 0], rwkv_mu[l], w_rkv[l],
                                                 [w1 for w1, _ in loras], B=B, T=T)
                    branches = [(h, w2) for h, (_, w2) in zip(hidden, loras)]
                new_shift.append(last)
                if l == 0:
                    rkv_first = rkv
                wkv_call = ((rkv, branches, wkv_params[l], None if wkv0 is None else wkv0[l]),
                            dict(B=B, T=T, rkv_first=rkv_first if l > 0 else None,
                                 out_dtype=act_dtype))
                if small:
                    branch, s_new = _wkv(*wkv_call[0], **wkv_call[1])
                else:
                    branch, s_new = yield ("wkv", wkv_call)
                new_wkv.append(s_new.astype(state_wkv.dtype))
                w_out = w_o[l]
            else:
                j = l - n_a
                lam_init = _lambda_init(l)
                q = _matmul(xn_next, w_q[j], out_dtypes=(act_dtype,), rope=rope, out_scale=Q_SCALE)
                if small:
                    branch = yield ("attn", dict(q=q, k_new=k_att, v_new=v_att, lp=attn_lambda[j],
                                                 sg=attn_subln[j], lam_init=lam_init))
                else:
                    branch = attend(q, k_att, v_att, attn_lambda[j], attn_subln[j], lam_init)
                w_out = w_ao[j]
            x, nrm = _resnorm(x, branch, norm_mix[l, 1], [norm_ffn[l, 0]], w=w_out)
            g_next = []
            if l == n_a - 1:
                g_next.append(kv_norm)
            if n_a - 1 <= l < depth - 1:
                g_next.append(norm_mix[l + 1, 0])
            x, nrm = _mlp((nrm, 0), (w_f1, l), (w_f2, l), x, norm_ffn[l, 1], g_next)
            if l == n_a - 1:
                if small:
                    k_sh = k_att = _matmul((nrm, 0), w_kv[0], rope=rope)
                    v_sh = v_att = _matmul((nrm, 0), w_kv[1])
                else:
                    k_sh, k_att = _matmul((nrm, 0), w_kv[0], rope=rope, out_dtypes=(F32, BF16),
                                          chunk_rows=True)
                    v_sh, v_att = _matmul((nrm, 0), w_kv[1], out_dtypes=(F32, BF16))
            if g_next:
                xn_next = (nrm, len(g_next) - 1)
        return x, k_sh, v_sh, jnp.stack(new_shift), jnp.stack(new_wkv)

    Bp, Tp, _ = x_prompt.shape
    shift0_p = jnp.zeros((n_a, Bp, D), F32)
    wkv0_p = None
    rope_p = _rope_tables(jnp.arange(Tp, dtype=jnp.int32), 1)

    def attend_p(q, k, v, lp, sg, lam_init):
        return _attn_prompt(q, k, v, lp, sg, B=Bp, T=Tp, lam_init=lam_init)

    prompt = trunk(x_prompt.reshape(Bp * Tp, D), Bp, Tp, shift0_p, wkv0_p, rope_p, attend_p, False)

    Bd, Td, _ = x_sample.shape
    n_pages = page_table.shape[1]
    page = cache_k.shape[1]
    past_len = n_pages * page
    rope_s = _rope_tables(past_len + jnp.arange(Td, dtype=jnp.int32), Bd)
    ck = cache_k.reshape(cache_k.shape[0], page * n_att_heads * 2, ATT_HEAD)
    cv = cache_v.reshape(cache_v.shape[0], page * n_att_heads, 2 * ATT_HEAD)
    sample = trunk(x_sample.reshape(Bd * Td, D), Bd, Td, state_shift, state_wkv, rope_s, None, True)

    def advance(gen, value):
        try:
            return gen.send(value), None
        except StopIteration as done:
            return None, done.value

    can_ride = _ride_shape(Bp, D, Tp, page_table) is not None
    req_p, out_p = advance(prompt, None)
    req_s, out_s = advance(sample, None)
    while req_p is not None or req_s is not None:
        dec = None
        if req_s is not None:
            dec = dict(req_s[1], cache_k=ck, cache_v=cv, page_table=page_table)
        if req_p is not None and dec is not None and can_ride:
            branch, s_new, o = _wkv(*req_p[1][0], **req_p[1][1], decode=dec)
            req_p, out_p = advance(prompt, (branch, s_new))
            req_s, out_s = advance(sample, o)
        elif req_p is not None:
            req_p, out_p = advance(prompt, _wkv(*req_p[1][0], **req_p[1][1]))
        else:
            req_s, out_s = advance(sample, _attn_sample(dec))
    y_p, k_p, v_p, shift_p, wkv_p = out_p
    y_s, k_s, v_s, shift_s, wkv_s = out_s

    H = n_att_heads
    return (y_p.reshape(Bp, Tp, D), y_s.reshape(Bd, Td, D),
            k_p.reshape(Bp, Tp, H, 2, ATT_HEAD), v_p.reshape(Bp, Tp, H, 2 * ATT_HEAD),
            shift_p, wkv_p,
            k_s.reshape(Bd, Td, H, 2, ATT_HEAD), v_s.reshape(Bd, Td, H, 2 * ATT_HEAD),
            shift_s, wkv_s)
```

```python
import functools
import math

import jax
import jax.numpy as jnp
from jax import lax
from jax.experimental import pallas as pl
from jax.experimental.pallas import tpu as pltpu

F32 = jnp.float32
BF16 = jnp.bfloat16

RWKV_HEAD = 64
ATT_HEAD = 128
NORM_EPS = 1e-6
GN_EPS = 64e-5
SUBLN_EPS = 1e-5
ROPE_THETA = 10000.0

LANES = 128
GROUP = 256
HEADS_PER_GROUP = GROUP // RWKV_HEAD
CHUNK = 64
WKV_GROUPS_PER_STEP = 8
VMEM_LIMIT = 48 * 1024 * 1024
MLP_VMEM_LIMIT = 56 * 1024 * 1024
RIDE_VMEM_LIMIT = 60 * 1024 * 1024
NEG_BIG = -1e30


def _cparams(*sem):
    return pltpu.CompilerParams(dimension_semantics=sem, vmem_limit_bytes=VMEM_LIMIT)


def _dot(a, b):
    return jnp.dot(a, b, preferred_element_type=F32)


def _dot_nt(a, b):
    return lax.dot_general(a, b, (((1,), (1,)), ((), ())), preferred_element_type=F32)


def _rms(x, eps):
    return x * lax.rsqrt(jnp.mean(x * x, axis=-1, keepdims=True) + eps)


def _mm_kernel(x_ref, w_ref, *rest, act, rope, out_scale, chunk_rows):
    acc = _dot(x_ref[...].astype(BF16), w_ref[...])
    if act == "tanh":
        acc = jnp.tanh(acc)
    elif act == "sigmoid":
        acc = jax.nn.sigmoid(acc)
    o_refs = rest[2:] if rope else rest
    if rope:
        cos = rest[0][...] * out_scale
        sin = rest[1][...] * out_scale
        tm = acc.shape[0]
        nchunk = acc.shape[1] // LANES
        for c in range(nchunk):
            blk = acc[:, c * LANES:(c + 1) * LANES]
            res = blk * cos + pltpu.roll(blk, LANES // 2, 1) * sin
            for k, o_ref in enumerate(o_refs):
                if chunk_rows and k == 0:
                    o_ref[pl.ds(c, tm, stride=nchunk), :] = res.astype(o_ref.dtype)
                else:
                    o_ref[:, c * LANES:(c + 1) * LANES] = res.astype(o_ref.dtype)
    else:
        if out_scale != 1.0:
            acc = acc * out_scale
        for o_ref in o_refs:
            o_ref[...] = acc.astype(o_ref.dtype)


def _planes(x):
    return x if isinstance(x, tuple) else (x[None], 0)


MAX_WEIGHT_TILE_BYTES = 8 * 1024 * 1024


def _matmul(x, w, *, out_dtypes=(F32,), act=None, rope=None, out_scale=1.0, chunk_rows=False,
            tm=512):
    x, p0 = _planes(x)
    squeeze = w.ndim == 2
    if squeeze:
        w = w[None]
    _, M, K = x.shape
    G, _, N = w.shape
    tm = min(tm, M)
    if rope is not None:
        tm = min(tm, rope[0].shape[0])
        assert rope[0].shape[0] % tm == 0
    tn = N
    while K * tn * w.dtype.itemsize > MAX_WEIGHT_TILE_BYTES and tn % (2 * LANES) == 0:
        tn //= 2
    assert M % tm == 0 and N % tn == 0
    in_specs = [pl.BlockSpec((None, tm, K), lambda g, i, j: (p0 + g, i, 0)),
                pl.BlockSpec((None, K, tn), lambda g, i, j: (g, 0, j))]
    args = [x, w]
    if rope is not None:
        cos, sin = rope
        nblk = cos.shape[0] // tm
        spec = pl.BlockSpec((tm, LANES), lambda g, i, j: (i % nblk, 0))
        in_specs += [spec, spec]
        args += [cos, sin]
    out_specs = [pl.BlockSpec((None, tm, tn), lambda g, i, j: (g, i, j)) for _ in out_dtypes]
    out_shape = [jax.ShapeDtypeStruct((G, M, N), dt) for dt in out_dtypes]
    if chunk_rows:
        assert rope is not None and tn == N and G == 1
        nchunk = N // LANES
        out_specs[0] = pl.BlockSpec((None, tm * nchunk, LANES), lambda g, i, j: (g, i, 0))
        out_shape[0] = jax.ShapeDtypeStruct((G, M * nchunk, LANES), out_dtypes[0])
    outs = pl.pallas_call(
        functools.partial(_mm_kernel, act=act, rope=rope is not None, out_scale=out_scale,
                          chunk_rows=chunk_rows),
        grid=(G, M // tm, N // tn),
        in_specs=in_specs,
        out_specs=out_specs,
        out_shape=out_shape,
        compiler_params=_cparams("parallel", "parallel", "arbitrary"),
        name="matmul",
    )(*args)
    outs = [o[0] if squeeze else o for o in outs]
    return outs[0] if len(outs) == 1 else outs


def _lora_kernel(x_ref, w1_ref, w2_ref, o_ref, *, act):
    h = _dot(x_ref[...].astype(BF16), w1_ref[...])
    if act == "tanh":
        h = jnp.tanh(h)
    elif act == "sigmoid":
        h = jax.nn.sigmoid(h)
    o_ref[...] = _dot(h.astype(BF16), w2_ref[...])


def _lora(x, w1, w2, *, act=None, tm=512):
    x, p0 = _planes(x)
    _, M, K = x.shape
    R = w1.shape[1]
    N = w2.shape[1]
    tm = min(tm, M)
    return pl.pallas_call(
        functools.partial(_lora_kernel, act=act),
        grid=(M // tm,),
        in_specs=[pl.BlockSpec((None, tm, K), lambda i: (p0, i, 0)),
                  pl.BlockSpec((K, R), lambda i: (0, 0)),
                  pl.BlockSpec((R, N), lambda i: (0, 0))],
        out_specs=pl.BlockSpec((tm, N), lambda i: (i, 0)),
        out_shape=jax.ShapeDtypeStruct((M, N), F32),
        compiler_params=_cparams("parallel"),
        name="lora",
    )(x, w1, w2)


def _pad_lora(w1, w2):
    r = w1.shape[1]
    rp = -(-r // LANES) * LANES
    return (jnp.pad(w1, ((0, 0), (0, rp - r))).astype(BF16),
            jnp.pad(w2, ((0, rp - r), (0, 0))).astype(BF16))


def _mlp_kernel(x_ref, w1_ref, w2_ref, xr_ref, gp_ref, *rest, n_out, nf):
    if n_out:
        gn_ref, xo_ref, no_ref = rest
    else:
        xo_ref, = rest
    f = pl.program_id(1)

    @pl.when(f == 0)
    def _():
        xo_ref[...] = jnp.zeros(xo_ref.shape, F32)

    h = jnp.maximum(_dot(x_ref[...], w1_ref[...]), 0.0)
    xo_ref[...] += _dot((h * h).astype(BF16), w2_ref[...])

    @pl.when(f == nf - 1)
    def _():
        x = xr_ref[...] + _rms(xo_ref[...], NORM_EPS) * gp_ref[...]
        xo_ref[...] = x
        if n_out:
            y = _rms(x, NORM_EPS)
            for j in range(n_out):
                no_ref[j] = (y * gn_ref[j:j + 1, :]).astype(no_ref.dtype)


MLP_HIDDEN_TILE = 1024


def _mlp_w1_tiles(w1):
    L, D, F = w1.shape
    tf = min(MLP_HIDDEN_TILE, F)
    return w1.reshape(L, D, F // tf, tf).transpose(0, 2, 1, 3)


def _mlp(x, w1, w2, x_res, g_post, g_next, *, tm=512):
    x, p0 = _planes(x)
    w1, l1 = w1
    w2, l2 = w2
    _, M, D = x.shape
    _, nf, _, tf = w1.shape
    tm = min(tm, M)
    n_out = len(g_next)
    row = pl.BlockSpec((tm, D), lambda i, f: (i, 0))
    vec = pl.BlockSpec((1, D), lambda i, f: (0, 0))
    in_specs = [pl.BlockSpec((None, tm, D), lambda i, f: (p0, i, 0)),
                pl.BlockSpec((None, None, D, tf), lambda i, f: (l1, f, 0, 0)),
                pl.BlockSpec((None, tf, D), lambda i, f: (l2, f, 0)),
                pl.BlockSpec((tm, D), lambda i, f: (i, 0), pipeline_mode=pl.Buffered(1)),
                vec]
    args = [x, w1, w2, x_res, g_post.reshape(1, D)]
    out_specs = [row]
    out_shape = [jax.ShapeDtypeStruct((M, D), F32)]
    if n_out:
        in_specs.append(pl.BlockSpec((n_out, D), lambda i, f: (0, 0)))
        args.append(jnp.stack(g_next))
        out_specs.append(pl.BlockSpec((n_out, tm, D), lambda i, f: (0, i, 0)))
        out_shape.append(jax.ShapeDtypeStruct((n_out, M, D), BF16))
    res = pl.pallas_call(
        functools.partial(_mlp_kernel, n_out=n_out, nf=nf),
        grid=(M // tm, nf),
        in_specs=in_specs,
        out_specs=out_specs,
        out_shape=out_shape,
        compiler_params=pltpu.CompilerParams(dimension_semantics=("parallel", "arbitrary"),
                                             vmem_limit_bytes=MLP_VMEM_LIMIT),
        name="mlp",
    )(*args)
    return (res[0], res[1]) if n_out else (res[0], None)


MIX_ORDER = (0, 2, 3, 1, 4, 5)
PL_R, PL_K, PL_V, PL_W, PL_A, PL_G = range(6)


def _prep_kernel(x_ref, sh_ref, g_ref, mu_ref, mix_ref, last_ref, carry_ref):
    i = pl.program_id(1)
    xn = _rms(x_ref[...], NORM_EPS) * g_ref[...]
    tr = xn.shape[0]

    @pl.when(i == 0)
    def _():
        carry_ref[...] = sh_ref[...]

    row = lax.broadcasted_iota(jnp.int32, xn.shape, 0)
    xprev = jnp.where(row == 0, carry_ref[...], pltpu.roll(xn, 1, 0))
    xx = xprev - xn
    for plane, m in enumerate(MIX_ORDER):
        mix_ref[plane] = (xn + xx * mu_ref[m:m + 1, :]).astype(mix_ref.dtype)
    last = xn[tr - 1:tr, :]
    carry_ref[...] = last
    last_ref[...] = last


def _prep(x, shift0, g, mu, *, B, T, out_dtype):
    M, D = x.shape
    tr = min(256, T)
    nt = T // tr
    mix, last = pl.pallas_call(
        _prep_kernel,
        grid=(B, nt),
        in_specs=[pl.BlockSpec((tr, D), lambda b, i: (b * nt + i, 0)),
                  pl.BlockSpec((None, 1, D), lambda b, i: (b, 0, 0)),
                  pl.BlockSpec((1, D), lambda b, i: (0, 0)),
                  pl.BlockSpec((6, D), lambda b, i: (0, 0))],
        out_specs=[pl.BlockSpec((6, tr, D), lambda b, i: (0, b * nt + i, 0)),
                   pl.BlockSpec((None, 1, D), lambda b, i: (b, 0, 0))],
        out_shape=[jax.ShapeDtypeStruct((6, M, D), out_dtype),
                   jax.ShapeDtypeStruct((B, 1, D), F32)],
        scratch_shapes=[pltpu.VMEM((1, D), F32)],
        compiler_params=_cparams("arbitrary", "arbitrary"),
        name="rwkv_prep",
    )(x, shift0.reshape(B, 1, D), g.reshape(1, D), mu)
    return mix, last.reshape(B, D)


N_RKV = 3
MIX_SUB_ROWS = 16


def _rwkv_in_kernel(x_ref, sh_ref, g_ref, mu_ref, w_ref, *rest, nt, n_lora):
    l1_refs = rest[:n_lora]
    rkv_ref = rest[n_lora]
    h_refs = rest[n_lora + 1:2 * n_lora + 1]
    last_ref, mix_ref, carry_ref = rest[2 * n_lora + 1:]
    i = pl.program_id(0)
    g = pl.program_id(1)

    @pl.when(g == 0)
    def _():
        tr = x_ref.shape[0]
        sub = min(MIX_SUB_ROWS, tr)
        row = lax.broadcasted_iota(jnp.int32, (sub, x_ref.shape[1]), 0)

        def chunk(ci, prev_row):
            rows = pl.ds(pl.multiple_of(ci * sub, sub), sub)
            xn = _rms(x_ref[rows, :], NORM_EPS) * g_ref[...]
            xx = jnp.where(row == 0, prev_row, pltpu.roll(xn, 1, 0)) - xn
            for plane, m in enumerate(MIX_ORDER):
                mix_ref[plane, rows, :] = (xn + xx * mu_ref[m:m + 1, :]).astype(mix_ref.dtype)
            return xn[sub - 1:sub, :]

        first = jnp.where(i % nt == 0, sh_ref[...], carry_ref[...])
        last = lax.fori_loop(0, tr // sub, chunk, first)
        carry_ref[...] = last
        last_ref[...] = last

    @pl.when(g < N_RKV)
    def _():
        rkv_ref[...] = _dot(mix_ref[g], w_ref[...])

    @pl.when(g == N_RKV)
    def _():
        for (plane, act), l1_ref, h_ref in zip(LORA_BRANCHES, l1_refs, h_refs):
            h = _dot(mix_ref[plane], l1_ref[...])
            if act == "tanh":
                h = jnp.tanh(h)
            elif act == "sigmoid":
                h = jax.nn.sigmoid(h)
            h_ref[...] = h.astype(h_ref.dtype)


LORA_BRANCHES = ((PL_W, "tanh"), (PL_A, None), (PL_G, "sigmoid"), (PL_V, None))


def _rwkv_in(x, shift0, g, mu, w_rkv, lora1, *, B, T, tm=512):
    M, D = x.shape
    tm = min(tm, T)
    nt = T // tm
    n_lora = len(lora1)
    in_specs = [pl.BlockSpec((tm, D), lambda i, g: (i, 0)),
                pl.BlockSpec((None, 1, D), lambda i, g: (i // nt, 0, 0)),
                pl.BlockSpec((1, D), lambda i, g: (0, 0)),
                pl.BlockSpec((6, D), lambda i, g: (0, 0)),
                pl.BlockSpec((None, D, D), lambda i, g: (jnp.minimum(g, N_RKV - 1), 0, 0))]
    in_specs += [pl.BlockSpec(w.shape, lambda i, g: (0, 0), pipeline_mode=pl.Buffered(1))
                 for w in lora1]
    out_specs = [pl.BlockSpec((None, tm, D), lambda i, g: (jnp.minimum(g, N_RKV - 1), i, 0))]
    out_specs += [pl.BlockSpec((tm, w.shape[1]), lambda i, g: (i, 0)) for w in lora1]
    out_specs.append(pl.BlockSpec((None, 1, D), lambda i, g: (i // nt, 0, 0)))
    out_shape = [jax.ShapeDtypeStruct((N_RKV, M, D), F32)]
    out_shape += [jax.ShapeDtypeStruct((M, w.shape[1]), BF16) for w in lora1]
    out_shape.append(jax.ShapeDtypeStruct((B, 1, D), F32))
    res = pl.pallas_call(
        functools.partial(_rwkv_in_kernel, nt=nt, n_lora=n_lora),
        grid=(M // tm, N_RKV + 1),
        in_specs=in_specs,
        out_specs=out_specs,
        out_shape=out_shape,
        scratch_shapes=[pltpu.VMEM((6, tm, D), BF16), pltpu.VMEM((1, D), F32)],
        compiler_params=pltpu.CompilerParams(dimension_semantics=("arbitrary", "arbitrary"),
                                             vmem_limit_bytes=MLP_VMEM_LIMIT),
        name="rwkv_in",
    )(x, shift0.reshape(B, 1, D), g.reshape(1, D), mu, w_rkv, *lora1)
    return res[0], res[1:1 + n_lora], res[-1].reshape(B, D)


def _resnorm_kernel(x_ref, h_ref, *rest, n_out, project):
    w_ref = None
    if project:
        w_ref, *rest = rest
    gp_ref, *rest = rest
    if n_out:
        gn_ref, xo_ref, no_ref = rest
    else:
        xo_ref, = rest
    tr = x_ref.shape[0]
    halves = 2 if project and tr % 32 == 0 else 1
    bounds = [slice(k * tr // halves, (k + 1) * tr // halves) for k in range(halves)]
    hs = [h_ref[rows, :] for rows in bounds]
    if project:
        hs = [_dot(h.astype(BF16), w_ref[...]) for h in hs]
    for rows, h in zip(bounds, hs):
        x = x_ref[rows, :] + _rms(h, NORM_EPS) * gp_ref[...]
        if n_out:
            y = _rms(x, NORM_EPS)
            for j in range(n_out):
                no_ref[j, rows, :] = (y * gn_ref[j:j + 1, :]).astype(no_ref.dtype)
        xo_ref[rows, :] = x


def _resnorm(x, h, g_post, g_next, w=None):
    M, D = x.shape
    n_out = len(g_next)
    tr = min(512 if w is not None else 256, M)
    row = pl.BlockSpec((tr, D), lambda i: (i, 0))
    vec = pl.BlockSpec((1, D), lambda i: (0, 0))
    in_specs = [row, pl.BlockSpec((tr, h.shape[1]), lambda i: (i, 0))]
    args = [x, h]
    if w is not None:
        in_specs.append(pl.BlockSpec(w.shape, lambda i: (0, 0)))
        args.append(w)
    in_specs.append(vec)
    args.append(g_post.reshape(1, D))
    out_specs = [row]
    out_shape = [jax.ShapeDtypeStruct((M, D), F32)]
    if n_out:
        in_specs.append(pl.BlockSpec((n_out, D), lambda i: (0, 0)))
        args.append(jnp.stack(g_next))
        out_specs.append(pl.BlockSpec((n_out, tr, D), lambda i: (0, i, 0)))
        out_shape.append(jax.ShapeDtypeStruct((n_out, M, D), BF16))
    res = pl.pallas_call(
        functools.partial(_resnorm_kernel, n_out=n_out, project=w is not None),
        grid=(M // tr,),
        in_specs=in_specs,
        out_specs=out_specs,
        out_shape=out_shape,
        compiler_params=_cparams("parallel"),
        name="resnorm",
    )(*args)
    return (res[0], res[1]) if n_out else (res[0], None)


def _norm_kernel(x_ref, g_ref, o_ref):
    o_ref[...] = (_rms(x_ref[...], NORM_EPS) * g_ref[...]).astype(o_ref.dtype)


def _norm(x, g):
    M, D = x.shape
    tr = min(256, M)
    return pl.pallas_call(
        _norm_kernel,
        grid=(M // tr,),
        in_specs=[pl.BlockSpec((tr, D), lambda i: (i, 0)), pl.BlockSpec((1, D), lambda i: (0, 0))],
        out_specs=pl.BlockSpec((tr, D), lambda i: (i, 0)),
        out_shape=jax.ShapeDtypeStruct((M, D), BF16),
        compiler_params=_cparams("parallel"),
        name="norm",
    )(x, g.reshape(1, D))


def _split2(x):
    hi = x.astype(BF16)
    lo = (x - hi.astype(F32)).astype(BF16)
    return hi, lo


def _seg_sum(x, ones_bd):
    hi, lo = _split2(x)
    n = x.shape[0]
    s = _dot(jnp.concatenate([hi, lo], axis=0), ones_bd)
    return s[:n] + s[n:]


def _wkv_kernel(*refs, rows, vres, ng, fused):
    _wkv_body(refs, pl.program_id(2), pl.num_programs(2), rows=rows, vres=vres, ng=ng, fused=fused)


def _wkv_num_inputs(vres, fused):
    return (3 + vres) + (3 + vres) * (2 if fused else 1) + 2


def _wkv_body(refs, c, nc, *, rows, vres, ng, fused, rider=None):
    n_rk = 3 + vres
    nb = 3 + vres
    rk_refs = refs[:n_rk]
    br_refs = refs[n_rk:n_rk + (2 * nb if fused else nb)]
    p_ref, s0_ref, y_ref, so_ref, s_ref = refs[n_rk + len(br_refs):]
    row_head = lax.broadcasted_iota(jnp.int32, (GROUP, GROUP), 0) // RWKV_HEAD
    lane_head = lax.broadcasted_iota(jnp.int32, (GROUP, GROUP), 1) // RWKV_HEAD

    @pl.when(c == 0)
    def _():
        for gi in range(ng):
            s_ref[gi] = jnp.where(row_head == lane_head, s0_ref[gi], 0.0)

    if rider is not None:
        rider[0]()

    gens = []
    hidden = [ref[...] for ref in br_refs[:nb]] if fused else None
    for gi in range(ng):
        lanes = pl.ds(gi * GROUP, GROUP)
        rk = [ref[:, lanes] for ref in rk_refs]
        if fused:
            br = [_dot(h, w2_ref[:, lanes]) for h, w2_ref in zip(hidden, br_refs[nb:])]
        else:
            br = [ref[:, lanes] for ref in br_refs]
        vals = rk[:3] + br[:3] + ([rk[3], br[3]] if vres else []) + [p_ref[:, lanes], s_ref[gi]]
        gens.append(_wkv_group(*vals, rows=rows, vres=vres))
    if rider is not None:
        gens.append(rider[1]())
    outs = [None] * len(gens)
    live = list(range(len(gens)))
    while live:
        for gi in list(live):
            try:
                next(gens[gi])
            except StopIteration as done:
                outs[gi] = done.value
                live.remove(gi)
    for gi, (y, s_new) in enumerate(outs[:ng]):
        y_ref[:, pl.ds(gi * GROUP, GROUP)] = y.astype(y_ref.dtype)
        s_ref[gi] = s_new

    if rider is not None:
        rider[2]()

    @pl.when(c == nc - 1)
    def _():
        for gi in range(ng):
            s = s_ref[gi]
            packed = s
            for h in range(1, HEADS_PER_GROUP):
                packed = jnp.where(row_head == h, pltpu.roll(s, GROUP - h * RWKV_HEAD, 1), packed)
            so_ref[gi] = packed[:, :RWKV_HEAD]


def _wkv_group(r_ref, k_ref, v_ref, wl_ref, al_ref, g_ref, *rest, rows, vres):
    if vres:
        vf_ref, vl_ref, p, s = rest
    else:
        p, s = rest
    L = CHUNK
    W = GROUP
    HG = HEADS_PER_GROUP

    def load(val):
        x = val.astype(F32)
        if rows < L:
            x = jnp.concatenate([x, jnp.zeros((L - rows, W), F32)], axis=0)
        return x

    w0, a0, k_k, k_a = p[0:1], p[1:2], p[2:3], p[3:4]
    lnx_w, lnx_b, rk, v0 = p[4:5], p[5:6], p[6:7], p[7:8]

    lane_head = lax.broadcasted_iota(jnp.int32, (1, W), 1) // RWKV_HEAD
    rr = lax.broadcasted_iota(jnp.int32, (W, W), 0) // RWKV_HEAD
    cc = lax.broadcasted_iota(jnp.int32, (W, W), 1) // RWKV_HEAD
    bd = rr == cc
    ones_bd = jnp.where(bd, 1.0, 0.0).astype(BF16)
    trow = lax.broadcasted_iota(jnp.int32, (L, HG * L), 0)
    tcol = lax.broadcasted_iota(jnp.int32, (L, HG * L), 1) & (L - 1)
    strict = tcol < trow
    incl = tcol <= trow
    rb = lax.broadcasted_iota(jnp.int32, (HG * L, HG * L), 0) // L
    cb = lax.broadcasted_iota(jnp.int32, (HG * L, HG * L), 1) // L
    bd_l = rb == cb

    def ystack(x):
        return jnp.concatenate(
            [jnp.where(lane_head == h, x, 0.0) for h in range(HG)], axis=0).astype(BF16)

    r = load(r_ref)
    k = load(k_ref)
    v = load(v_ref)
    a_sig = jax.nn.sigmoid(load(al_ref) + a0)
    if vres:
        v = v + (load(vf_ref) - v) * jax.nn.sigmoid(load(vl_ref) + v0)
    z = -(load(wl_ref) + w0)
    softplus = jnp.maximum(z, 0.0) + jnp.log(1.0 + jnp.exp(-jnp.abs(z)))
    lw = -jnp.exp(-softplus - 0.5)
    if rows < L:
        trow1 = lax.broadcasted_iota(jnp.int32, (L, W), 0)
        lw = jnp.where(trow1 < rows, lw, 0.0)
    kkr = k * k_k
    nrm2 = _seg_sum(kkr * kkr, ones_bd)
    yield
    nrm = jnp.sqrt(nrm2)
    kk = kkr / jnp.maximum(nrm, 1e-12)
    k2 = k * (1.0 + (a_sig - 1.0) * k_a)
    a = -kk
    b = kk * a_sig

    l1 = lw.astype(BF16)
    l2r = lw - l1.astype(F32)
    l2 = l2r.astype(BF16)
    l3 = (l2r - l2.astype(F32)).astype(BF16)
    tri = (lax.broadcasted_iota(jnp.int32, (L, L), 1)
           <= lax.broadcasted_iota(jnp.int32, (L, L), 0))
    tri = jnp.where(tri, 1.0, 0.0).astype(BF16)
    cs = _dot(tri, jnp.concatenate([l1, l2, l3], axis=1))
    yield
    cum = cs[:, :W] + cs[:, W:2 * W] + cs[:, 2 * W:]
    cl = cum[L - 1:L, :]
    e_c = jnp.exp(cum)
    e_cm = jnp.exp(cum - lw)
    e_nc = jnp.exp(-cum)
    e_lc = jnp.exp(cl - cum)
    g_l = jnp.exp(cl)
    at = a * e_cm
    rt = r * e_c
    bt = b * e_nc
    kt = k2 * e_nc
    bh = b * e_lc
    kh = k2 * e_lc

    x4 = _dot_nt(jnp.concatenate([at, rt], axis=0).astype(BF16),
                 jnp.concatenate([ystack(bt), ystack(kt)], axis=0))
    yield
    n_ab = jnp.where(strict, x4[:L, :HG * L], 0.0)
    m_ak = jnp.where(strict, x4[:L, HG * L:], 0.0)
    m_rb = jnp.where(incl, x4[L:, :HG * L], 0.0)
    m_rk = jnp.where(incl, x4[L:, HG * L:], 0.0)

    def bdiag(x):
        return jnp.where(bd_l, jnp.concatenate([x] * HG, axis=0), 0.0).astype(BF16)

    t_inv = jnp.where(tcol == trow, 1.0, 0.0) + n_ab
    yv = ystack(v)
    makv = _dot(m_ak.astype(BF16), yv)
    pw = _dot(n_ab.astype(BF16), bdiag(n_ab))
    yield
    steps = int(math.log2(L)) - 1
    for it in range(steps):
        rhs = bdiag(pw)
        if it < steps - 1:
            res = _dot(jnp.concatenate([t_inv, pw], axis=0).astype(BF16), rhs)
            yield
            t_inv = t_inv + res[:L]
            pw = res[L:]
        else:
            res = _dot(t_inv.astype(BF16), rhs)
            yield
            t_inv = t_inv + res

    au = _dot(t_inv.astype(BF16), jnp.concatenate([ystack(at), ystack(makv)], axis=1))
    yield
    abar = au[:, :W]
    u0 = au[:, W:]
    y_abar = ystack(abar)
    d1 = _dot(m_rb.astype(BF16), jnp.concatenate([y_abar, ystack(u0)], axis=1))
    d2 = _dot(m_rk.astype(BF16), yv)
    s_bf = s.astype(BF16)
    sa = _dot_nt(s_bf, y_abar)
    yield
    rbar = rt + d1[:, :W]
    o0 = d1[:, W:] + d2
    o = _dot_nt(rbar.astype(BF16), s_bf) + o0
    uv_t = jnp.concatenate([u0, v], axis=0).T
    q_t = _dot(uv_t.astype(BF16), jnp.concatenate([bh, kh], axis=0).astype(BF16))
    s_new = s * g_l + _dot(sa.astype(BF16), ystack(bh)) + jnp.where(bd, q_t, 0.0)
    bonus = _seg_sum(r * k2 * rk, ones_bd) * v
    yield
    inv_n = 1.0 / RWKV_HEAD
    mean = _seg_sum(o, ones_bd) * inv_n
    yield
    dlt = o - mean
    var = _seg_sum(dlt * dlt, ones_bd) * inv_n
    yield
    yn = dlt * lax.rsqrt(var + GN_EPS) * lnx_w + lnx_b
    out = (yn + bonus) * load(g_ref)
    return out[:rows], s_new


def _wkv(rkv, branches, params, s0, *, B, T, rkv_first=None, out_dtype=BF16, decode=None):
    _, M, D = rkv.shape
    fused = isinstance(branches[0], tuple)
    vres = rkv_first is not None
    n_groups = B * D // GROUP
    if s0 is None:
        s0_rep = jnp.zeros((n_groups, GROUP, GROUP), F32)
    else:
        s0_rep = jnp.tile(s0.astype(F32).reshape(n_groups, GROUP, RWKV_HEAD), (1, 1, HEADS_PER_GROUP))
    ng, G, rows, nc = _wkv_grid(D, T)
    mk = (lambda f: f) if decode is None else (lambda f: (lambda b, g, c, pt: f(b, g, c)))

    def plane(pi):
        return pl.BlockSpec((None, rows, ng * GROUP), mk(lambda b, g, c: (pi, b * nc + c, g)))

    tile = pl.BlockSpec((rows, ng * GROUP), mk(lambda b, g, c: (b * nc + c, g)))
    lanes = mk(lambda b, g, c: (0, g))
    state = pl.BlockSpec((ng, GROUP, GROUP), mk(lambda b, g, c: (b * G + g, 0, 0)))
    in_specs = [plane(0), plane(1), plane(2)]
    args = [rkv, rkv, rkv]
    if vres:
        in_specs.append(plane(2))
        args.append(rkv_first)
    if fused:
        in_specs += [pl.BlockSpec((rows, h.shape[1]), mk(lambda b, g, c: (b * nc + c, 0)))
                     for h, _ in branches]
        in_specs += [pl.BlockSpec((w2.shape[0], ng * GROUP), lanes) for _, w2 in branches]
        args += [h for h, _ in branches] + [w2 for _, w2 in branches]
    else:
        in_specs += [tile] * len(branches)
        args += list(branches)
    in_specs += [pl.BlockSpec((8, ng * GROUP), lanes), state]
    args += [params, s0_rep]
    out_specs = [tile, pl.BlockSpec((ng, GROUP, RWKV_HEAD), mk(lambda b, g, c: (b * G + g, 0, 0)))]
    out_shape = [jax.ShapeDtypeStruct((M, D), out_dtype),
                 jax.ShapeDtypeStruct((n_groups, GROUP, RWKV_HEAD), F32)]
    scratch = [pltpu.VMEM((ng, GROUP, GROUP), F32)]
    wkv_kw = dict(rows=rows, vres=vres, ng=ng, fused=fused)
    if decode is None:
        y, s_out = pl.pallas_call(
            functools.partial(_wkv_kernel, **wkv_kw),
            grid=(B, G, nc),
            in_specs=in_specs,
            out_specs=out_specs,
            out_shape=out_shape,
            scratch_shapes=scratch,
            compiler_params=_cparams("parallel", "parallel", "arbitrary"),
            name="wkv7",
        )(*args)
        return y, s_out.reshape(B, D // RWKV_HEAD, RWKV_HEAD, RWKV_HEAD)

    spb, npp = _ride_shape(B, D, T, decode["page_table"])
    step = lambda b, g, c: (b * G + g) * nc + c
    dec = _decode_call_parts(decode, npp, lambda b, g, c, pt: step(b, g, c) // spb,
                             lambda b, g, c, pt: (step(b, g, c) % spb) * npp)
    n_wkv_in, n_dec_in = len(args), len(dec["args"])
    grid_spec = pltpu.PrefetchScalarGridSpec(
        num_scalar_prefetch=1,
        grid=(B, G, nc),
        in_specs=in_specs + dec["in_specs"],
        out_specs=out_specs + [dec["out_spec"]],
        scratch_shapes=scratch + dec["scratch"],
    )
    y, s_out, o = pl.pallas_call(
        functools.partial(_wkv_decode_kernel, n_wkv_in=n_wkv_in, n_dec_in=n_dec_in, G=G, nc=nc,
                          spb=spb, wkv_kw=wkv_kw, dec_kw=dec["kw"]),
        grid_spec=grid_spec,
        out_shape=out_shape + [dec["out_shape"]],
        compiler_params=pltpu.CompilerParams(
            dimension_semantics=("arbitrary", "arbitrary", "arbitrary"),
            vmem_limit_bytes=RIDE_VMEM_LIMIT),
        name="wkv7_decode",
    )(decode["page_table"].reshape(-1), *args, *dec["args"])
    return y, s_out.reshape(B, D // RWKV_HEAD, RWKV_HEAD, RWKV_HEAD), o


def _wkv_grid(D, T):
    ng = max(d for d in range(1, WKV_GROUPS_PER_STEP + 1) if (D // GROUP) % d == 0)
    rows = min(CHUNK, T)
    return ng, D // (ng * GROUP), rows, T // rows


MAX_RIDING_PAGES = 8


def _ride_shape(B, D, T, page_table):
    _, G, _, nc = _wkv_grid(D, T)
    steps = B * G * nc
    Bd, npg = page_table.shape
    if steps % Bd or npg % (steps // Bd) or npg // (steps // Bd) > MAX_RIDING_PAGES:
        return None
    return steps // Bd, npg // (steps // Bd)


def _wkv_decode_kernel(pt_ref, *refs, n_wkv_in, n_dec_in, G, nc, spb, wkv_kw, dec_kw):
    wkv_in = refs[:n_wkv_in]
    dec_in = refs[n_wkv_in:n_wkv_in + n_dec_in]
    y_ref, so_ref, o_ref, s_ref, m_ref, l_ref, acc_ref = refs[n_wkv_in + n_dec_in:]
    c = pl.program_id(2)
    step = (pl.program_id(0) * G + pl.program_id(1)) * nc + c
    rider = _decode_parts(tuple(dec_in) + (o_ref, m_ref, l_ref, acc_ref), step % spb, spb, **dec_kw)
    _wkv_body(tuple(wkv_in) + (y_ref, so_ref, s_ref), c, nc, rider=rider, **wkv_kw)


def _lambda(lp, lam_init):
    return (jnp.exp(jnp.sum(lp[0:1] * lp[1:2], keepdims=True))
            - jnp.exp(jnp.sum(lp[2:3] * lp[3:4], keepdims=True)) + lam_init)


Q_SCALE = ATT_HEAD ** -0.5 * math.log2(math.e)


def _rep(x, n):
    return x if n == LANES else jnp.concatenate([x] * (n // LANES), axis=1)


def _softmax_step(s, vt, m, l, acc, ones=None):
    m_new = jnp.maximum(m, jnp.max(s, axis=-1, keepdims=True))
    alpha = jnp.exp2(m - m_new)
    p = jnp.exp2(s - _rep(m_new, s.shape[1]))
    pb = p.astype(BF16)
    if ones is None:
        l_new = alpha * l + jnp.sum(p, axis=-1, keepdims=True)
    else:
        l_new = alpha * l + _dot(pb, ones)
    acc_new = _rep(alpha, acc.shape[1]) * acc + _dot(pb, vt)
    return m_new, l_new, acc_new


def _attn_prompt_kernel(q_ref, k_ref, v_ref, lp_ref, sg_ref, o_ref, m_ref, l_ref, acc_ref,
                        *, tq, lam_init):
    qi = pl.program_id(2)
    q = q_ref[...]
    qs = (q[:, :ATT_HEAD], q[:, ATT_HEAD:])
    m_ref[...] = jnp.full(m_ref.shape, NEG_BIG, F32)
    l_ref[...] = jnp.zeros(l_ref.shape, F32)
    acc_ref[...] = jnp.zeros(acc_ref.shape, F32)

    def scores(j):
        kt = k_ref[pl.ds(pl.multiple_of(j * tq, tq), tq), :]
        return tuple(_dot_nt(qs[mp], kt[:, mp * ATT_HEAD:(mp + 1) * ATT_HEAD]) for mp in range(2))

    def update(j, s):
        vt = v_ref[pl.ds(pl.multiple_of(j * tq, tq), tq), :]
        for mp in range(2):
            m_ref[mp], l_ref[mp], acc_ref[mp] = _softmax_step(
                s[mp], vt, m_ref[mp], l_ref[mp], acc_ref[mp])

    def body(j, s):
        s_next = scores(j + 1)
        update(j, s)
        return s_next

    s = lax.fori_loop(0, qi, body, scores(0))
    keep = (lax.broadcasted_iota(jnp.int32, (tq, tq), 1)
            <= lax.broadcasted_iota(jnp.int32, (tq, tq), 0))
    update(qi, tuple(jnp.where(keep, x, NEG_BIG) for x in s))
    lam = _lambda(lp_ref[...], lam_init)
    hw = acc_ref.shape[2]
    o = acc_ref[0] * _rep(1.0 / l_ref[0], hw) - lam * (acc_ref[1] * _rep(1.0 / l_ref[1], hw))
    o = _rms(o, SUBLN_EPS) * sg_ref[...] * (1.0 - lam_init)
    o_ref[...] = o.astype(o_ref.dtype)


def _attn_prompt(q, k, v, lp, sg, *, B, T, lam_init, tq=512):
    M, Wd = q.shape
    HW = 2 * ATT_HEAD
    H = Wd // HW
    tq = min(tq, T)
    nq = T // tq
    return pl.pallas_call(
        functools.partial(_attn_prompt_kernel, tq=tq, lam_init=lam_init),
        grid=(B, H, nq),
        in_specs=[pl.BlockSpec((tq, HW), lambda b, h, i: (b * nq + i, h)),
                  pl.BlockSpec((T, HW), lambda b, h, i: (b, h)),
                  pl.BlockSpec((T, HW), lambda b, h, i: (b, h)),
                  pl.BlockSpec((4, ATT_HEAD), lambda b, h, i: (0, 0)),
                  pl.BlockSpec((1, HW), lambda b, h, i: (0, 0))],
        out_specs=pl.BlockSpec((tq, HW), lambda b, h, i: (b * nq + i, h)),
        out_shape=jax.ShapeDtypeStruct((M, Wd), BF16),
        scratch_shapes=[pltpu.VMEM((2, tq, LANES), F32), pltpu.VMEM((2, tq, LANES), F32),
                        pltpu.VMEM((2, tq, HW), F32)],
        compiler_params=_cparams("parallel", "parallel", "arbitrary"),
        name="diff_attn_prompt",
    )(q, k, v, lp, sg.reshape(1, HW))


PAGES_PER_STEP = 4


def _decode_parts(refs, p, nsteps, *, heads, tq, npp, new_rows, lam_init):
    q_ref = refs[0]
    k_refs = refs[1:1 + npp]
    v_refs = refs[1 + npp:1 + 2 * npp]
    kn_ref, vn_ref, lp_ref, sg_ref, o_ref, m_ref, l_ref, acc_ref = refs[1 + 2 * npp:]
    HW = 2 * ATT_HEAD
    R = heads * tq

    def queries():
        q = q_ref[...]
        return [jnp.concatenate([q[:, h * HW + mp * ATT_HEAD:h * HW + (mp + 1) * ATT_HEAD]
                                 for h in range(heads)], axis=0).astype(BF16) for mp in range(2)]

    def raw_scores(qm, k0, k1):
        return jnp.concatenate([_dot_nt(qm[0], k0.astype(BF16)), _dot_nt(qm[1], k1.astype(BF16))],
                               axis=0)

    def head_mask(ncol):
        row = lax.broadcasted_iota(jnp.int32, (2 * R, ncol), 0)
        col = lax.broadcasted_iota(jnp.int32, (2 * R, ncol), 1)
        return (col % heads) == ((row // tq) % heads), row, col

    def init():
        @pl.when(p == 0)
        def _():
            m_ref[...] = jnp.full(m_ref.shape, NEG_BIG, F32)
            l_ref[...] = jnp.zeros(l_ref.shape, F32)
            acc_ref[...] = jnp.zeros(acc_ref.shape, F32)

    def stages():
        qm = queries()
        state = (m_ref[...], l_ref[...], acc_ref[...])
        nk = k_refs[0].shape[0] // 2
        same_head, _, _ = head_mask(nk)
        for j in range(npp):
            g = raw_scores(qm, k_refs[j][pl.ds(0, nk, stride=2), :],
                           k_refs[j][pl.ds(1, nk, stride=2), :])
            yield
            state = _softmax_step(jnp.where(same_head, g, NEG_BIG),
                                  v_refs[j][...].astype(BF16), *state)
            yield
        m_ref[...], l_ref[...], acc_ref[...] = state

    def final():
        @pl.when(p == nsteps - 1)
        def _():
            pad_to = max(new_rows, LANES)
            k0 = kn_ref[pl.ds(0, new_rows, stride=2), :]
            k1 = kn_ref[pl.ds(1, new_rows, stride=2), :]
            vn = vn_ref[...]
            if pad_to > new_rows:
                zk = jnp.zeros((pad_to - new_rows, ATT_HEAD), F32)
                k0 = jnp.concatenate([k0, zk], axis=0)
                k1 = jnp.concatenate([k1, zk], axis=0)
                vn = jnp.concatenate([vn, jnp.zeros((pad_to - new_rows, HW), F32)], axis=0)
            same, row, col = head_mask(pad_to)
            valid = same & ((col // heads) <= (row % tq)) & (col < new_rows)
            s = jnp.where(valid, raw_scores(queries(), k0, k1), NEG_BIG)
            m, l, acc = _softmax_step(s, vn.astype(BF16), m_ref[...], l_ref[...], acc_ref[...])
            lam = _lambda(lp_ref[...], lam_init)
            sg = sg_ref[...]
            for h in range(heads):
                r0 = slice(h * tq, (h + 1) * tq)
                r1 = slice(R + h * tq, R + (h + 1) * tq)
                o = acc[r0] / _rep(l[r0], HW) - lam * (acc[r1] / _rep(l[r1], HW))
                o = _rms(o, SUBLN_EPS) * sg * (1.0 - lam_init)
                o_ref[:, h * HW:(h + 1) * HW] = o

    return init, stages, final


def _attn_sample_kernel(pt_ref, *refs, **kw):
    init, stages, final = _decode_parts(refs, pl.program_id(1), pl.num_programs(1), **kw)
    init()
    for _ in stages():
        pass
    final()


def _decode_call_parts(d, npp, seq_of, first_page_of):
    Bd, npg = d["page_table"].shape
    M, Wd = d["q"].shape
    tq = M // Bd
    HW = 2 * ATT_HEAD
    heads = Wd // HW
    krows, vrows = d["cache_k"].shape[1], d["cache_v"].shape[1]
    new_rows = tq * heads
    rows = pl.BlockSpec((tq, Wd), lambda *ids: (seq_of(*ids), 0))
    per_seq = lambda *ids: (seq_of(*ids), 0, 0)
    const = lambda *ids: (0, 0)

    def page_spec(nrows, width, j):
        return pl.BlockSpec(
            (None, nrows, width),
            lambda *ids: (ids[-1][seq_of(*ids) * npg + first_page_of(*ids) + j], 0, 0))

    return dict(
        in_specs=([rows] + [page_spec(krows, ATT_HEAD, j) for j in range(npp)]
                  + [page_spec(vrows, HW, j) for j in range(npp)]
                  + [pl.BlockSpec((None, 2 * new_rows, ATT_HEAD), per_seq),
                     pl.BlockSpec((None, new_rows, HW), per_seq),
                     pl.BlockSpec((4, ATT_HEAD), const),
                     pl.BlockSpec((1, HW), const)]),
        args=[d["q"], *([d["cache_k"]] * npp), *([d["cache_v"]] * npp),
              d["k_new"].reshape(Bd, 2 * new_rows, ATT_HEAD), d["v_new"].reshape(Bd, new_rows, HW),
              d["lp"], d["sg"].reshape(1, HW)],
        out_spec=rows,
        out_shape=jax.ShapeDtypeStruct((M, Wd), F32),
        scratch=[pltpu.VMEM((2 * new_rows, LANES), F32),
                 pltpu.VMEM((2 * new_rows, LANES), F32),
                 pltpu.VMEM((2 * new_rows, HW), F32)],
        kw=dict(heads=heads, tq=tq, npp=npp, new_rows=new_rows, lam_init=d["lam_init"]),
    )


def _attn_sample(d):
    Bd, npg = d["page_table"].shape
    npp = max(n for n in range(1, PAGES_PER_STEP + 1) if npg % n == 0)
    parts = _decode_call_parts(d, npp, lambda b, p, pt: b, lambda b, p, pt: p * npp)
    grid_spec = pltpu.PrefetchScalarGridSpec(
        num_scalar_prefetch=1,
        grid=(Bd, npg // npp),
        in_specs=parts["in_specs"],
        out_specs=parts["out_spec"],
        scratch_shapes=parts["scratch"],
    )
    return pl.pallas_call(
        functools.partial(_attn_sample_kernel, **parts["kw"]),
        grid_spec=grid_spec,
        out_shape=parts["out_shape"],
        compiler_params=_cparams("parallel", "arbitrary"),
        name="diff_attn_sample",
    )(d["page_table"].reshape(-1), *parts["args"])


def _rope_tables(pos, reps):
    half = ATT_HEAD // 2
    inv = jnp.power(ROPE_THETA, -jnp.arange(half, dtype=F32) / half)
    ang = pos.astype(F32)[:, None] * inv[None, :]
    cos = jnp.concatenate([jnp.cos(ang), jnp.cos(ang)], axis=-1)
    sin = jnp.concatenate([-jnp.sin(ang), jnp.sin(ang)], axis=-1)
    return jnp.tile(cos, (reps, 1)), jnp.tile(sin, (reps, 1))


def _lambda_init(layer_idx):
    return 0.8 - 0.6 * math.exp(-0.3 * layer_idx)


def kernel(x_prompt, x_sample, cache_k, cache_v, state_shift, state_wkv, page_table, norm_mix, norm_ffn, rwkv_mu, rwkv_vec, rwkv_wr, rwkv_wk, rwkv_wv, rwkv_wo, rwkv_w1, rwkv_w2, rwkv_a1, rwkv_a2, rwkv_v0, rwkv_v1, rwkv_v2, rwkv_g1, rwkv_g2, rwkv_rk, kv_norm, kv_wk, kv_wv, attn_wq, attn_wo, attn_lambda, attn_subln, ffn_w1, ffn_w2):
    depth = norm_mix.shape[0]
    n_a = rwkv_mu.shape[0]
    D = x_prompt.shape[-1]
    Wd = kv_wk.shape[1]
    n_att_heads = Wd // (2 * ATT_HEAD)
    n_rwkv_heads = D // RWKV_HEAD

    w_rkv = [jnp.stack([rwkv_wr[l], rwkv_wk[l], rwkv_wv[l]]).astype(BF16) for l in range(n_a)]
    w_o = [rwkv_wo[l].astype(BF16) for l in range(n_a)]
    lora_w = [_pad_lora(rwkv_w1[l], rwkv_w2[l]) for l in range(n_a)]
    lora_a = [_pad_lora(rwkv_a1[l], rwkv_a2[l]) for l in range(n_a)]
    lora_g = [_pad_lora(rwkv_g1[l], rwkv_g2[l]) for l in range(n_a)]
    lora_v = [_pad_lora(rwkv_v1[l], rwkv_v2[l]) for l in range(n_a - 1)]
    wkv_params = []
    for l in range(n_a):
        v0 = rwkv_v0[l - 1] if l > 0 else jnp.zeros((D,), F32)
        wkv_params.append(jnp.concatenate(
            [rwkv_vec[l], rwkv_rk[l].reshape(1, D), v0.reshape(1, D)], axis=0))
    w_kv = jnp.stack([kv_wk, kv_wv]).astype(BF16)
    w_f1 = _mlp_w1_tiles(ffn_w1.astype(BF16))
    w_f2 = ffn_w2.astype(BF16)
    w_q = [attn_wq[j].astype(BF16) for j in range(depth - n_a)]
    w_ao = [attn_wo[j].astype(BF16) for j in range(depth - n_a)]

    def trunk(x, B, T, shift0, wkv0, rope, attend, small):
        act_dtype = F32 if small else BF16
        new_shift, new_wkv = [], []
        rkv_first = None
        k_sh = v_sh = k_att = v_att = None
        xn_next = None
        for l in range(depth):
            if l < n_a:
                loras = [lora_w[l], lora_a[l], lora_g[l]] + ([lora_v[l - 1]] if l > 0 else [])
                if small:
                    mix, last = _prep(x, shift0[l], norm_mix[l, 0], rwkv_mu[l], B=B, T=T,
                                      out_dtype=act_dtype)
                    rkv = _matmul((mix, PL_R), w_rkv[l])
                    branches = [_lora((mix, plane), w1, w2, act=act)
                                for (plane, act), (w1, w2) in zip(LORA_BRANCHES, loras)]
                else:
                    rkv, hidden, last = _rwkv_in(x, shift0[l], norm_mix[l, 0], rwkv_mu[l], w_rkv[l],
                                                 [w1 for w1, _ in loras], B=B, T=T)
                    branches = [(h, w2) for h, (_, w2) in zip(hidden, loras)]
                new_shift.append(last)
                if l == 0:
                    rkv_first = rkv
                wkv_call = ((rkv, branches, wkv_params[l], None if wkv0 is None else wkv0[l]),
                            dict(B=B, T=T, rkv_first=rkv_first if l > 0 else None,
                                 out_dtype=act_dtype))
                if small:
                    branch, s_new = _wkv(*wkv_call[0], **wkv_call[1])
                else:
                    branch, s_new = yield ("wkv", wkv_call)
                new_wkv.append(s_new.astype(state_wkv.dtype))
                w_out = w_o[l]
            else:
                j = l - n_a
                lam_init = _lambda_init(l)
                q = _matmul(xn_next, w_q[j], out_dtypes=(act_dtype,), rope=rope, out_scale=Q_SCALE)
                if small:
                    branch = yield ("attn", dict(q=q, k_new=k_att, v_new=v_att, lp=attn_lambda[j],
                                                 sg=attn_subln[j], lam_init=lam_init))
                else:
                    branch = attend(q, k_att, v_att, attn_lambda[j], attn_subln[j], lam_init)
                w_out = w_ao[j]
            x, nrm = _resnorm(x, branch, norm_mix[l, 1], [norm_ffn[l, 0]], w=w_out)
            g_next = []
            if l == n_a - 1:
                g_next.append(kv_norm)
            if n_a - 1 <= l < depth - 1:
                g_next.append(norm_mix[l + 1, 0])
            x, nrm = _mlp((nrm, 0), (w_f1, l), (w_f2, l), x, norm_ffn[l, 1], g_next)
            if l == n_a - 1:
                if small:
                    k_sh = k_att = _matmul((nrm, 0), w_kv[0], rope=rope)
                    v_sh = v_att = _matmul((nrm, 0), w_kv[1])
                else:
                    k_sh, k_att = _matmul((nrm, 0), w_kv[0], rope=rope, out_dtypes=(F32, BF16),
                                          chunk_rows=True)
                    v_sh, v_att = _matmul((nrm, 0), w_kv[1], out_dtypes=(F32, BF16))
            if g_next:
                xn_next = (nrm, len(g_next) - 1)
        return x, k_sh, v_sh, jnp.stack(new_shift), jnp.stack(new_wkv)

    Bp, Tp, _ = x_prompt.shape
    shift0_p = jnp.zeros((n_a, Bp, D), F32)
    wkv0_p = None
    rope_p = _rope_tables(jnp.arange(Tp, dtype=jnp.int32), 1)

    def attend_p(q, k, v, lp, sg, lam_init):
        return _attn_prompt(q, k, v, lp, sg, B=Bp, T=Tp, lam_init=lam_init)

    prompt = trunk(x_prompt.reshape(Bp * Tp, D), Bp, Tp, shift0_p, wkv0_p, rope_p, attend_p, False)

    Bd, Td, _ = x_sample.shape
    n_pages = page_table.shape[1]
    page = cache_k.shape[1]
    past_len = n_pages * page
    rope_s = _rope_tables(past_len + jnp.arange(Td, dtype=jnp.int32), Bd)
    ck = cache_k.reshape(cache_k.shape[0], page * n_att_heads * 2, ATT_HEAD)
    cv = cache_v.reshape(cache_v.shape[0], page * n_att_heads, 2 * ATT_HEAD)
    sample = trunk(x_sample.reshape(Bd * Td, D), Bd, Td, state_shift, state_wkv, rope_s, None, True)

    def advance(gen, value):
        try:
            return gen.send(value), None
        except StopIteration as done:
            return None, done.value

    can_ride = _ride_shape(Bp, D, Tp, page_table) is not None
    req_p, out_p = advance(prompt, None)
    req_s, out_s = advance(sample, None)
    while req_p is not None or req_s is not None:
        dec = None
        if req_s is not None:
            dec = dict(req_s[1], cache_k=ck, cache_v=cv, page_table=page_table)
        if req_p is not None and dec is not None and can_ride:
            branch, s_new, o = _wkv(*req_p[1][0], **req_p[1][1], decode=dec)
            req_p, out_p = advance(prompt, (branch, s_new))
            req_s, out_s = advance(sample, o)
        elif req_p is not None:
            req_p, out_p = advance(prompt, _wkv(*req_p[1][0], **req_p[1][1]))
        else:
            req_s, out_s = advance(sample, _attn_sample(dec))
    y_p, k_p, v_p, shift_p, wkv_p = out_p
    y_s, k_s, v_s, shift_s, wkv_s = out_s

    H = n_att_heads
    return (y_p.reshape(Bp, Tp, D), y_s.reshape(Bd, Td, D),
            k_p.reshape(Bp, Tp, H, 2, ATT_HEAD), v_p.reshape(Bp, Tp, H, 2 * ATT_HEAD),
            shift_p, wkv_p,
            k_s.reshape(Bd, Td, H, 2, ATT_HEAD), v_s.reshape(Bd, Td, H, 2 * ATT_HEAD),
            shift_s, wkv_s)
```

```python
import functools
import math

import jax
import jax.numpy as jnp
from jax import lax
from jax.experimental import pallas as pl
from jax.experimental.pallas import tpu as pltpu

F32 = jnp.float32
BF16 = jnp.bfloat16

RWKV_HEAD = 64
ATT_HEAD = 128
NORM_EPS = 1e-6
GN_EPS = 64e-5
SUBLN_EPS = 1e-5
ROPE_THETA = 10000.0

LANES = 128
GROUP = 256
HEADS_PER_GROUP = GROUP // RWKV_HEAD
CHUNK = 64
WKV_GROUPS_PER_STEP = 8
VMEM_LIMIT = 48 * 1024 * 1024
MLP_VMEM_LIMIT = 56 * 1024 * 1024
RIDE_VMEM_LIMIT = 60 * 1024 * 1024
NEG_BIG = -1e30


def _cparams(*sem):
    return pltpu.CompilerParams(dimension_semantics=sem, vmem_limit_bytes=VMEM_LIMIT)


def _dot(a, b):
    return jnp.dot(a, b, preferred_element_type=F32)


def _dot_nt(a, b):
    return lax.dot_general(a, b, (((1,), (1,)), ((), ())), preferred_element_type=F32)


def _rms(x, eps):
    return x * lax.rsqrt(jnp.mean(x * x, axis=-1, keepdims=True) + eps)


def _mm_kernel(x_ref, w_ref, *rest, act, rope, out_scale, chunk_rows):
    acc = _dot(x_ref[...].astype(BF16), w_ref[...])
    if act == "tanh":
        acc = jnp.tanh(acc)
    elif act == "sigmoid":
        acc = jax.nn.sigmoid(acc)
    o_refs = rest[2:] if rope else rest
    if rope:
        cos = rest[0][...] * out_scale
        sin = rest[1][...] * out_scale
        tm = acc.shape[0]
        nchunk = acc.shape[1] // LANES
        for c in range(nchunk):
            blk = acc[:, c * LANES:(c + 1) * LANES]
            res = blk * cos + pltpu.roll(blk, LANES // 2, 1) * sin
            for k, o_ref in enumerate(o_refs):
                if chunk_rows and k == 0:
                    o_ref[pl.ds(c, tm, stride=nchunk), :] = res.astype(o_ref.dtype)
                else:
                    o_ref[:, c * LANES:(c + 1) * LANES] = res.astype(o_ref.dtype)
    else:
        if out_scale != 1.0:
            acc = acc * out_scale
        for o_ref in o_refs:
            o_ref[...] = acc.astype(o_ref.dtype)


def _planes(x):
    return x if isinstance(x, tuple) else (x[None], 0)


MAX_WEIGHT_TILE_BYTES = 8 * 1024 * 1024


def _matmul(x, w, *, out_dtypes=(F32,), act=None, rope=None, out_scale=1.0, chunk_rows=False,
            tm=512):
    x, p0 = _planes(x)
    squeeze = w.ndim == 2
    if squeeze:
        w = w[None]
    _, M, K = x.shape
    G, _, N = w.shape
    tm = min(tm, M)
    if rope is not None:
        tm = min(tm, rope[0].shape[0])
        assert rope[0].shape[0] % tm == 0
    tn = N
    while K * tn * w.dtype.itemsize > MAX_WEIGHT_TILE_BYTES and tn % (2 * LANES) == 0:
        tn //= 2
    assert M % tm == 0 and N % tn == 0
    in_specs = [pl.BlockSpec((None, tm, K), lambda g, i, j: (p0 + g, i, 0)),
                pl.BlockSpec((None, K, tn), lambda g, i, j: (g, 0, j))]
    args = [x, w]
    if rope is not None:
        cos, sin = rope
        nblk = cos.shape[0] // tm
        spec = pl.BlockSpec((tm, LANES), lambda g, i, j: (i % nblk, 0))
        in_specs += [spec, spec]
        args += [cos, sin]
    out_specs = [pl.BlockSpec((None, tm, tn), lambda g, i, j: (g, i, j)) for _ in out_dtypes]
    out_shape = [jax.ShapeDtypeStruct((G, M, N), dt) for dt in out_dtypes]
    if chunk_rows:
        assert rope is not None and tn == N and G == 1
        nchunk = N // LANES
        out_specs[0] = pl.BlockSpec((None, tm * nchunk, LANES), lambda g, i, j: (g, i, 0))
        out_shape[0] = jax.ShapeDtypeStruct((G, M * nchunk, LANES), out_dtypes[0])
    outs = pl.pallas_call(
        functools.partial(_mm_kernel, act=act, rope=rope is not None, out_scale=out_scale,
                          chunk_rows=chunk_rows),
        grid=(G, M // tm, N // tn),
        in_specs=in_specs,
        out_specs=out_specs,
        out_shape=out_shape,
        compiler_params=_cparams("parallel", "parallel", "arbitrary"),
        name="matmul",
    )(*args)
    outs = [o[0] if squeeze else o for o in outs]
    return outs[0] if len(outs) == 1 else outs


def _lora_kernel(x_ref, w1_ref, w2_ref, o_ref, *, act):
    h = _dot(x_ref[...].astype(BF16), w1_ref[...])
    if act == "tanh":
        h = jnp.tanh(h)
    elif act == "sigmoid":
        h = jax.nn.sigmoid(h)
    o_ref[...] = _dot(h.astype(BF16), w2_ref[...])


def _lora(x, w1, w2, *, act=None, tm=512):
    x, p0 = _planes(x)
    _, M, K = x.shape
    R = w1.shape[1]
    N = w2.shape[1]
    tm = min(tm, M)
    return pl.pallas_call(
        functools.partial(_lora_kernel, act=act),
        grid=(M // tm,),
        in_specs=[pl.BlockSpec((None, tm, K), lambda i: (p0, i, 0)),
                  pl.BlockSpec((K, R), lambda i: (0, 0)),
                  pl.BlockSpec((R, N), lambda i: (0, 0))],
        out_specs=pl.BlockSpec((tm, N), lambda i: (i, 0)),
        out_shape=jax.ShapeDtypeStruct((M, N), F32),
        compiler_params=_cparams("parallel"),
        name="lora",
    )(x, w1, w2)


def _pad_lora(w1, w2):
    r = w1.shape[1]
    rp = -(-r // LANES) * LANES
    return (jnp.pad(w1, ((0, 0), (0, rp - r))).astype(BF16),
            jnp.pad(w2, ((0, rp - r), (0, 0))).astype(BF16))


def _mlp_kernel(x_ref, w1_ref, w2_ref, xr_ref, gp_ref, *rest, n_out, nf):
    if n_out:
        gn_ref, xo_ref, no_ref = rest
    else:
        xo_ref, = rest
    f = pl.program_id(1)

    @pl.when(f == 0)
    def _():
        xo_ref[...] = jnp.zeros(xo_ref.shape, F32)

    h = jnp.maximum(_dot(x_ref[...], w1_ref[...]), 0.0)
    xo_ref[...] += _dot((h * h).astype(BF16), w2_ref[...])

    @pl.when(f == nf - 1)
    def _():
        x = xr_ref[...] + _rms(xo_ref[...], NORM_EPS) * gp_ref[...]
        xo_ref[...] = x
        if n_out:
            y = _rms(x, NORM_EPS)
            for j in range(n_out):
                no_ref[j] = (y * gn_ref[j:j + 1, :]).astype(no_ref.dtype)


def _mlp(x, w1, w2, x_res, g_post, g_next, *, tm=512, tf=1024):
    x, p0 = _planes(x)
    w1, l1 = _planes(w1)
    w2, l2 = _planes(w2)
    _, M, D = x.shape
    F = w1.shape[2]
    tm = min(tm, M)
    tf = min(tf, F)
    n_out = len(g_next)
    nf = F // tf
    row = pl.BlockSpec((tm, D), lambda i, f: (i, 0))
    vec = pl.BlockSpec((1, D), lambda i, f: (0, 0))
    in_specs = [pl.BlockSpec((None, tm, D), lambda i, f: (p0, i, 0)),
                pl.BlockSpec((None, D, tf), lambda i, f: (l1, 0, f)),
                pl.BlockSpec((None, tf, D), lambda i, f: (l2, f, 0)),
                row,
                vec]
    args = [x, w1, w2, x_res, g_post.reshape(1, D)]
    out_specs = [row]
    out_shape = [jax.ShapeDtypeStruct((M, D), F32)]
    if n_out:
        in_specs.append(pl.BlockSpec((n_out, D), lambda i, f: (0, 0)))
        args.append(jnp.stack(g_next))
        out_specs.append(pl.BlockSpec((n_out, tm, D), lambda i, f: (0, i, 0)))
        out_shape.append(jax.ShapeDtypeStruct((n_out, M, D), BF16))
    res = pl.pallas_call(
        functools.partial(_mlp_kernel, n_out=n_out, nf=nf),
        grid=(M // tm, nf),
        in_specs=in_specs,
        out_specs=out_specs,
        out_shape=out_shape,
        compiler_params=pltpu.CompilerParams(dimension_semantics=("parallel", "arbitrary"),
                                             vmem_limit_bytes=MLP_VMEM_LIMIT),
        name="mlp",
    )(*args)
    return (res[0], res[1]) if n_out else (res[0], None)


MIX_ORDER = (0, 2, 3, 1, 4, 5)
PL_R, PL_K, PL_V, PL_W, PL_A, PL_G = range(6)


def _prep_kernel(x_ref, sh_ref, g_ref, mu_ref, mix_ref, last_ref, carry_ref):
    i = pl.program_id(1)
    xn = _rms(x_ref[...], NORM_EPS) * g_ref[...]
    tr = xn.shape[0]

    @pl.when(i == 0)
    def _():
        carry_ref[...] = sh_ref[...]

    row = lax.broadcasted_iota(jnp.int32, xn.shape, 0)
    xprev = jnp.where(row == 0, carry_ref[...], pltpu.roll(xn, 1, 0))
    xx = xprev - xn
    for plane, m in enumerate(MIX_ORDER):
        mix_ref[plane] = (xn + xx * mu_ref[m:m + 1, :]).astype(mix_ref.dtype)
    last = xn[tr - 1:tr, :]
    carry_ref[...] = last
    last_ref[...] = last


def _prep(x, shift0, g, mu, *, B, T, out_dtype):
    M, D = x.shape
    tr = min(256, T)
    nt = T // tr
    mix, last = pl.pallas_call(
        _prep_kernel,
        grid=(B, nt),
        in_specs=[pl.BlockSpec((tr, D), lambda b, i: (b * nt + i, 0)),
                  pl.BlockSpec((None, 1, D), lambda b, i: (b, 0, 0)),
                  pl.BlockSpec((1, D), lambda b, i: (0, 0)),
                  pl.BlockSpec((6, D), lambda b, i: (0, 0))],
        out_specs=[pl.BlockSpec((6, tr, D), lambda b, i: (0, b * nt + i, 0)),
                   pl.BlockSpec((None, 1, D), lambda b, i: (b, 0, 0))],
        out_shape=[jax.ShapeDtypeStruct((6, M, D), out_dtype),
                   jax.ShapeDtypeStruct((B, 1, D), F32)],
        scratch_shapes=[pltpu.VMEM((1, D), F32)],
        compiler_params=_cparams("arbitrary", "arbitrary"),
        name="rwkv_prep",
    )(x, shift0.reshape(B, 1, D), g.reshape(1, D), mu)
    return mix, last.reshape(B, D)


N_RKV = 3
MIX_SUB_ROWS = 16


def _rwkv_in_kernel(x_ref, sh_ref, g_ref, mu_ref, w_ref, *rest, nt, n_lora):
    l1_refs = rest[:n_lora]
    rkv_ref = rest[n_lora]
    h_refs = rest[n_lora + 1:2 * n_lora + 1]
    last_ref, mix_ref, carry_ref = rest[2 * n_lora + 1:]
    i = pl.program_id(0)
    g = pl.program_id(1)

    @pl.when(g == 0)
    def _():
        tr = x_ref.shape[0]
        sub = min(MIX_SUB_ROWS, tr)
        row = lax.broadcasted_iota(jnp.int32, (sub, x_ref.shape[1]), 0)

        def chunk(ci, prev_row):
            rows = pl.ds(pl.multiple_of(ci * sub, sub), sub)
            xn = _rms(x_ref[rows, :], NORM_EPS) * g_ref[...]
            xx = jnp.where(row == 0, prev_row, pltpu.roll(xn, 1, 0)) - xn
            for plane, m in enumerate(MIX_ORDER):
                mix_ref[plane, rows, :] = (xn + xx * mu_ref[m:m + 1, :]).astype(mix_ref.dtype)
            return xn[sub - 1:sub, :]

        first = jnp.where(i % nt == 0, sh_ref[...], carry_ref[...])
        last = lax.fori_loop(0, tr // sub, chunk, first)
        carry_ref[...] = last
        last_ref[...] = last

    @pl.when(g < N_RKV)
    def _():
        rkv_ref[...] = _dot(mix_ref[g], w_ref[...])

    @pl.when(g == N_RKV)
    def _():
        for (plane, act), l1_ref, h_ref in zip(LORA_BRANCHES, l1_refs, h_refs):
            h = _dot(mix_ref[plane], l1_ref[...])
            if act == "tanh":
                h = jnp.tanh(h)
            elif act == "sigmoid":
                h = jax.nn.sigmoid(h)
            h_ref[...] = h.astype(h_ref.dtype)


LORA_BRANCHES = ((PL_W, "tanh"), (PL_A, None), (PL_G, "sigmoid"), (PL_V, None))


def _rwkv_in(x, shift0, g, mu, w_rkv, lora1, *, B, T, tm=512):
    M, D = x.shape
    tm = min(tm, T)
    nt = T // tm
    n_lora = len(lora1)
    in_specs = [pl.BlockSpec((tm, D), lambda i, g: (i, 0)),
                pl.BlockSpec((None, 1, D), lambda i, g: (i // nt, 0, 0)),
                pl.BlockSpec((1, D), lambda i, g: (0, 0)),
                pl.BlockSpec((6, D), lambda i, g: (0, 0)),
                pl.BlockSpec((None, D, D), lambda i, g: (jnp.minimum(g, N_RKV - 1), 0, 0))]
    in_specs += [pl.BlockSpec(w.shape, lambda i, g: (0, 0), pipeline_mode=pl.Buffered(1))
                 for w in lora1]
    out_specs = [pl.BlockSpec((None, tm, D), lambda i, g: (jnp.minimum(g, N_RKV - 1), i, 0))]
    out_specs += [pl.BlockSpec((tm, w.shape[1]), lambda i, g: (i, 0)) for w in lora1]
    out_specs.append(pl.BlockSpec((None, 1, D), lambda i, g: (i // nt, 0, 0)))
    out_shape = [jax.ShapeDtypeStruct((N_RKV, M, D), F32)]
    out_shape += [jax.ShapeDtypeStruct((M, w.shape[1]), BF16) for w in lora1]
    out_shape.append(jax.ShapeDtypeStruct((B, 1, D), F32))
    res = pl.pallas_call(
        functools.partial(_rwkv_in_kernel, nt=nt, n_lora=n_lora),
        grid=(M // tm, N_RKV + 1),
        in_specs=in_specs,
        out_specs=out_specs,
        out_shape=out_shape,
        scratch_shapes=[pltpu.VMEM((6, tm, D), BF16), pltpu.VMEM((1, D), F32)],
        compiler_params=pltpu.CompilerParams(dimension_semantics=("arbitrary", "arbitrary"),
                                             vmem_limit_bytes=MLP_VMEM_LIMIT),
        name="rwkv_in",
    )(x, shift0.reshape(B, 1, D), g.reshape(1, D), mu, w_rkv, *lora1)
    return res[0], res[1:1 + n_lora], res[-1].reshape(B, D)


def _resnorm_kernel(x_ref, h_ref, *rest, n_out, project):
    w_ref = None
    if project:
        w_ref, *rest = rest
    gp_ref, *rest = rest
    if n_out:
        gn_ref, xo_ref, no_ref = rest
    else:
        xo_ref, = rest
    tr = x_ref.shape[0]
    halves = 2 if project and tr % 32 == 0 else 1
    bounds = [slice(k * tr // halves, (k + 1) * tr // halves) for k in range(halves)]
    hs = [h_ref[rows, :] for rows in bounds]
    if project:
        hs = [_dot(h.astype(BF16), w_ref[...]) for h in hs]
    for rows, h in zip(bounds, hs):
        x = x_ref[rows, :] + _rms(h, NORM_EPS) * gp_ref[...]
        if n_out:
            y = _rms(x, NORM_EPS)
            for j in range(n_out):
                no_ref[j, rows, :] = (y * gn_ref[j:j + 1, :]).astype(no_ref.dtype)
        xo_ref[rows, :] = x


def _resnorm(x, h, g_post, g_next, w=None):
    M, D = x.shape
    n_out = len(g_next)
    tr = min(512 if w is not None else 256, M)
    row = pl.BlockSpec((tr, D), lambda i: (i, 0))
    vec = pl.BlockSpec((1, D), lambda i: (0, 0))
    in_specs = [row, pl.BlockSpec((tr, h.shape[1]), lambda i: (i, 0))]
    args = [x, h]
    if w is not None:
        in_specs.append(pl.BlockSpec(w.shape, lambda i: (0, 0)))
        args.append(w)
    in_specs.append(vec)
    args.append(g_post.reshape(1, D))
    out_specs = [row]
    out_shape = [jax.ShapeDtypeStruct((M, D), F32)]
    if n_out:
        in_specs.append(pl.BlockSpec((n_out, D), lambda i: (0, 0)))
        args.append(jnp.stack(g_next))
        out_specs.append(pl.BlockSpec((n_out, tr, D), lambda i: (0, i, 0)))
        out_shape.append(jax.ShapeDtypeStruct((n_out, M, D), BF16))
    res = pl.pallas_call(
        functools.partial(_resnorm_kernel, n_out=n_out, project=w is not None),
        grid=(M // tr,),
        in_specs=in_specs,
        out_specs=out_specs,
        out_shape=out_shape,
        compiler_params=_cparams("parallel"),
        name="resnorm",
    )(*args)
    return (res[0], res[1]) if n_out else (res[0], None)


def _norm_kernel(x_ref, g_ref, o_ref):
    o_ref[...] = (_rms(x_ref[...], NORM_EPS) * g_ref[...]).astype(o_ref.dtype)


def _norm(x, g):
    M, D = x.shape
    tr = min(256, M)
    return pl.pallas_call(
        _norm_kernel,
        grid=(M // tr,),
        in_specs=[pl.BlockSpec((tr, D), lambda i: (i, 0)), pl.BlockSpec((1, D), lambda i: (0, 0))],
        out_specs=pl.BlockSpec((tr, D), lambda i: (i, 0)),
        out_shape=jax.ShapeDtypeStruct((M, D), BF16),
        compiler_params=_cparams("parallel"),
        name="norm",
    )(x, g.reshape(1, D))


def _split2(x):
    hi = x.astype(BF16)
    lo = (x - hi.astype(F32)).astype(BF16)
    return hi, lo


def _seg_sum(x, ones_bd):
    hi, lo = _split2(x)
    n = x.shape[0]
    s = _dot(jnp.concatenate([hi, lo], axis=0), ones_bd)
    return s[:n] + s[n:]


def _wkv_kernel(*refs, rows, vres, ng, fused):
    _wkv_body(refs, pl.program_id(2), pl.num_programs(2), rows=rows, vres=vres, ng=ng, fused=fused)


def _wkv_num_inputs(vres, fused):
    return (3 + vres) + (3 + vres) * (2 if fused else 1) + 2


def _wkv_body(refs, c, nc, *, rows, vres, ng, fused, rider=None):
    n_rk = 3 + vres
    nb = 3 + vres
    rk_refs = refs[:n_rk]
    br_refs = refs[n_rk:n_rk + (2 * nb if fused else nb)]
    p_ref, s0_ref, y_ref, so_ref, s_ref = refs[n_rk + len(br_refs):]
    row_head = lax.broadcasted_iota(jnp.int32, (GROUP, GROUP), 0) // RWKV_HEAD
    lane_head = lax.broadcasted_iota(jnp.int32, (GROUP, GROUP), 1) // RWKV_HEAD

    @pl.when(c == 0)
    def _():
        for gi in range(ng):
            s_ref[gi] = jnp.where(row_head == lane_head, s0_ref[gi], 0.0)

    if rider is not None:
        rider[0]()

    gens = []
    hidden = [ref[...] for ref in br_refs[:nb]] if fused else None
    for gi in range(ng):
        lanes = pl.ds(gi * GROUP, GROUP)
        rk = [ref[:, lanes] for ref in rk_refs]
        if fused:
            br = [_dot(h, w2_ref[:, lanes]) for h, w2_ref in zip(hidden, br_refs[nb:])]
        else:
            br = [ref[:, lanes] for ref in br_refs]
        vals = rk[:3] + br[:3] + ([rk[3], br[3]] if vres else []) + [p_ref[:, lanes], s_ref[gi]]
        gens.append(_wkv_group(*vals, rows=rows, vres=vres))
    if rider is not None:
        gens.append(rider[1]())
    outs = [None] * len(gens)
    live = list(range(len(gens)))
    while live:
        for gi in list(live):
            try:
                next(gens[gi])
            except StopIteration as done:
                outs[gi] = done.value
                live.remove(gi)
    for gi, (y, s_new) in enumerate(outs[:ng]):
        y_ref[:, pl.ds(gi * GROUP, GROUP)] = y.astype(y_ref.dtype)
        s_ref[gi] = s_new

    if rider is not None:
        rider[2]()

    @pl.when(c == nc - 1)
    def _():
        for gi in range(ng):
            s = s_ref[gi]
            packed = s
            for h in range(1, HEADS_PER_GROUP):
                packed = jnp.where(row_head == h, pltpu.roll(s, GROUP - h * RWKV_HEAD, 1), packed)
            so_ref[gi] = packed[:, :RWKV_HEAD]


def _wkv_group(r_ref, k_ref, v_ref, wl_ref, al_ref, g_ref, *rest, rows, vres):
    if vres:
        vf_ref, vl_ref, p, s = rest
    else:
        p, s = rest
    L = CHUNK
    W = GROUP
    HG = HEADS_PER_GROUP

    def load(val):
        x = val.astype(F32)
        if rows < L:
            x = jnp.concatenate([x, jnp.zeros((L - rows, W), F32)], axis=0)
        return x

    w0, a0, k_k, k_a = p[0:1], p[1:2], p[2:3], p[3:4]
    lnx_w, lnx_b, rk, v0 = p[4:5], p[5:6], p[6:7], p[7:8]

    lane_head = lax.broadcasted_iota(jnp.int32, (1, W), 1) // RWKV_HEAD
    rr = lax.broadcasted_iota(jnp.int32, (W, W), 0) // RWKV_HEAD
    cc = lax.broadcasted_iota(jnp.int32, (W, W), 1) // RWKV_HEAD
    bd = rr == cc
    ones_bd = jnp.where(bd, 1.0, 0.0).astype(BF16)
    trow = lax.broadcasted_iota(jnp.int32, (L, HG * L), 0)
    tcol = lax.broadcasted_iota(jnp.int32, (L, HG * L), 1) & (L - 1)
    strict = tcol < trow
    incl = tcol <= trow
    rb = lax.broadcasted_iota(jnp.int32, (HG * L, HG * L), 0) // L
    cb = lax.broadcasted_iota(jnp.int32, (HG * L, HG * L), 1) // L
    bd_l = rb == cb

    def ystack(x):
        return jnp.concatenate(
            [jnp.where(lane_head == h, x, 0.0) for h in range(HG)], axis=0).astype(BF16)

    r = load(r_ref)
    k = load(k_ref)
    v = load(v_ref)
    a_sig = jax.nn.sigmoid(load(al_ref) + a0)
    if vres:
        v = v + (load(vf_ref) - v) * jax.nn.sigmoid(load(vl_ref) + v0)
    z = -(load(wl_ref) + w0)
    softplus = jnp.maximum(z, 0.0) + jnp.log(1.0 + jnp.exp(-jnp.abs(z)))
    lw = -jnp.exp(-softplus - 0.5)
    if rows < L:
        trow1 = lax.broadcasted_iota(jnp.int32, (L, W), 0)
        lw = jnp.where(trow1 < rows, lw, 0.0)
    kkr = k * k_k
    nrm2 = _seg_sum(kkr * kkr, ones_bd)
    yield
    nrm = jnp.sqrt(nrm2)
    kk = kkr / jnp.maximum(nrm, 1e-12)
    k2 = k * (1.0 + (a_sig - 1.0) * k_a)
    a = -kk
    b = kk * a_sig

    l1 = lw.astype(BF16)
    l2r = lw - l1.astype(F32)
    l2 = l2r.astype(BF16)
    l3 = (l2r - l2.astype(F32)).astype(BF16)
    tri = (lax.broadcasted_iota(jnp.int32, (L, L), 1)
           <= lax.broadcasted_iota(jnp.int32, (L, L), 0))
    tri = jnp.where(tri, 1.0, 0.0).astype(BF16)
    cs = _dot(tri, jnp.concatenate([l1, l2, l3], axis=1))
    yield
    cum = cs[:, :W] + cs[:, W:2 * W] + cs[:, 2 * W:]
    cl = cum[L - 1:L, :]
    e_c = jnp.exp(cum)
    e_cm = jnp.exp(cum - lw)
    e_nc = jnp.exp(-cum)
    e_lc = jnp.exp(cl - cum)
    g_l = jnp.exp(cl)
    at = a * e_cm
    rt = r * e_c
    bt = b * e_nc
    kt = k2 * e_nc
    bh = b * e_lc
    kh = k2 * e_lc

    x4 = _dot_nt(jnp.concatenate([at, rt], axis=0).astype(BF16),
                 jnp.concatenate([ystack(bt), ystack(kt)], axis=0))
    yield
    n_ab = jnp.where(strict, x4[:L, :HG * L], 0.0)
    m_ak = jnp.where(strict, x4[:L, HG * L:], 0.0)
    m_rb = jnp.where(incl, x4[L:, :HG * L], 0.0)
    m_rk = jnp.where(incl, x4[L:, HG * L:], 0.0)

    def bdiag(x):
        return jnp.where(bd_l, jnp.concatenate([x] * HG, axis=0), 0.0).astype(BF16)

    t_inv = jnp.where(tcol == trow, 1.0, 0.0) + n_ab
    yv = ystack(v)
    makv = _dot(m_ak.astype(BF16), yv)
    pw = _dot(n_ab.astype(BF16), bdiag(n_ab))
    yield
    steps = int(math.log2(L)) - 1
    for it in range(steps):
        rhs = bdiag(pw)
        if it < steps - 1:
            res = _dot(jnp.concatenate([t_inv, pw], axis=0).astype(BF16), rhs)
            yield
            t_inv = t_inv + res[:L]
            pw = res[L:]
        else:
            res = _dot(t_inv.astype(BF16), rhs)
            yield
            t_inv = t_inv + res

    au = _dot(t_inv.astype(BF16), jnp.concatenate([ystack(at), ystack(makv)], axis=1))
    yield
    abar = au[:, :W]
    u0 = au[:, W:]
    y_abar = ystack(abar)
    d1 = _dot(m_rb.astype(BF16), jnp.concatenate([y_abar, ystack(u0)], axis=1))
    d2 = _dot(m_rk.astype(BF16), yv)
    s_bf = s.astype(BF16)
    sa = _dot_nt(s_bf, y_abar)
    yield
    rbar = rt + d1[:, :W]
    o0 = d1[:, W:] + d2
    o = _dot_nt(rbar.astype(BF16), s_bf) + o0
    uv_t = jnp.concatenate([u0, v], axis=0).T
    q_t = _dot(uv_t.astype(BF16), jnp.concatenate([bh, kh], axis=0).astype(BF16))
    s_new = s * g_l + _dot(sa.astype(BF16), ystack(bh)) + jnp.where(bd, q_t, 0.0)
    bonus = _seg_sum(r * k2 * rk, ones_bd) * v
    yield
    inv_n = 1.0 / RWKV_HEAD
    mean = _seg_sum(o, ones_bd) * inv_n
    yield
    dlt = o - mean
    var = _seg_sum(dlt * dlt, ones_bd) * inv_n
    yield
    yn = dlt * lax.rsqrt(var + GN_EPS) * lnx_w + lnx_b
    out = (yn + bonus) * load(g_ref)
    return out[:rows], s_new


def _wkv(rkv, branches, params, s0, *, B, T, rkv_first=None, out_dtype=BF16, decode=None):
    _, M, D = rkv.shape
    fused = isinstance(branches[0], tuple)
    vres = rkv_first is not None
    n_groups = B * D // GROUP
    if s0 is None:
        s0_rep = jnp.zeros((n_groups, GROUP, GROUP), F32)
    else:
        s0_rep = jnp.tile(s0.astype(F32).reshape(n_groups, GROUP, RWKV_HEAD), (1, 1, HEADS_PER_GROUP))
    ng, G, rows, nc = _wkv_grid(D, T)
    mk = (lambda f: f) if decode is None else (lambda f: (lambda b, g, c, pt: f(b, g, c)))

    def plane(pi):
        return pl.BlockSpec((None, rows, ng * GROUP), mk(lambda b, g, c: (pi, b * nc + c, g)))

    tile = pl.BlockSpec((rows, ng * GROUP), mk(lambda b, g, c: (b * nc + c, g)))
    lanes = mk(lambda b, g, c: (0, g))
    state = pl.BlockSpec((ng, GROUP, GROUP), mk(lambda b, g, c: (b * G + g, 0, 0)))
    in_specs = [plane(0), plane(1), plane(2)]
    args = [rkv, rkv, rkv]
    if vres:
        in_specs.append(plane(2))
        args.append(rkv_first)
    if fused:
        in_specs += [pl.BlockSpec((rows, h.shape[1]), mk(lambda b, g, c: (b * nc + c, 0)))
                     for h, _ in branches]
        in_specs += [pl.BlockSpec((w2.shape[0], ng * GROUP), lanes) for _, w2 in branches]
        args += [h for h, _ in branches] + [w2 for _, w2 in branches]
    else:
        in_specs += [tile] * len(branches)
        args += list(branches)
    in_specs += [pl.BlockSpec((8, ng * GROUP), lanes), state]
    args += [params, s0_rep]
    out_specs = [tile, pl.BlockSpec((ng, GROUP, RWKV_HEAD), mk(lambda b, g, c: (b * G + g, 0, 0)))]
    out_shape = [jax.ShapeDtypeStruct((M, D), out_dtype),
                 jax.ShapeDtypeStruct((n_groups, GROUP, RWKV_HEAD), F32)]
    scratch = [pltpu.VMEM((ng, GROUP, GROUP), F32)]
    wkv_kw = dict(rows=rows, vres=vres, ng=ng, fused=fused)
    if decode is None:
        y, s_out = pl.pallas_call(
            functools.partial(_wkv_kernel, **wkv_kw),
            grid=(B, G, nc),
            in_specs=in_specs,
            out_specs=out_specs,
            out_shape=out_shape,
            scratch_shapes=scratch,
            compiler_params=_cparams("parallel", "parallel", "arbitrary"),
            name="wkv7",
        )(*args)
        return y, s_out.reshape(B, D // RWKV_HEAD, RWKV_HEAD, RWKV_HEAD)

    spb, npp = _ride_shape(B, D, T, decode["page_table"])
    step = lambda b, g, c: (b * G + g) * nc + c
    dec = _decode_call_parts(decode, npp, lambda b, g, c, pt: step(b, g, c) // spb,
                             lambda b, g, c, pt: (step(b, g, c) % spb) * npp)
    n_wkv_in, n_dec_in = len(args), len(dec["args"])
    grid_spec = pltpu.PrefetchScalarGridSpec(
        num_scalar_prefetch=1,
        grid=(B, G, nc),
        in_specs=in_specs + dec["in_specs"],
        out_specs=out_specs + [dec["out_spec"]],
        scratch_shapes=scratch + dec["scratch"],
    )
    y, s_out, o = pl.pallas_call(
        functools.partial(_wkv_decode_kernel, n_wkv_in=n_wkv_in, n_dec_in=n_dec_in, G=G, nc=nc,
                          spb=spb, wkv_kw=wkv_kw, dec_kw=dec["kw"]),
        grid_spec=grid_spec,
        out_shape=out_shape + [dec["out_shape"]],
        compiler_params=pltpu.CompilerParams(
            dimension_semantics=("arbitrary", "arbitrary", "arbitrary"),
            vmem_limit_bytes=RIDE_VMEM_LIMIT),
        name="wkv7_decode",
    )(decode["page_table"].reshape(-1), *args, *dec["args"])
    return y, s_out.reshape(B, D // RWKV_HEAD, RWKV_HEAD, RWKV_HEAD), o


def _wkv_grid(D, T):
    ng = max(d for d in range(1, WKV_GROUPS_PER_STEP + 1) if (D // GROUP) % d == 0)
    rows = min(CHUNK, T)
    return ng, D // (ng * GROUP), rows, T // rows


MAX_RIDING_PAGES = 8


def _ride_shape(B, D, T, page_table):
    _, G, _, nc = _wkv_grid(D, T)
    steps = B * G * nc
    Bd, npg = page_table.shape
    if steps % Bd or npg % (steps // Bd) or npg // (steps // Bd) > MAX_RIDING_PAGES:
        return None
    return steps // Bd, npg // (steps // Bd)


def _wkv_decode_kernel(pt_ref, *refs, n_wkv_in, n_dec_in, G, nc, spb, wkv_kw, dec_kw):
    wkv_in = refs[:n_wkv_in]
    dec_in = refs[n_wkv_in:n_wkv_in + n_dec_in]
    y_ref, so_ref, o_ref, s_ref, m_ref, l_ref, acc_ref = refs[n_wkv_in + n_dec_in:]
    c = pl.program_id(2)
    step = (pl.program_id(0) * G + pl.program_id(1)) * nc + c
    rider = _decode_parts(tuple(dec_in) + (o_ref, m_ref, l_ref, acc_ref), step % spb, spb, **dec_kw)
    _wkv_body(tuple(wkv_in) + (y_ref, so_ref, s_ref), c, nc, rider=rider, **wkv_kw)


def _lambda(lp, lam_init):
    return (jnp.exp(jnp.sum(lp[0:1] * lp[1:2], keepdims=True))
            - jnp.exp(jnp.sum(lp[2:3] * lp[3:4], keepdims=True)) + lam_init)


Q_SCALE = ATT_HEAD ** -0.5 * math.log2(math.e)


def _rep(x, n):
    return x if n == LANES else jnp.concatenate([x] * (n // LANES), axis=1)


def _softmax_step(s, vt, m, l, acc, ones=None):
    m_new = jnp.maximum(m, jnp.max(s, axis=-1, keepdims=True))
    alpha = jnp.exp2(m - m_new)
    p = jnp.exp2(s - _rep(m_new, s.shape[1]))
    pb = p.astype(BF16)
    if ones is None:
        l_new = alpha * l + jnp.sum(p, axis=-1, keepdims=True)
    else:
        l_new = alpha * l + _dot(pb, ones)
    acc_new = _rep(alpha, acc.shape[1]) * acc + _dot(pb, vt)
    return m_new, l_new, acc_new


def _attn_prompt_kernel(q_ref, k_ref, v_ref, lp_ref, sg_ref, o_ref, m_ref, l_ref, acc_ref,
                        *, tq, lam_init):
    qi = pl.program_id(2)
    q = q_ref[...]
    qs = (q[:, :ATT_HEAD], q[:, ATT_HEAD:])
    m_ref[...] = jnp.full(m_ref.shape, NEG_BIG, F32)
    l_ref[...] = jnp.zeros(l_ref.shape, F32)
    acc_ref[...] = jnp.zeros(acc_ref.shape, F32)

    def scores(j):
        kt = k_ref[pl.ds(pl.multiple_of(j * tq, tq), tq), :]
        return tuple(_dot_nt(qs[mp], kt[:, mp * ATT_HEAD:(mp + 1) * ATT_HEAD]) for mp in range(2))

    def update(j, s):
        vt = v_ref[pl.ds(pl.multiple_of(j * tq, tq), tq), :]
        for mp in range(2):
            m_ref[mp], l_ref[mp], acc_ref[mp] = _softmax_step(
                s[mp], vt, m_ref[mp], l_ref[mp], acc_ref[mp])

    def body(j, s):
        s_next = scores(j + 1)
        update(j, s)
        return s_next

    s = lax.fori_loop(0, qi, body, scores(0))
    keep = (lax.broadcasted_iota(jnp.int32, (tq, tq), 1)
            <= lax.broadcasted_iota(jnp.int32, (tq, tq), 0))
    update(qi, tuple(jnp.where(keep, x, NEG_BIG) for x in s))
    lam = _lambda(lp_ref[...], lam_init)
    hw = acc_ref.shape[2]
    o = acc_ref[0] * _rep(1.0 / l_ref[0], hw) - lam * (acc_ref[1] * _rep(1.0 / l_ref[1], hw))
    o = _rms(o, SUBLN_EPS) * sg_ref[...] * (1.0 - lam_init)
    o_ref[...] = o.astype(o_ref.dtype)


def _attn_prompt(q, k, v, lp, sg, *, B, T, lam_init, tq=512):
    M, Wd = q.shape
    HW = 2 * ATT_HEAD
    H = Wd // HW
    tq = min(tq, T)
    nq = T // tq
    return pl.pallas_call(
        functools.partial(_attn_prompt_kernel, tq=tq, lam_init=lam_init),
        grid=(B, H, nq),
        in_specs=[pl.BlockSpec((tq, HW), lambda b, h, i: (b * nq + i, h)),
                  pl.BlockSpec((T, HW), lambda b, h, i: (b, h)),
                  pl.BlockSpec((T, HW), lambda b, h, i: (b, h)),
                  pl.BlockSpec((4, ATT_HEAD), lambda b, h, i: (0, 0)),
                  pl.BlockSpec((1, HW), lambda b, h, i: (0, 0))],
        out_specs=pl.BlockSpec((tq, HW), lambda b, h, i: (b * nq + i, h)),
        out_shape=jax.ShapeDtypeStruct((M, Wd), BF16),
        scratch_shapes=[pltpu.VMEM((2, tq, LANES), F32), pltpu.VMEM((2, tq, LANES), F32),
                        pltpu.VMEM((2, tq, HW), F32)],
        compiler_params=_cparams("parallel", "parallel", "arbitrary"),
        name="diff_attn_prompt",
    )(q, k, v, lp, sg.reshape(1, HW))


PAGES_PER_STEP = 4


def _decode_parts(refs, p, nsteps, *, heads, tq, npp, new_rows, lam_init):
    q_ref = refs[0]
    k_refs = refs[1:1 + npp]
    v_refs = refs[1 + npp:1 + 2 * npp]
    kn_ref, vn_ref, lp_ref, sg_ref, o_ref, m_ref, l_ref, acc_ref = refs[1 + 2 * npp:]
    HW = 2 * ATT_HEAD
    R = heads * tq

    def queries():
        q = q_ref[...]
        return [jnp.concatenate([q[:, h * HW + mp * ATT_HEAD:h * HW + (mp + 1) * ATT_HEAD]
                                 for h in range(heads)], axis=0).astype(BF16) for mp in range(2)]

    def raw_scores(qm, k0, k1):
        return jnp.concatenate([_dot_nt(qm[0], k0.astype(BF16)), _dot_nt(qm[1], k1.astype(BF16))],
                               axis=0)

    def head_mask(ncol):
        row = lax.broadcasted_iota(jnp.int32, (2 * R, ncol), 0)
        col = lax.broadcasted_iota(jnp.int32, (2 * R, ncol), 1)
        return (col % heads) == ((row // tq) % heads), row, col

    def init():
        @pl.when(p == 0)
        def _():
            m_ref[...] = jnp.full(m_ref.shape, NEG_BIG, F32)
            l_ref[...] = jnp.zeros(l_ref.shape, F32)
            acc_ref[...] = jnp.zeros(acc_ref.shape, F32)

    def stages():
        qm = queries()
        state = (m_ref[...], l_ref[...], acc_ref[...])
        nk = k_refs[0].shape[0] // 2
        same_head, _, _ = head_mask(nk)
        for j in range(npp):
            g = raw_scores(qm, k_refs[j][pl.ds(0, nk, stride=2), :],
                           k_refs[j][pl.ds(1, nk, stride=2), :])
            yield
            state = _softmax_step(jnp.where(same_head, g, NEG_BIG),
                                  v_refs[j][...].astype(BF16), *state)
            yield
        m_ref[...], l_ref[...], acc_ref[...] = state

    def final():
        @pl.when(p == nsteps - 1)
        def _():
            pad_to = max(new_rows, LANES)
            k0 = kn_ref[pl.ds(0, new_rows, stride=2), :]
            k1 = kn_ref[pl.ds(1, new_rows, stride=2), :]
            vn = vn_ref[...]
            if pad_to > new_rows:
                zk = jnp.zeros((pad_to - new_rows, ATT_HEAD), F32)
                k0 = jnp.concatenate([k0, zk], axis=0)
                k1 = jnp.concatenate([k1, zk], axis=0)
                vn = jnp.concatenate([vn, jnp.zeros((pad_to - new_rows, HW), F32)], axis=0)
            same, row, col = head_mask(pad_to)
            valid = same & ((col // heads) <= (row % tq)) & (col < new_rows)
            s = jnp.where(valid, raw_scores(queries(), k0, k1), NEG_BIG)
            m, l, acc = _softmax_step(s, vn.astype(BF16), m_ref[...], l_ref[...], acc_ref[...])
            lam = _lambda(lp_ref[...], lam_init)
            sg = sg_ref[...]
            for h in range(heads):
                r0 = slice(h * tq, (h + 1) * tq)
                r1 = slice(R + h * tq, R + (h + 1) * tq)
                o = acc[r0] / _rep(l[r0], HW) - lam * (acc[r1] / _rep(l[r1], HW))
                o = _rms(o, SUBLN_EPS) * sg * (1.0 - lam_init)
                o_ref[:, h * HW:(h + 1) * HW] = o

    return init, stages, final


def _attn_sample_kernel(pt_ref, *refs, **kw):
    init, stages, final = _decode_parts(refs, pl.program_id(1), pl.num_programs(1), **kw)
    init()
    for _ in stages():
        pass
    final()


def _decode_call_parts(d, npp, seq_of, first_page_of):
    Bd, npg = d["page_table"].shape
    M, Wd = d["q"].shape
    tq = M // Bd
    HW = 2 * ATT_HEAD
    heads = Wd // HW
    krows, vrows = d["cache_k"].shape[1], d["cache_v"].shape[1]
    new_rows = tq * heads
    rows = pl.BlockSpec((tq, Wd), lambda *ids: (seq_of(*ids), 0))
    per_seq = lambda *ids: (seq_of(*ids), 0, 0)
    const = lambda *ids: (0, 0)

    def page_spec(nrows, width, j):
        return pl.BlockSpec(
            (None, nrows, width),
            lambda *ids: (ids[-1][seq_of(*ids) * npg + first_page_of(*ids) + j], 0, 0))

    return dict(
        in_specs=([rows] + [page_spec(krows, ATT_HEAD, j) for j in range(npp)]
                  + [page_spec(vrows, HW, j) for j in range(npp)]
                  + [pl.BlockSpec((None, 2 * new_rows, ATT_HEAD), per_seq),
                     pl.BlockSpec((None, new_rows, HW), per_seq),
                     pl.BlockSpec((4, ATT_HEAD), const),
                     pl.BlockSpec((1, HW), const)]),
        args=[d["q"], *([d["cache_k"]] * npp), *([d["cache_v"]] * npp),
              d["k_new"].reshape(Bd, 2 * new_rows, ATT_HEAD), d["v_new"].reshape(Bd, new_rows, HW),
              d["lp"], d["sg"].reshape(1, HW)],
        out_spec=rows,
        out_shape=jax.ShapeDtypeStruct((M, Wd), F32),
        scratch=[pltpu.VMEM((2 * new_rows, LANES), F32),
                 pltpu.VMEM((2 * new_rows, LANES), F32),
                 pltpu.VMEM((2 * new_rows, HW), F32)],
        kw=dict(heads=heads, tq=tq, npp=npp, new_rows=new_rows, lam_init=d["lam_init"]),
    )


def _attn_sample(d):
    Bd, npg = d["page_table"].shape
    npp = max(n for n in range(1, PAGES_PER_STEP + 1) if npg % n == 0)
    parts = _decode_call_parts(d, npp, lambda b, p, pt: b, lambda b, p, pt: p * npp)
    grid_spec = pltpu.PrefetchScalarGridSpec(
        num_scalar_prefetch=1,
        grid=(Bd, npg // npp),
        in_specs=parts["in_specs"],
        out_specs=parts["out_spec"],
        scratch_shapes=parts["scratch"],
    )
    return pl.pallas_call(
        functools.partial(_attn_sample_kernel, **parts["kw"]),
        grid_spec=grid_spec,
        out_shape=parts["out_shape"],
        compiler_params=_cparams("parallel", "arbitrary"),
        name="diff_attn_sample",
    )(d["page_table"].reshape(-1), *parts["args"])


def _rope_tables(pos, reps):
    half = ATT_HEAD // 2
    inv = jnp.power(ROPE_THETA, -jnp.arange(half, dtype=F32) / half)
    ang = pos.astype(F32)[:, None] * inv[None, :]
    cos = jnp.concatenate([jnp.cos(ang), jnp.cos(ang)], axis=-1)
    sin = jnp.concatenate([-jnp.sin(ang), jnp.sin(ang)], axis=-1)
    return jnp.tile(cos, (reps, 1)), jnp.tile(sin, (reps, 1))


def _lambda_init(layer_idx):
    return 0.8 - 0.6 * math.exp(-0.3 * layer_idx)


def kernel(x_prompt, x_sample, cache_k, cache_v, state_shift, state_wkv, page_table, norm_mix, norm_ffn, rwkv_mu, rwkv_vec, rwkv_wr, rwkv_wk, rwkv_wv, rwkv_wo, rwkv_w1, rwkv_w2, rwkv_a1, rwkv_a2, rwkv_v0, rwkv_v1, rwkv_v2, rwkv_g1, rwkv_g2, rwkv_rk, kv_norm, kv_wk, kv_wv, attn_wq, attn_wo, attn_lambda, attn_subln, ffn_w1, ffn_w2):
    depth = norm_mix.shape[0]
    n_a = rwkv_mu.shape[0]
    D = x_prompt.shape[-1]
    Wd = kv_wk.shape[1]
    n_att_heads = Wd // (2 * ATT_HEAD)
    n_rwkv_heads = D // RWKV_HEAD

    w_rkv = [jnp.stack([rwkv_wr[l], rwkv_wk[l], rwkv_wv[l]]).astype(BF16) for l in range(n_a)]
    w_o = [rwkv_wo[l].astype(BF16) for l in range(n_a)]
    lora_w = [_pad_lora(rwkv_w1[l], rwkv_w2[l]) for l in range(n_a)]
    lora_a = [_pad_lora(rwkv_a1[l], rwkv_a2[l]) for l in range(n_a)]
    lora_g = [_pad_lora(rwkv_g1[l], rwkv_g2[l]) for l in range(n_a)]
    lora_v = [_pad_lora(rwkv_v1[l], rwkv_v2[l]) for l in range(n_a - 1)]
    wkv_params = []
    for l in range(n_a):
        v0 = rwkv_v0[l - 1] if l > 0 else jnp.zeros((D,), F32)
        wkv_params.append(jnp.concatenate(
            [rwkv_vec[l], rwkv_rk[l].reshape(1, D), v0.reshape(1, D)], axis=0))
    w_kv = jnp.stack([kv_wk, kv_wv]).astype(BF16)
    w_f1 = ffn_w1.astype(BF16)
    w_f2 = ffn_w2.astype(BF16)
    w_q = [attn_wq[j].astype(BF16) for j in range(depth - n_a)]
    w_ao = [attn_wo[j].astype(BF16) for j in range(depth - n_a)]

    def trunk(x, B, T, shift0, wkv0, rope, attend, small):
        act_dtype = F32 if small else BF16
        new_shift, new_wkv = [], []
        rkv_first = None
        k_sh = v_sh = k_att = v_att = None
        xn_next = None
        for l in range(depth):
            if l < n_a:
                loras = [lora_w[l], lora_a[l], lora_g[l]] + ([lora_v[l - 1]] if l > 0 else [])
                if small:
                    mix, last = _prep(x, shift0[l], norm_mix[l, 0], rwkv_mu[l], B=B, T=T,
                                      out_dtype=act_dtype)
                    rkv = _matmul((mix, PL_R), w_rkv[l])
                    branches = [_lora((mix, plane), w1, w2, act=act)
                                for (plane, act), (w1, w2) in zip(LORA_BRANCHES, loras)]
                else:
                    rkv, hidden, last = _rwkv_in(x, shift0[l], norm_mix[l, 0], rwkv_mu[l], w_rkv[l],
                                                 [w1 for w1, _ in loras], B=B, T=T)
                    branches = [(h, w2) for h, (_, w2) in zip(hidden, loras)]
                new_shift.append(last)
                if l == 0:
                    rkv_first = rkv
                wkv_call = ((rkv, branches, wkv_params[l], None if wkv0 is None else wkv0[l]),
                            dict(B=B, T=T, rkv_first=rkv_first if l > 0 else None,
                                 out_dtype=act_dtype))
                if small:
                    branch, s_new = _wkv(*wkv_call[0], **wkv_call[1])
                else:
                    branch, s_new = yield ("wkv", wkv_call)
                new_wkv.append(s_new.astype(state_wkv.dtype))
                w_out = w_o[l]
            else:
                j = l - n_a
                lam_init = _lambda_init(l)
                q = _matmul(xn_next, w_q[j], out_dtypes=(act_dtype,), rope=rope, out_scale=Q_SCALE)
                if small:
                    branch = yield ("attn", dict(q=q, k_new=k_att, v_new=v_att, lp=attn_lambda[j],
                                                 sg=attn_subln[j], lam_init=lam_init))
                else:
                    branch = attend(q, k_att, v_att, attn_lambda[j], attn_subln[j], lam_init)
                w_out = w_ao[j]
            x, nrm = _resnorm(x, branch, norm_mix[l, 1], [norm_ffn[l, 0]], w=w_out)
            g_next = []
            if l == n_a - 1:
                g_next.append(kv_norm)
            if n_a - 1 <= l < depth - 1:
                g_next.append(norm_mix[l + 1, 0])
            x, nrm = _mlp((nrm, 0), (w_f1, l), (w_f2, l), x, norm_ffn[l, 1], g_next)
            if l == n_a - 1:
                if small:
                    k_sh = k_att = _matmul((nrm, 0), w_kv[0], rope=rope)
                    v_sh = v_att = _matmul((nrm, 0), w_kv[1])
                else:
                    k_sh, k_att = _matmul((nrm, 0), w_kv[0], rope=rope, out_dtypes=(F32, BF16),
                                          chunk_rows=True)
                    v_sh, v_att = _matmul((nrm, 0), w_kv[1], out_dtypes=(F32, BF16))
            if g_next:
                xn_next = (nrm, len(g_next) - 1)
        return x, k_sh, v_sh, jnp.stack(new_shift), jnp.stack(new_wkv)

    Bp, Tp, _ = x_prompt.shape
    shift0_p = jnp.zeros((n_a, Bp, D), F32)
    wkv0_p = None
    rope_p = _rope_tables(jnp.arange(Tp, dtype=jnp.int32), 1)

    def attend_p(q, k, v, lp, sg, lam_init):
        return _attn_prompt(q, k, v, lp, sg, B=Bp, T=Tp, lam_init=lam_init)

    prompt = trunk(x_prompt.reshape(Bp * Tp, D), Bp, Tp, shift0_p, wkv0_p, rope_p, attend_p, False)

    Bd, Td, _ = x_sample.shape
    n_pages = page_table.shape[1]
    page = cache_k.shape[1]
    past_len = n_pages * page
    rope_s = _rope_tables(past_len + jnp.arange(Td, dtype=jnp.int32), Bd)
    ck = cache_k.reshape(cache_k.shape[0], page * n_att_heads * 2, ATT_HEAD)
    cv = cache_v.reshape(cache_v.shape[0], page * n_att_heads, 2 * ATT_HEAD)
    sample = trunk(x_sample.reshape(Bd * Td, D), Bd, Td, state_shift, state_wkv, rope_s, None, True)

    def advance(gen, value):
        try:
            return gen.send(value), None
        except StopIteration as done:
            return None, done.value

    can_ride = _ride_shape(Bp, D, Tp, page_table) is not None
    req_p, out_p = advance(prompt, None)
    req_s, out_s = advance(sample, None)
    while req_p is not None or req_s is not None:
        dec = None
        if req_s is not None:
            dec = dict(req_s[1], cache_k=ck, cache_v=cv, page_table=page_table)
        if req_p is not None and dec is not None and can_ride:
            branch, s_new, o = _wkv(*req_p[1][0], **req_p[1][1], decode=dec)
            req_p, out_p = advance(prompt, (branch, s_new))
            req_s, out_s = advance(sample, o)
        elif req_p is not None:
            req_p, out_p = advance(prompt, _wkv(*req_p[1][0], **req_p[1][1]))
        else:
            req_s, out_s = advance(sample, _attn_sample(dec))
    y_p, k_p, v_p, shift_p, wkv_p = out_p
    y_s, k_s, v_s, shift_s, wkv_s = out_s

    H = n_att_heads
    return (y_p.reshape(Bp, Tp, D), y_s.reshape(Bd, Td, D),
            k_p.reshape(Bp, Tp, H, 2, ATT_HEAD), v_p.reshape(Bp, Tp, H, 2 * ATT_HEAD),
            shift_p, wkv_p,
            k_s.reshape(Bd, Td, H, 2, ATT_HEAD), v_s.reshape(Bd, Td, H, 2 * ATT_HEAD),
            shift_s, wkv_s)
```

```python
import functools
import math

import jax
import jax.numpy as jnp
from jax import lax
from jax.experimental import pallas as pl
from jax.experimental.pallas import tpu as pltpu

F32 = jnp.float32
BF16 = jnp.bfloat16

RWKV_HEAD = 64
ATT_HEAD = 128
NORM_EPS = 1e-6
GN_EPS = 64e-5
SUBLN_EPS = 1e-5
ROPE_THETA = 10000.0

LANES = 128
GROUP = 256
HEADS_PER_GROUP = GROUP // RWKV_HEAD
CHUNK = 64
WKV_GROUPS_PER_STEP = 8
VMEM_LIMIT = 48 * 1024 * 1024
MLP_VMEM_LIMIT = 56 * 1024 * 1024
RIDE_VMEM_LIMIT = 60 * 1024 * 1024
NEG_BIG = -1e30


def _cparams(*sem):
    return pltpu.CompilerParams(dimension_semantics=sem, vmem_limit_bytes=VMEM_LIMIT)


def _dot(a, b):
    return jnp.dot(a, b, preferred_element_type=F32)


def _dot_nt(a, b):
    return lax.dot_general(a, b, (((1,), (1,)), ((), ())), preferred_element_type=F32)


def _rms(x, eps):
    return x * lax.rsqrt(jnp.mean(x * x, axis=-1, keepdims=True) + eps)


def _mm_kernel(x_ref, w_ref, *rest, act, rope, out_scale, chunk_rows):
    acc = _dot(x_ref[...].astype(BF16), w_ref[...])
    if act == "tanh":
        acc = jnp.tanh(acc)
    elif act == "sigmoid":
        acc = jax.nn.sigmoid(acc)
    o_refs = rest[2:] if rope else rest
    if rope:
        cos = rest[0][...] * out_scale
        sin = rest[1][...] * out_scale
        tm = acc.shape[0]
        nchunk = acc.shape[1] // LANES
        for c in range(nchunk):
            blk = acc[:, c * LANES:(c + 1) * LANES]
            res = blk * cos + pltpu.roll(blk, LANES // 2, 1) * sin
            for k, o_ref in enumerate(o_refs):
                if chunk_rows and k == 0:
                    o_ref[pl.ds(c, tm, stride=nchunk), :] = res.astype(o_ref.dtype)
                else:
                    o_ref[:, c * LANES:(c + 1) * LANES] = res.astype(o_ref.dtype)
    else:
        if out_scale != 1.0:
            acc = acc * out_scale
        for o_ref in o_refs:
            o_ref[...] = acc.astype(o_ref.dtype)


def _planes(x):
    return x if isinstance(x, tuple) else (x[None], 0)


MAX_WEIGHT_TILE_BYTES = 8 * 1024 * 1024


def _matmul(x, w, *, out_dtypes=(F32,), act=None, rope=None, out_scale=1.0, chunk_rows=False,
            tm=512):
    x, p0 = _planes(x)
    squeeze = w.ndim == 2
    if squeeze:
        w = w[None]
    _, M, K = x.shape
    G, _, N = w.shape
    tm = min(tm, M)
    if rope is not None:
        tm = min(tm, rope[0].shape[0])
        assert rope[0].shape[0] % tm == 0
    tn = N
    while K * tn * w.dtype.itemsize > MAX_WEIGHT_TILE_BYTES and tn % (2 * LANES) == 0:
        tn //= 2
    assert M % tm == 0 and N % tn == 0
    in_specs = [pl.BlockSpec((None, tm, K), lambda g, i, j: (p0 + g, i, 0)),
                pl.BlockSpec((None, K, tn), lambda g, i, j: (g, 0, j))]
    args = [x, w]
    if rope is not None:
        cos, sin = rope
        nblk = cos.shape[0] // tm
        spec = pl.BlockSpec((tm, LANES), lambda g, i, j: (i % nblk, 0))
        in_specs += [spec, spec]
        args += [cos, sin]
    out_specs = [pl.BlockSpec((None, tm, tn), lambda g, i, j: (g, i, j)) for _ in out_dtypes]
    out_shape = [jax.ShapeDtypeStruct((G, M, N), dt) for dt in out_dtypes]
    if chunk_rows:
        assert rope is not None and tn == N and G == 1
        nchunk = N // LANES
        out_specs[0] = pl.BlockSpec((None, tm * nchunk, LANES), lambda g, i, j: (g, i, 0))
        out_shape[0] = jax.ShapeDtypeStruct((G, M * nchunk, LANES), out_dtypes[0])
    outs = pl.pallas_call(
        functools.partial(_mm_kernel, act=act, rope=rope is not None, out_scale=out_scale,
                          chunk_rows=chunk_rows),
        grid=(G, M // tm, N // tn),
        in_specs=in_specs,
        out_specs=out_specs,
        out_shape=out_shape,
        compiler_params=_cparams("parallel", "parallel", "arbitrary"),
        name="matmul",
    )(*args)
    outs = [o[0] if squeeze else o for o in outs]
    return outs[0] if len(outs) == 1 else outs


def _lora_kernel(x_ref, w1_ref, w2_ref, o_ref, *, act):
    h = _dot(x_ref[...].astype(BF16), w1_ref[...])
    if act == "tanh":
        h = jnp.tanh(h)
    elif act == "sigmoid":
        h = jax.nn.sigmoid(h)
    o_ref[...] = _dot(h.astype(BF16), w2_ref[...])


def _lora(x, w1, w2, *, act=None, tm=512):
    x, p0 = _planes(x)
    _, M, K = x.shape
    R = w1.shape[1]
    N = w2.shape[1]
    tm = min(tm, M)
    return pl.pallas_call(
        functools.partial(_lora_kernel, act=act),
        grid=(M // tm,),
        in_specs=[pl.BlockSpec((None, tm, K), lambda i: (p0, i, 0)),
                  pl.BlockSpec((K, R), lambda i: (0, 0)),
                  pl.BlockSpec((R, N), lambda i: (0, 0))],
        out_specs=pl.BlockSpec((tm, N), lambda i: (i, 0)),
        out_shape=jax.ShapeDtypeStruct((M, N), F32),
        compiler_params=_cparams("parallel"),
        name="lora",
    )(x, w1, w2)


def _pad_lora(w1, w2):
    r = w1.shape[1]
    rp = -(-r // LANES) * LANES
    return (jnp.pad(w1, ((0, 0), (0, rp - r))).astype(BF16),
            jnp.pad(w2, ((0, rp - r), (0, 0))).astype(BF16))


def _mlp_kernel(x_ref, w1_ref, w2_ref, xr_ref, gp_ref, *rest, n_out, nf):
    if n_out:
        gn_ref, xo_ref, no_ref = rest
    else:
        xo_ref, = rest
    f = pl.program_id(1)

    @pl.when(f == 0)
    def _():
        xo_ref[...] = jnp.zeros(xo_ref.shape, F32)

    h = jnp.maximum(_dot(x_ref[...], w1_ref[...]), 0.0)
    xo_ref[...] += _dot((h * h).astype(BF16), w2_ref[...])

    @pl.when(f == nf - 1)
    def _():
        x = xr_ref[...] + _rms(xo_ref[...], NORM_EPS) * gp_ref[...]
        xo_ref[...] = x
        if n_out:
            y = _rms(x, NORM_EPS)
            for j in range(n_out):
                no_ref[j] = (y * gn_ref[j:j + 1, :]).astype(no_ref.dtype)


def _mlp(x, w1, w2, x_res, g_post, g_next, *, tm=512, tf=1024):
    x, p0 = _planes(x)
    w1, l1 = _planes(w1)
    w2, l2 = _planes(w2)
    _, M, D = x.shape
    F = w1.shape[2]
    tm = min(tm, M)
    tf = min(tf, F)
    n_out = len(g_next)
    nf = F // tf
    row = pl.BlockSpec((tm, D), lambda i, f: (i, 0))
    vec = pl.BlockSpec((1, D), lambda i, f: (0, 0))
    in_specs = [pl.BlockSpec((None, tm, D), lambda i, f: (p0, i, 0)),
                pl.BlockSpec((None, D, tf), lambda i, f: (l1, 0, f)),
                pl.BlockSpec((None, tf, D), lambda i, f: (l2, f, 0)),
                row,
                vec]
    args = [x, w1, w2, x_res, g_post.reshape(1, D)]
    out_specs = [row]
    out_shape = [jax.ShapeDtypeStruct((M, D), F32)]
    if n_out:
        in_specs.append(pl.BlockSpec((n_out, D), lambda i, f: (0, 0)))
        args.append(jnp.stack(g_next))
        out_specs.append(pl.BlockSpec((n_out, tm, D), lambda i, f: (0, i, 0)))
        out_shape.append(jax.ShapeDtypeStruct((n_out, M, D), BF16))
    res = pl.pallas_call(
        functools.partial(_mlp_kernel, n_out=n_out, nf=nf),
        grid=(M // tm, nf),
        in_specs=in_specs,
        out_specs=out_specs,
        out_shape=out_shape,
        compiler_params=pltpu.CompilerParams(dimension_semantics=("parallel", "arbitrary"),
                                             vmem_limit_bytes=MLP_VMEM_LIMIT),
        name="mlp",
    )(*args)
    return (res[0], res[1]) if n_out else (res[0], None)


MIX_ORDER = (0, 2, 3, 1, 4, 5)
PL_R, PL_K, PL_V, PL_W, PL_A, PL_G = range(6)


def _prep_kernel(x_ref, sh_ref, g_ref, mu_ref, mix_ref, last_ref, carry_ref):
    i = pl.program_id(1)
    xn = _rms(x_ref[...], NORM_EPS) * g_ref[...]
    tr = xn.shape[0]

    @pl.when(i == 0)
    def _():
        carry_ref[...] = sh_ref[...]

    row = lax.broadcasted_iota(jnp.int32, xn.shape, 0)
    xprev = jnp.where(row == 0, carry_ref[...], pltpu.roll(xn, 1, 0))
    xx = xprev - xn
    for plane, m in enumerate(MIX_ORDER):
        mix_ref[plane] = (xn + xx * mu_ref[m:m + 1, :]).astype(mix_ref.dtype)
    last = xn[tr - 1:tr, :]
    carry_ref[...] = last
    last_ref[...] = last


def _prep(x, shift0, g, mu, *, B, T, out_dtype):
    M, D = x.shape
    tr = min(256, T)
    nt = T // tr
    mix, last = pl.pallas_call(
        _prep_kernel,
        grid=(B, nt),
        in_specs=[pl.BlockSpec((tr, D), lambda b, i: (b * nt + i, 0)),
                  pl.BlockSpec((None, 1, D), lambda b, i: (b, 0, 0)),
                  pl.BlockSpec((1, D), lambda b, i: (0, 0)),
                  pl.BlockSpec((6, D), lambda b, i: (0, 0))],
        out_specs=[pl.BlockSpec((6, tr, D), lambda b, i: (0, b * nt + i, 0)),
                   pl.BlockSpec((None, 1, D), lambda b, i: (b, 0, 0))],
        out_shape=[jax.ShapeDtypeStruct((6, M, D), out_dtype),
                   jax.ShapeDtypeStruct((B, 1, D), F32)],
        scratch_shapes=[pltpu.VMEM((1, D), F32)],
        compiler_params=_cparams("arbitrary", "arbitrary"),
        name="rwkv_prep",
    )(x, shift0.reshape(B, 1, D), g.reshape(1, D), mu)
    return mix, last.reshape(B, D)


N_RKV = 3
MIX_SUB_ROWS = 16


def _rwkv_in_kernel(x_ref, sh_ref, g_ref, mu_ref, w_ref, *rest, nt, n_lora):
    l1_refs = rest[:n_lora]
    rkv_ref = rest[n_lora]
    h_refs = rest[n_lora + 1:2 * n_lora + 1]
    last_ref, mix_ref, carry_ref = rest[2 * n_lora + 1:]
    i = pl.program_id(0)
    g = pl.program_id(1)

    @pl.when(g == 0)
    def _():
        tr = x_ref.shape[0]
        sub = min(MIX_SUB_ROWS, tr)
        row = lax.broadcasted_iota(jnp.int32, (sub, x_ref.shape[1]), 0)

        def chunk(ci, prev_row):
            rows = pl.ds(pl.multiple_of(ci * sub, sub), sub)
            xn = _rms(x_ref[rows, :], NORM_EPS) * g_ref[...]
            xx = jnp.where(row == 0, prev_row, pltpu.roll(xn, 1, 0)) - xn
            for plane, m in enumerate(MIX_ORDER):
                mix_ref[plane, rows, :] = (xn + xx * mu_ref[m:m + 1, :]).astype(mix_ref.dtype)
            return xn[sub - 1:sub, :]

        first = jnp.where(i % nt == 0, sh_ref[...], carry_ref[...])
        last = lax.fori_loop(0, tr // sub, chunk, first)
        carry_ref[...] = last
        last_ref[...] = last

    @pl.when(g > 0)
    def _():
        rkv_ref[...] = _dot(mix_ref[g - 1], w_ref[...])

    @pl.when(g == 0)
    def _():
        for (plane, act), l1_ref, h_ref in zip(LORA_BRANCHES, l1_refs, h_refs):
            h = _dot(mix_ref[plane], l1_ref[...])
            if act == "tanh":
                h = jnp.tanh(h)
            elif act == "sigmoid":
                h = jax.nn.sigmoid(h)
            h_ref[...] = h.astype(h_ref.dtype)


LORA_BRANCHES = ((PL_W, "tanh"), (PL_A, None), (PL_G, "sigmoid"), (PL_V, None))


def _rwkv_in(x, shift0, g, mu, w_rkv, lora1, *, B, T, tm=512):
    M, D = x.shape
    tm = min(tm, T)
    nt = T // tm
    n_lora = len(lora1)
    in_specs = [pl.BlockSpec((tm, D), lambda i, g: (i, 0)),
                pl.BlockSpec((None, 1, D), lambda i, g: (i // nt, 0, 0)),
                pl.BlockSpec((1, D), lambda i, g: (0, 0)),
                pl.BlockSpec((6, D), lambda i, g: (0, 0)),
                pl.BlockSpec((None, D, D), lambda i, g: (jnp.maximum(g - 1, 0), 0, 0))]
    in_specs += [pl.BlockSpec(w.shape, lambda i, g: (0, 0), pipeline_mode=pl.Buffered(1))
                 for w in lora1]
    out_specs = [pl.BlockSpec((None, tm, D), lambda i, g: (jnp.maximum(g - 1, 0), i, 0))]
    out_specs += [pl.BlockSpec((tm, w.shape[1]), lambda i, g: (i, 0)) for w in lora1]
    out_specs.append(pl.BlockSpec((None, 1, D), lambda i, g: (i // nt, 0, 0)))
    out_shape = [jax.ShapeDtypeStruct((N_RKV, M, D), F32)]
    out_shape += [jax.ShapeDtypeStruct((M, w.shape[1]), BF16) for w in lora1]
    out_shape.append(jax.ShapeDtypeStruct((B, 1, D), F32))
    res = pl.pallas_call(
        functools.partial(_rwkv_in_kernel, nt=nt, n_lora=n_lora),
        grid=(M // tm, N_RKV + 1),
        in_specs=in_specs,
        out_specs=out_specs,
        out_shape=out_shape,
        scratch_shapes=[pltpu.VMEM((6, tm, D), BF16), pltpu.VMEM((1, D), F32)],
        compiler_params=pltpu.CompilerParams(dimension_semantics=("arbitrary", "arbitrary"),
                                             vmem_limit_bytes=MLP_VMEM_LIMIT),
        name="rwkv_in",
    )(x, shift0.reshape(B, 1, D), g.reshape(1, D), mu, w_rkv, *lora1)
    return res[0], res[1:1 + n_lora], res[-1].reshape(B, D)


def _resnorm_kernel(x_ref, h_ref, *rest, n_out, project):
    w_ref = None
    if project:
        w_ref, *rest = rest
    gp_ref, *rest = rest
    if n_out:
        gn_ref, xo_ref, no_ref = rest
    else:
        xo_ref, = rest
    tr = x_ref.shape[0]
    halves = 2 if project and tr % 32 == 0 else 1
    bounds = [slice(k * tr // halves, (k + 1) * tr // halves) for k in range(halves)]
    hs = [h_ref[rows, :] for rows in bounds]
    if project:
        hs = [_dot(h.astype(BF16), w_ref[...]) for h in hs]
    for rows, h in zip(bounds, hs):
        x = x_ref[rows, :] + _rms(h, NORM_EPS) * gp_ref[...]
        if n_out:
            y = _rms(x, NORM_EPS)
            for j in range(n_out):
                no_ref[j, rows, :] = (y * gn_ref[j:j + 1, :]).astype(no_ref.dtype)
        xo_ref[rows, :] = x


def _resnorm(x, h, g_post, g_next, w=None):
    M, D = x.shape
    n_out = len(g_next)
    tr = min(512 if w is not None else 256, M)
    row = pl.BlockSpec((tr, D), lambda i: (i, 0))
    vec = pl.BlockSpec((1, D), lambda i: (0, 0))
    in_specs = [row, pl.BlockSpec((tr, h.shape[1]), lambda i: (i, 0))]
    args = [x, h]
    if w is not None:
        in_specs.append(pl.BlockSpec(w.shape, lambda i: (0, 0)))
        args.append(w)
    in_specs.append(vec)
    args.append(g_post.reshape(1, D))
    out_specs = [row]
    out_shape = [jax.ShapeDtypeStruct((M, D), F32)]
    if n_out:
        in_specs.append(pl.BlockSpec((n_out, D), lambda i: (0, 0)))
        args.append(jnp.stack(g_next))
        out_specs.append(pl.BlockSpec((n_out, tr, D), lambda i: (0, i, 0)))
        out_shape.append(jax.ShapeDtypeStruct((n_out, M, D), BF16))
    res = pl.pallas_call(
        functools.partial(_resnorm_kernel, n_out=n_out, project=w is not None),
        grid=(M // tr,),
        in_specs=in_specs,
        out_specs=out_specs,
        out_shape=out_shape,
        compiler_params=_cparams("parallel"),
        name="resnorm",
    )(*args)
    return (res[0], res[1]) if n_out else (res[0], None)


def _norm_kernel(x_ref, g_ref, o_ref):
    o_ref[...] = (_rms(x_ref[...], NORM_EPS) * g_ref[...]).astype(o_ref.dtype)


def _norm(x, g):
    M, D = x.shape
    tr = min(256, M)
    return pl.pallas_call(
        _norm_kernel,
        grid=(M // tr,),
        in_specs=[pl.BlockSpec((tr, D), lambda i: (i, 0)), pl.BlockSpec((1, D), lambda i: (0, 0))],
        out_specs=pl.BlockSpec((tr, D), lambda i: (i, 0)),
        out_shape=jax.ShapeDtypeStruct((M, D), BF16),
        compiler_params=_cparams("parallel"),
        name="norm",
    )(x, g.reshape(1, D))


def _split2(x):
    hi = x.astype(BF16)
    lo = (x - hi.astype(F32)).astype(BF16)
    return hi, lo


def _seg_sum(x, ones_bd):
    hi, lo = _split2(x)
    n = x.shape[0]
    s = _dot(jnp.concatenate([hi, lo], axis=0), ones_bd)
    return s[:n] + s[n:]


def _wkv_kernel(*refs, rows, vres, ng, fused):
    _wkv_body(refs, pl.program_id(2), pl.num_programs(2), rows=rows, vres=vres, ng=ng, fused=fused)


def _wkv_num_inputs(vres, fused):
    return (3 + vres) + (3 + vres) * (2 if fused else 1) + 2


def _wkv_body(refs, c, nc, *, rows, vres, ng, fused, rider=None):
    n_rk = 3 + vres
    nb = 3 + vres
    rk_refs = refs[:n_rk]
    br_refs = refs[n_rk:n_rk + (2 * nb if fused else nb)]
    p_ref, s0_ref, y_ref, so_ref, s_ref = refs[n_rk + len(br_refs):]
    row_head = lax.broadcasted_iota(jnp.int32, (GROUP, GROUP), 0) // RWKV_HEAD
    lane_head = lax.broadcasted_iota(jnp.int32, (GROUP, GROUP), 1) // RWKV_HEAD

    @pl.when(c == 0)
    def _():
        for gi in range(ng):
            s_ref[gi] = jnp.where(row_head == lane_head, s0_ref[gi], 0.0)

    if rider is not None:
        rider[0]()

    gens = []
    hidden = [ref[...] for ref in br_refs[:nb]] if fused else None
    for gi in range(ng):
        lanes = pl.ds(gi * GROUP, GROUP)
        rk = [ref[:, lanes] for ref in rk_refs]
        if fused:
            br = [_dot(h, w2_ref[:, lanes]) for h, w2_ref in zip(hidden, br_refs[nb:])]
        else:
            br = [ref[:, lanes] for ref in br_refs]
        vals = rk[:3] + br[:3] + ([rk[3], br[3]] if vres else []) + [p_ref[:, lanes], s_ref[gi]]
        gens.append(_wkv_group(*vals, rows=rows, vres=vres))
    if rider is not None:
        gens.append(rider[1]())
    outs = [None] * len(gens)
    live = list(range(len(gens)))
    while live:
        for gi in list(live):
            try:
                next(gens[gi])
            except StopIteration as done:
                outs[gi] = done.value
                live.remove(gi)
    for gi, (y, s_new) in enumerate(outs[:ng]):
        y_ref[:, pl.ds(gi * GROUP, GROUP)] = y.astype(y_ref.dtype)
        s_ref[gi] = s_new

    if rider is not None:
        rider[2]()

    @pl.when(c == nc - 1)
    def _():
        for gi in range(ng):
            s = s_ref[gi]
            packed = s
            for h in range(1, HEADS_PER_GROUP):
                packed = jnp.where(row_head == h, pltpu.roll(s, GROUP - h * RWKV_HEAD, 1), packed)
            so_ref[gi] = packed[:, :RWKV_HEAD]


def _wkv_group(r_ref, k_ref, v_ref, wl_ref, al_ref, g_ref, *rest, rows, vres):
    if vres:
        vf_ref, vl_ref, p, s = rest
    else:
        p, s = rest
    L = CHUNK
    W = GROUP
    HG = HEADS_PER_GROUP

    def load(val):
        x = val.astype(F32)
        if rows < L:
            x = jnp.concatenate([x, jnp.zeros((L - rows, W), F32)], axis=0)
        return x

    w0, a0, k_k, k_a = p[0:1], p[1:2], p[2:3], p[3:4]
    lnx_w, lnx_b, rk, v0 = p[4:5], p[5:6], p[6:7], p[7:8]

    lane_head = lax.broadcasted_iota(jnp.int32, (1, W), 1) // RWKV_HEAD
    rr = lax.broadcasted_iota(jnp.int32, (W, W), 0) // RWKV_HEAD
    cc = lax.broadcasted_iota(jnp.int32, (W, W), 1) // RWKV_HEAD
    bd = rr == cc
    ones_bd = jnp.where(bd, 1.0, 0.0).astype(BF16)
    trow = lax.broadcasted_iota(jnp.int32, (L, HG * L), 0)
    tcol = lax.broadcasted_iota(jnp.int32, (L, HG * L), 1) & (L - 1)
    strict = tcol < trow
    incl = tcol <= trow
    rb = lax.broadcasted_iota(jnp.int32, (HG * L, HG * L), 0) // L
    cb = lax.broadcasted_iota(jnp.int32, (HG * L, HG * L), 1) // L
    bd_l = rb == cb

    def ystack(x):
        return jnp.concatenate(
            [jnp.where(lane_head == h, x, 0.0) for h in range(HG)], axis=0).astype(BF16)

    r = load(r_ref)
    k = load(k_ref)
    v = load(v_ref)
    a_sig = jax.nn.sigmoid(load(al_ref) + a0)
    if vres:
        v = v + (load(vf_ref) - v) * jax.nn.sigmoid(load(vl_ref) + v0)
    z = -(load(wl_ref) + w0)
    softplus = jnp.maximum(z, 0.0) + jnp.log(1.0 + jnp.exp(-jnp.abs(z)))
    lw = -jnp.exp(-softplus - 0.5)
    if rows < L:
        trow1 = lax.broadcasted_iota(jnp.int32, (L, W), 0)
        lw = jnp.where(trow1 < rows, lw, 0.0)
    kkr = k * k_k
    nrm2 = _seg_sum(kkr * kkr, ones_bd)
    yield
    nrm = jnp.sqrt(nrm2)
    kk = kkr / jnp.maximum(nrm, 1e-12)
    k2 = k * (1.0 + (a_sig - 1.0) * k_a)
    a = -kk
    b = kk * a_sig

    l1 = lw.astype(BF16)
    l2r = lw - l1.astype(F32)
    l2 = l2r.astype(BF16)
    l3 = (l2r - l2.astype(F32)).astype(BF16)
    tri = (lax.broadcasted_iota(jnp.int32, (L, L), 1)
           <= lax.broadcasted_iota(jnp.int32, (L, L), 0))
    tri = jnp.where(tri, 1.0, 0.0).astype(BF16)
    cs = _dot(tri, jnp.concatenate([l1, l2, l3], axis=1))
    yield
    cum = cs[:, :W] + cs[:, W:2 * W] + cs[:, 2 * W:]
    cl = cum[L - 1:L, :]
    e_c = jnp.exp(cum)
    e_cm = jnp.exp(cum - lw)
    e_nc = jnp.exp(-cum)
    e_lc = jnp.exp(cl - cum)
    g_l = jnp.exp(cl)
    at = a * e_cm
    rt = r * e_c
    bt = b * e_nc
    kt = k2 * e_nc
    bh = b * e_lc
    kh = k2 * e_lc

    x4 = _dot_nt(jnp.concatenate([at, rt], axis=0).astype(BF16),
                 jnp.concatenate([ystack(bt), ystack(kt)], axis=0))
    yield
    n_ab = jnp.where(strict, x4[:L, :HG * L], 0.0)
    m_ak = jnp.where(strict, x4[:L, HG * L:], 0.0)
    m_rb = jnp.where(incl, x4[L:, :HG * L], 0.0)
    m_rk = jnp.where(incl, x4[L:, HG * L:], 0.0)

    def bdiag(x):
        return jnp.where(bd_l, jnp.concatenate([x] * HG, axis=0), 0.0).astype(BF16)

    t_inv = jnp.where(tcol == trow, 1.0, 0.0) + n_ab
    yv = ystack(v)
    makv = _dot(m_ak.astype(BF16), yv)
    pw = _dot(n_ab.astype(BF16), bdiag(n_ab))
    yield
    steps = int(math.log2(L)) - 1
    for it in range(steps):
        rhs = bdiag(pw)
        if it < steps - 1:
            res = _dot(jnp.concatenate([t_inv, pw], axis=0).astype(BF16), rhs)
            yield
            t_inv = t_inv + res[:L]
            pw = res[L:]
        else:
            res = _dot(t_inv.astype(BF16), rhs)
            yield
            t_inv = t_inv + res

    au = _dot(t_inv.astype(BF16), jnp.concatenate([ystack(at), ystack(makv)], axis=1))
    yield
    abar = au[:, :W]
    u0 = au[:, W:]
    y_abar = ystack(abar)
    d1 = _dot(m_rb.astype(BF16), jnp.concatenate([y_abar, ystack(u0)], axis=1))
    d2 = _dot(m_rk.astype(BF16), yv)
    s_bf = s.astype(BF16)
    sa = _dot_nt(s_bf, y_abar)
    yield
    rbar = rt + d1[:, :W]
    o0 = d1[:, W:] + d2
    o = _dot_nt(rbar.astype(BF16), s_bf) + o0
    uv_t = jnp.concatenate([u0, v], axis=0).T
    q_t = _dot(uv_t.astype(BF16), jnp.concatenate([bh, kh], axis=0).astype(BF16))
    s_new = s * g_l + _dot(sa.astype(BF16), ystack(bh)) + jnp.where(bd, q_t, 0.0)
    bonus = _seg_sum(r * k2 * rk, ones_bd) * v
    yield
    inv_n = 1.0 / RWKV_HEAD
    mean = _seg_sum(o, ones_bd) * inv_n
    yield
    dlt = o - mean
    var = _seg_sum(dlt * dlt, ones_bd) * inv_n
    yield
    yn = dlt * lax.rsqrt(var + GN_EPS) * lnx_w + lnx_b
    out = (yn + bonus) * load(g_ref)
    return out[:rows], s_new


def _wkv(rkv, branches, params, s0, *, B, T, rkv_first=None, out_dtype=BF16, decode=None):
    _, M, D = rkv.shape
    fused = isinstance(branches[0], tuple)
    vres = rkv_first is not None
    n_groups = B * D // GROUP
    if s0 is None:
        s0_rep = jnp.zeros((n_groups, GROUP, GROUP), F32)
    else:
        s0_rep = jnp.tile(s0.astype(F32).reshape(n_groups, GROUP, RWKV_HEAD), (1, 1, HEADS_PER_GROUP))
    ng, G, rows, nc = _wkv_grid(D, T)
    mk = (lambda f: f) if decode is None else (lambda f: (lambda b, g, c, pt: f(b, g, c)))

    def plane(pi):
        return pl.BlockSpec((None, rows, ng * GROUP), mk(lambda b, g, c: (pi, b * nc + c, g)))

    tile = pl.BlockSpec((rows, ng * GROUP), mk(lambda b, g, c: (b * nc + c, g)))
    lanes = mk(lambda b, g, c: (0, g))
    state = pl.BlockSpec((ng, GROUP, GROUP), mk(lambda b, g, c: (b * G + g, 0, 0)))
    in_specs = [plane(0), plane(1), plane(2)]
    args = [rkv, rkv, rkv]
    if vres:
        in_specs.append(plane(2))
        args.append(rkv_first)
    if fused:
        in_specs += [pl.BlockSpec((rows, h.shape[1]), mk(lambda b, g, c: (b * nc + c, 0)))
                     for h, _ in branches]
        in_specs += [pl.BlockSpec((w2.shape[0], ng * GROUP), lanes) for _, w2 in branches]
        args += [h for h, _ in branches] + [w2 for _, w2 in branches]
    else:
        in_specs += [tile] * len(branches)
        args += list(branches)
    in_specs += [pl.BlockSpec((8, ng * GROUP), lanes), state]
    args += [params, s0_rep]
    out_specs = [tile, pl.BlockSpec((ng, GROUP, RWKV_HEAD), mk(lambda b, g, c: (b * G + g, 0, 0)))]
    out_shape = [jax.ShapeDtypeStruct((M, D), out_dtype),
                 jax.ShapeDtypeStruct((n_groups, GROUP, RWKV_HEAD), F32)]
    scratch = [pltpu.VMEM((ng, GROUP, GROUP), F32)]
    wkv_kw = dict(rows=rows, vres=vres, ng=ng, fused=fused)
    if decode is None:
        y, s_out = pl.pallas_call(
            functools.partial(_wkv_kernel, **wkv_kw),
            grid=(B, G, nc),
            in_specs=in_specs,
            out_specs=out_specs,
            out_shape=out_shape,
            scratch_shapes=scratch,
            compiler_params=_cparams("parallel", "parallel", "arbitrary"),
            name="wkv7",
        )(*args)
        return y, s_out.reshape(B, D // RWKV_HEAD, RWKV_HEAD, RWKV_HEAD)

    spb, npp = _ride_shape(B, D, T, decode["page_table"])
    step = lambda b, g, c: (b * G + g) * nc + c
    dec = _decode_call_parts(decode, npp, lambda b, g, c, pt: step(b, g, c) // spb,
                             lambda b, g, c, pt: (step(b, g, c) % spb) * npp)
    n_wkv_in, n_dec_in = len(args), len(dec["args"])
    grid_spec = pltpu.PrefetchScalarGridSpec(
        num_scalar_prefetch=1,
        grid=(B, G, nc),
        in_specs=in_specs + dec["in_specs"],
        out_specs=out_specs + [dec["out_spec"]],
        scratch_shapes=scratch + dec["scratch"],
    )
    y, s_out, o = pl.pallas_call(
        functools.partial(_wkv_decode_kernel, n_wkv_in=n_wkv_in, n_dec_in=n_dec_in, G=G, nc=nc,
                          spb=spb, wkv_kw=wkv_kw, dec_kw=dec["kw"]),
        grid_spec=grid_spec,
        out_shape=out_shape + [dec["out_shape"]],
        compiler_params=pltpu.CompilerParams(
            dimension_semantics=("arbitrary", "arbitrary", "arbitrary"),
            vmem_limit_bytes=RIDE_VMEM_LIMIT),
        name="wkv7_decode",
    )(decode["page_table"].reshape(-1), *args, *dec["args"])
    return y, s_out.reshape(B, D // RWKV_HEAD, RWKV_HEAD, RWKV_HEAD), o


def _wkv_grid(D, T):
    ng = max(d for d in range(1, WKV_GROUPS_PER_STEP + 1) if (D // GROUP) % d == 0)
    rows = min(CHUNK, T)
    return ng, D // (ng * GROUP), rows, T // rows


MAX_RIDING_PAGES = 8


def _ride_shape(B, D, T, page_table):
    _, G, _, nc = _wkv_grid(D, T)
    steps = B * G * nc
    Bd, npg = page_table.shape
    if steps % Bd or npg % (steps // Bd) or npg // (steps // Bd) > MAX_RIDING_PAGES:
        return None
    return steps // Bd, npg // (steps // Bd)


def _wkv_decode_kernel(pt_ref, *refs, n_wkv_in, n_dec_in, G, nc, spb, wkv_kw, dec_kw):
    wkv_in = refs[:n_wkv_in]
    dec_in = refs[n_wkv_in:n_wkv_in + n_dec_in]
    y_ref, so_ref, o_ref, s_ref, m_ref, l_ref, acc_ref = refs[n_wkv_in + n_dec_in:]
    c = pl.program_id(2)
    step = (pl.program_id(0) * G + pl.program_id(1)) * nc + c
    rider = _decode_parts(tuple(dec_in) + (o_ref, m_ref, l_ref, acc_ref), step % spb, spb, **dec_kw)
    _wkv_body(tuple(wkv_in) + (y_ref, so_ref, s_ref), c, nc, rider=rider, **wkv_kw)


def _lambda(lp, lam_init):
    return (jnp.exp(jnp.sum(lp[0:1] * lp[1:2], keepdims=True))
            - jnp.exp(jnp.sum(lp[2:3] * lp[3:4], keepdims=True)) + lam_init)


Q_SCALE = ATT_HEAD ** -0.5 * math.log2(math.e)


def _rep(x, n):
    return x if n == LANES else jnp.concatenate([x] * (n // LANES), axis=1)


def _softmax_step(s, vt, m, l, acc, ones=None):
    m_new = jnp.maximum(m, jnp.max(s, axis=-1, keepdims=True))
    alpha = jnp.exp2(m - m_new)
    p = jnp.exp2(s - _rep(m_new, s.shape[1]))
    pb = p.astype(BF16)
    if ones is None:
        l_new = alpha * l + jnp.sum(p, axis=-1, keepdims=True)
    else:
        l_new = alpha * l + _dot(pb, ones)
    acc_new = _rep(alpha, acc.shape[1]) * acc + _dot(pb, vt)
    return m_new, l_new, acc_new


def _attn_prompt_kernel(q_ref, k_ref, v_ref, lp_ref, sg_ref, o_ref, m_ref, l_ref, acc_ref,
                        *, tq, lam_init):
    qi = pl.program_id(2)
    q = q_ref[...]
    qs = (q[:, :ATT_HEAD], q[:, ATT_HEAD:])
    m_ref[...] = jnp.full(m_ref.shape, NEG_BIG, F32)
    l_ref[...] = jnp.zeros(l_ref.shape, F32)
    acc_ref[...] = jnp.zeros(acc_ref.shape, F32)

    def scores(j):
        kt = k_ref[pl.ds(pl.multiple_of(j * tq, tq), tq), :]
        return tuple(_dot_nt(qs[mp], kt[:, mp * ATT_HEAD:(mp + 1) * ATT_HEAD]) for mp in range(2))

    def update(j, s):
        vt = v_ref[pl.ds(pl.multiple_of(j * tq, tq), tq), :]
        for mp in range(2):
            m_ref[mp], l_ref[mp], acc_ref[mp] = _softmax_step(
                s[mp], vt, m_ref[mp], l_ref[mp], acc_ref[mp])

    def body(j, s):
        s_next = scores(j + 1)
        update(j, s)
        return s_next

    s = lax.fori_loop(0, qi, body, scores(0))
    keep = (lax.broadcasted_iota(jnp.int32, (tq, tq), 1)
            <= lax.broadcasted_iota(jnp.int32, (tq, tq), 0))
    update(qi, tuple(jnp.where(keep, x, NEG_BIG) for x in s))
    lam = _lambda(lp_ref[...], lam_init)
    hw = acc_ref.shape[2]
    o = acc_ref[0] * _rep(1.0 / l_ref[0], hw) - lam * (acc_ref[1] * _rep(1.0 / l_ref[1], hw))
    o = _rms(o, SUBLN_EPS) * sg_ref[...] * (1.0 - lam_init)
    o_ref[...] = o.astype(o_ref.dtype)


def _attn_prompt(q, k, v, lp, sg, *, B, T, lam_init, tq=512):
    M, Wd = q.shape
    HW = 2 * ATT_HEAD
    H = Wd // HW
    tq = min(tq, T)
    nq = T // tq
    return pl.pallas_call(
        functools.partial(_attn_prompt_kernel, tq=tq, lam_init=lam_init),
        grid=(B, H, nq),
        in_specs=[pl.BlockSpec((tq, HW), lambda b, h, i: (b * nq + i, h)),
                  pl.BlockSpec((T, HW), lambda b, h, i: (b, h)),
                  pl.BlockSpec((T, HW), lambda b, h, i: (b, h)),
                  pl.BlockSpec((4, ATT_HEAD), lambda b, h, i: (0, 0)),
                  pl.BlockSpec((1, HW), lambda b, h, i: (0, 0))],
        out_specs=pl.BlockSpec((tq, HW), lambda b, h, i: (b * nq + i, h)),
        out_shape=jax.ShapeDtypeStruct((M, Wd), BF16),
        scratch_shapes=[pltpu.VMEM((2, tq, LANES), F32), pltpu.VMEM((2, tq, LANES), F32),
                        pltpu.VMEM((2, tq, HW), F32)],
        compiler_params=_cparams("parallel", "parallel", "arbitrary"),
        name="diff_attn_prompt",
    )(q, k, v, lp, sg.reshape(1, HW))


PAGES_PER_STEP = 4


def _decode_parts(refs, p, nsteps, *, heads, tq, npp, new_rows, lam_init):
    q_ref = refs[0]
    k_refs = refs[1:1 + npp]
    v_refs = refs[1 + npp:1 + 2 * npp]
    kn_ref, vn_ref, lp_ref, sg_ref, o_ref, m_ref, l_ref, acc_ref = refs[1 + 2 * npp:]
    HW = 2 * ATT_HEAD
    R = heads * tq

    def queries():
        q = q_ref[...]
        return [jnp.concatenate([q[:, h * HW + mp * ATT_HEAD:h * HW + (mp + 1) * ATT_HEAD]
                                 for h in range(heads)], axis=0).astype(BF16) for mp in range(2)]

    def raw_scores(qm, k0, k1):
        return jnp.concatenate([_dot_nt(qm[0], k0.astype(BF16)), _dot_nt(qm[1], k1.astype(BF16))],
                               axis=0)

    def head_mask(ncol):
        row = lax.broadcasted_iota(jnp.int32, (2 * R, ncol), 0)
        col = lax.broadcasted_iota(jnp.int32, (2 * R, ncol), 1)
        return (col % heads) == ((row // tq) % heads), row, col

    def init():
        @pl.when(p == 0)
        def _():
            m_ref[...] = jnp.full(m_ref.shape, NEG_BIG, F32)
            l_ref[...] = jnp.zeros(l_ref.shape, F32)
            acc_ref[...] = jnp.zeros(acc_ref.shape, F32)

    def stages():
        qm = queries()
        state = (m_ref[...], l_ref[...], acc_ref[...])
        nk = k_refs[0].shape[0] // 2
        same_head, _, _ = head_mask(nk)
        for j in range(npp):
            g = raw_scores(qm, k_refs[j][pl.ds(0, nk, stride=2), :],
                           k_refs[j][pl.ds(1, nk, stride=2), :])
            yield
            state = _softmax_step(jnp.where(same_head, g, NEG_BIG),
                                  v_refs[j][...].astype(BF16), *state)
            yield
        m_ref[...], l_ref[...], acc_ref[...] = state

    def final():
        @pl.when(p == nsteps - 1)
        def _():
            pad_to = max(new_rows, LANES)
            k0 = kn_ref[pl.ds(0, new_rows, stride=2), :]
            k1 = kn_ref[pl.ds(1, new_rows, stride=2), :]
            vn = vn_ref[...]
            if pad_to > new_rows:
                zk = jnp.zeros((pad_to - new_rows, ATT_HEAD), F32)
                k0 = jnp.concatenate([k0, zk], axis=0)
                k1 = jnp.concatenate([k1, zk], axis=0)
                vn = jnp.concatenate([vn, jnp.zeros((pad_to - new_rows, HW), F32)], axis=0)
            same, row, col = head_mask(pad_to)
            valid = same & ((col // heads) <= (row % tq)) & (col < new_rows)
            s = jnp.where(valid, raw_scores(queries(), k0, k1), NEG_BIG)
            m, l, acc = _softmax_step(s, vn.astype(BF16), m_ref[...], l_ref[...], acc_ref[...])
            lam = _lambda(lp_ref[...], lam_init)
            sg = sg_ref[...]
            for h in range(heads):
                r0 = slice(h * tq, (h + 1) * tq)
                r1 = slice(R + h * tq, R + (h + 1) * tq)
                o = acc[r0] / _rep(l[r0], HW) - lam * (acc[r1] / _rep(l[r1], HW))
                o = _rms(o, SUBLN_EPS) * sg * (1.0 - lam_init)
                o_ref[:, h * HW:(h + 1) * HW] = o

    return init, stages, final


def _attn_sample_kernel(pt_ref, *refs, **kw):
    init, stages, final = _decode_parts(refs, pl.program_id(1), pl.num_programs(1), **kw)
    init()
    for _ in stages():
        pass
    final()


def _decode_call_parts(d, npp, seq_of, first_page_of):
    Bd, npg = d["page_table"].shape
    M, Wd = d["q"].shape
    tq = M // Bd
    HW = 2 * ATT_HEAD
    heads = Wd // HW
    krows, vrows = d["cache_k"].shape[1], d["cache_v"].shape[1]
    new_rows = tq * heads
    rows = pl.BlockSpec((tq, Wd), lambda *ids: (seq_of(*ids), 0))
    per_seq = lambda *ids: (seq_of(*ids), 0, 0)
    const = lambda *ids: (0, 0)

    def page_spec(nrows, width, j):
        return pl.BlockSpec(
            (None, nrows, width),
            lambda *ids: (ids[-1][seq_of(*ids) * npg + first_page_of(*ids) + j], 0, 0))

    return dict(
        in_specs=([rows] + [page_spec(krows, ATT_HEAD, j) for j in range(npp)]
                  + [page_spec(vrows, HW, j) for j in range(npp)]
                  + [pl.BlockSpec((None, 2 * new_rows, ATT_HEAD), per_seq),
                     pl.BlockSpec((None, new_rows, HW), per_seq),
                     pl.BlockSpec((4, ATT_HEAD), const),
                     pl.BlockSpec((1, HW), const)]),
        args=[d["q"], *([d["cache_k"]] * npp), *([d["cache_v"]] * npp),
              d["k_new"].reshape(Bd, 2 * new_rows, ATT_HEAD), d["v_new"].reshape(Bd, new_rows, HW),
              d["lp"], d["sg"].reshape(1, HW)],
        out_spec=rows,
        out_shape=jax.ShapeDtypeStruct((M, Wd), F32),
        scratch=[pltpu.VMEM((2 * new_rows, LANES), F32),
                 pltpu.VMEM((2 * new_rows, LANES), F32),
                 pltpu.VMEM((2 * new_rows, HW), F32)],
        kw=dict(heads=heads, tq=tq, npp=npp, new_rows=new_rows, lam_init=d["lam_init"]),
    )


def _attn_sample(d):
    Bd, npg = d["page_table"].shape
    npp = max(n for n in range(1, PAGES_PER_STEP + 1) if npg % n == 0)
    parts = _decode_call_parts(d, npp, lambda b, p, pt: b, lambda b, p, pt: p * npp)
    grid_spec = pltpu.PrefetchScalarGridSpec(
        num_scalar_prefetch=1,
        grid=(Bd, npg // npp),
        in_specs=parts["in_specs"],
        out_specs=parts["out_spec"],
        scratch_shapes=parts["scratch"],
    )
    return pl.pallas_call(
        functools.partial(_attn_sample_kernel, **parts["kw"]),
        grid_spec=grid_spec,
        out_shape=parts["out_shape"],
        compiler_params=_cparams("parallel", "arbitrary"),
        name="diff_attn_sample",
    )(d["page_table"].reshape(-1), *parts["args"])


def _rope_tables(pos, reps):
    half = ATT_HEAD // 2
    inv = jnp.power(ROPE_THETA, -jnp.arange(half, dtype=F32) / half)
    ang = pos.astype(F32)[:, None] * inv[None, :]
    cos = jnp.concatenate([jnp.cos(ang), jnp.cos(ang)], axis=-1)
    sin = jnp.concatenate([-jnp.sin(ang), jnp.sin(ang)], axis=-1)
    return jnp.tile(cos, (reps, 1)), jnp.tile(sin, (reps, 1))


def _lambda_init(layer_idx):
    return 0.8 - 0.6 * math.exp(-0.3 * layer_idx)


def kernel(x_prompt, x_sample, cache_k, cache_v, state_shift, state_wkv, page_table, norm_mix, norm_ffn, rwkv_mu, rwkv_vec, rwkv_wr, rwkv_wk, rwkv_wv, rwkv_wo, rwkv_w1, rwkv_w2, rwkv_a1, rwkv_a2, rwkv_v0, rwkv_v1, rwkv_v2, rwkv_g1, rwkv_g2, rwkv_rk, kv_norm, kv_wk, kv_wv, attn_wq, attn_wo, attn_lambda, attn_subln, ffn_w1, ffn_w2):
    depth = norm_mix.shape[0]
    n_a = rwkv_mu.shape[0]
    D = x_prompt.shape[-1]
    Wd = kv_wk.shape[1]
    n_att_heads = Wd // (2 * ATT_HEAD)
    n_rwkv_heads = D // RWKV_HEAD

    w_rkv = [jnp.stack([rwkv_wr[l], rwkv_wk[l], rwkv_wv[l]]).astype(BF16) for l in range(n_a)]
    w_o = [rwkv_wo[l].astype(BF16) for l in range(n_a)]
    lora_w = [_pad_lora(rwkv_w1[l], rwkv_w2[l]) for l in range(n_a)]
    lora_a = [_pad_lora(rwkv_a1[l], rwkv_a2[l]) for l in range(n_a)]
    lora_g = [_pad_lora(rwkv_g1[l], rwkv_g2[l]) for l in range(n_a)]
    lora_v = [_pad_lora(rwkv_v1[l], rwkv_v2[l]) for l in range(n_a - 1)]
    wkv_params = []
    for l in range(n_a):
        v0 = rwkv_v0[l - 1] if l > 0 else jnp.zeros((D,), F32)
        wkv_params.append(jnp.concatenate(
            [rwkv_vec[l], rwkv_rk[l].reshape(1, D), v0.reshape(1, D)], axis=0))
    w_kv = jnp.stack([kv_wk, kv_wv]).astype(BF16)
    w_f1 = ffn_w1.astype(BF16)
    w_f2 = ffn_w2.astype(BF16)
    w_q = [attn_wq[j].astype(BF16) for j in range(depth - n_a)]
    w_ao = [attn_wo[j].astype(BF16) for j in range(depth - n_a)]

    def trunk(x, B, T, shift0, wkv0, rope, attend, small):
        act_dtype = F32 if small else BF16
        new_shift, new_wkv = [], []
        rkv_first = None
        k_sh = v_sh = k_att = v_att = None
        xn_next = None
        for l in range(depth):
            if l < n_a:
                loras = [lora_w[l], lora_a[l], lora_g[l]] + ([lora_v[l - 1]] if l > 0 else [])
                if small:
                    mix, last = _prep(x, shift0[l], norm_mix[l, 0], rwkv_mu[l], B=B, T=T,
                                      out_dtype=act_dtype)
                    rkv = _matmul((mix, PL_R), w_rkv[l])
                    branches = [_lora((mix, plane), w1, w2, act=act)
                                for (plane, act), (w1, w2) in zip(LORA_BRANCHES, loras)]
                else:
                    rkv, hidden, last = _rwkv_in(x, shift0[l], norm_mix[l, 0], rwkv_mu[l], w_rkv[l],
                                                 [w1 for w1, _ in loras], B=B, T=T)
                    branches = [(h, w2) for h, (_, w2) in zip(hidden, loras)]
                new_shift.append(last)
                if l == 0:
                    rkv_first = rkv
                wkv_call = ((rkv, branches, wkv_params[l], None if wkv0 is None else wkv0[l]),
                            dict(B=B, T=T, rkv_first=rkv_first if l > 0 else None,
                                 out_dtype=act_dtype))
                if small:
                    branch, s_new = _wkv(*wkv_call[0], **wkv_call[1])
                else:
                    branch, s_new = yield ("wkv", wkv_call)
                new_wkv.append(s_new.astype(state_wkv.dtype))
                w_out = w_o[l]
            else:
                j = l - n_a
                lam_init = _lambda_init(l)
                q = _matmul(xn_next, w_q[j], out_dtypes=(act_dtype,), rope=rope, out_scale=Q_SCALE)
                if small:
                    branch = yield ("attn", dict(q=q, k_new=k_att, v_new=v_att, lp=attn_lambda[j],
                                                 sg=attn_subln[j], lam_init=lam_init))
                else:
                    branch = attend(q, k_att, v_att, attn_lambda[j], attn_subln[j], lam_init)
                w_out = w_ao[j]
            x, nrm = _resnorm(x, branch, norm_mix[l, 1], [norm_ffn[l, 0]], w=w_out)
            g_next = []
            if l == n_a - 1:
                g_next.append(kv_norm)
            if n_a - 1 <= l < depth - 1:
                g_next.append(norm_mix[l + 1, 0])
            x, nrm = _mlp((nrm, 0), (w_f1, l), (w_f2, l), x, norm_ffn[l, 1], g_next)
            if l == n_a - 1:
                if small:
                    k_sh = k_att = _matmul((nrm, 0), w_kv[0], rope=rope)
                    v_sh = v_att = _matmul((nrm, 0), w_kv[1])
                else:
                    k_sh, k_att = _matmul((nrm, 0), w_kv[0], rope=rope, out_dtypes=(F32, BF16),
                                          chunk_rows=True)
                    v_sh, v_att = _matmul((nrm, 0), w_kv[1], out_dtypes=(F32, BF16))
            if g_next:
                xn_next = (nrm, len(g_next) - 1)
        return x, k_sh, v_sh, jnp.stack(new_shift), jnp.stack(new_wkv)

    Bp, Tp, _ = x_prompt.shape
    shift0_p = jnp.zeros((n_a, Bp, D), F32)
    wkv0_p = None
    rope_p = _rope_tables(jnp.arange(Tp, dtype=jnp.int32), 1)

    def attend_p(q, k, v, lp, sg, lam_init):
        return _attn_prompt(q, k, v, lp, sg, B=Bp, T=Tp, lam_init=lam_init)

    prompt = trunk(x_prompt.reshape(Bp * Tp, D), Bp, Tp, shift0_p, wkv0_p, rope_p, attend_p, False)

    Bd, Td, _ = x_sample.shape
    n_pages = page_table.shape[1]
    page = cache_k.shape[1]
    past_len = n_pages * page
    rope_s = _rope_tables(past_len + jnp.arange(Td, dtype=jnp.int32), Bd)
    ck = cache_k.reshape(cache_k.shape[0], page * n_att_heads * 2, ATT_HEAD)
    cv = cache_v.reshape(cache_v.shape[0], page * n_att_heads, 2 * ATT_HEAD)
    sample = trunk(x_sample.reshape(Bd * Td, D), Bd, Td, state_shift, state_wkv, rope_s, None, True)

    def advance(gen, value):
        try:
            return gen.send(value), None
        except StopIteration as done:
            return None, done.value

    can_ride = _ride_shape(Bp, D, Tp, page_table) is not None
    req_p, out_p = advance(prompt, None)
    req_s, out_s = advance(sample, None)
    while req_p is not None or req_s is not None:
        dec = None
        if req_s is not None:
            dec = dict(req_s[1], cache_k=ck, cache_v=cv, page_table=page_table)
        if req_p is not None and dec is not None and can_ride:
            branch, s_new, o = _wkv(*req_p[1][0], **req_p[1][1], decode=dec)
            req_p, out_p = advance(prompt, (branch, s_new))
            req_s, out_s = advance(sample, o)
        elif req_p is not None:
            req_p, out_p = advance(prompt, _wkv(*req_p[1][0], **req_p[1][1]))
        else:
            req_s, out_s = advance(sample, _attn_sample(dec))
    y_p, k_p, v_p, shift_p, wkv_p = out_p
    y_s, k_s, v_s, shift_s, wkv_s = out_s

    H = n_att_heads
    return (y_p.reshape(Bp, Tp, D), y_s.reshape(Bd, Td, D),
            k_p.reshape(Bp, Tp, H, 2, ATT_HEAD), v_p.reshape(Bp, Tp, H, 2 * ATT_HEAD),
            shift_p, wkv_p,
            k_s.reshape(Bd, Td, H, 2, ATT_HEAD), v_s.reshape(Bd, Td, H, 2 * ATT_HEAD),
            shift_s, wkv_s)
```
